```python
import math
import jax, jax.numpy as jnp
from jax import lax
import numpy as np

D_MODEL = 1024
BATCH = 16
SEQ = 2048
DEPTH = 4

N_A_LAYERS = DEPTH // 2
N_B_LAYERS = DEPTH - N_A_LAYERS
HEAD_DIM = 64
A_HEADS = D_MODEL // (2 * HEAD_DIM)
B_KV_HEADS = D_MODEL // HEAD_DIM
B_WINDOWS = (128, 512, 2048)
B_DILATIONS = (1, 4, 16)
N_B_GROUPS = len(B_WINDOWS)
ROPE_THETA = 500000.0
ROPE_DIM = HEAD_DIM // 4
N_EXPERTS = 32
TOP_K = 4
D_FF = D_MODEL
SWIGLU_LIMIT = 7.0
SWIGLU_ALPHA = 1.702
Q_BLOCK = 128
EXPERT_BLOCK = 256
NORM_EPS = 1e-5

kernel_name = 'yoco_diffattn_dilated_moe_adaln'


def rms_norm(x, gain):
    x32 = x.astype(jnp.float32)
    y = x32 * lax.rsqrt(jnp.mean(x32 * x32, axis=-1, keepdims=True) + NORM_EPS)
    return y.astype(x.dtype) * gain


def modulate(x, gain, shift, scale):
    return rms_norm(x, gain) * (1 + scale[:, None, :]) + shift[:, None, :]


def rope_tables(seq, dtype):
    inv = ROPE_THETA ** (-jnp.arange(0, ROPE_DIM, 2, dtype=jnp.float32) / ROPE_DIM)
    ang = jnp.arange(seq, dtype=jnp.float32)[:, None] * inv[None, :]
    return jnp.cos(ang).astype(dtype), jnp.sin(ang).astype(dtype)


def partial_rope(x, cos, sin):
    shape = (cos.shape[0],) + (1,) * (x.ndim - 3) + (cos.shape[1],)
    cos = cos.reshape(shape)
    sin = sin.reshape(shape)
    half = ROPE_DIM // 2
    x1 = x[..., :half]
    x2 = x[..., half:ROPE_DIM]
    return jnp.concatenate([x1 * cos - x2 * sin, x2 * cos + x1 * sin, x[..., ROPE_DIM:]], axis=-1)


def differential_attention(h, wqkv, wo, lam_vecs, subln, cos, sin, lambda_init):
    B, S, D = h.shape
    q, k, v = jnp.split(h @ wqkv, 3, axis=-1)
    q = partial_rope(q.reshape(B, S, A_HEADS, 2, HEAD_DIM), cos, sin) * (HEAD_DIM ** -0.5)
    k = partial_rope(k.reshape(B, S, A_HEADS, 2, HEAD_DIM), cos, sin)
    v = v.reshape(B, S, A_HEADS, 2 * HEAD_DIM)
    lf = lam_vecs.astype(jnp.float32)
    lam = jnp.exp(jnp.sum(lf[0] * lf[1])) - jnp.exp(jnp.sum(lf[2] * lf[3])) + lambda_init
    n_blk = S // Q_BLOCK
    qb = q.reshape(B, n_blk, Q_BLOCK, A_HEADS, 2, HEAD_DIM).transpose(1, 0, 2, 3, 4, 5)
    key_pos = jnp.arange(S)

    def block(args):
        i, qi = args
        q_pos = i * Q_BLOCK + jnp.arange(Q_BLOCK)
        causal = q_pos[:, None] >= key_pos[None, :]
        s = jnp.einsum('bqhmd,bkhmd->bhmqk', qi, k).astype(jnp.float32)
        p = jax.nn.softmax(jnp.where(causal, s, -jnp.inf), axis=-1)
        a = p[:, :, 0] - lam * p[:, :, 1]
        return jnp.einsum('bhqk,bkhd->bqhd', a.astype(v.dtype), v)

    o = lax.map(block, (jnp.arange(n_blk), qb))
    o = o.transpose(1, 0, 2, 3, 4).reshape(B, S, A_HEADS, 2 * HEAD_DIM)
    o = rms_norm(o, subln) * (1.0 - lambda_init)
    return o.reshape(B, S, D) @ wo


def dilated_group(q, k, v, window, dilation):
    B, S, H, Dh = q.shape
    steps = window // dilation
    L = S // dilation
    n_blk = -(-L // steps)
    Lp = n_blk * steps

    def strided(t):
        t = t.reshape(B, L, dilation, H, Dh).transpose(0, 2, 1, 3, 4)
        t = jnp.pad(t, ((0, 0), (0, 0), (0, Lp - L), (0, 0), (0, 0)))
        return t.reshape(B, dilation, n_blk, steps, H, Dh)

    def band(t):
        prev = jnp.pad(t, ((0, 0), (0, 0), (1, 0), (0, 0), (0, 0), (0, 0)))[:, :, :-1]
        return jnp.concatenate([prev, t], axis=3)

    qs = strided(q)
    ks = band(strided(k))
    vs = band(strided(v))
    s = jnp.einsum('brnqhd,brnkhd->brnhqk', qs, ks).astype(jnp.float32)
    dist = (jnp.arange(steps)[:, None] + steps) - jnp.arange(2 * steps)[None, :]
    blk = jnp.arange(n_blk)[:, None, None]
    key_idx = blk * steps - steps + jnp.arange(2 * steps)[None, None, :]
    valid = (dist >= 0) & (dist <= steps) & (key_idx >= 0)
    s = jnp.where(valid[None, None, :, None], s, -jnp.inf)
    m = jnp.max(s, axis=-1, keepdims=True)
    p = jnp.exp(s - m)
    l = jnp.sum(p, axis=-1)
    o = jnp.einsum('brnhqk,brnkhd->brnqhd', p.astype(v.dtype), vs).astype(jnp.float32)
    o = o / jnp.swapaxes(l, -1, -2)[..., None]
    lse = jnp.swapaxes(m[..., 0] + jnp.log(l), -1, -2)

    def unstrided(t):
        t = t.reshape((B, dilation, Lp) + t.shape[4:])[:, :, :L]
        t = jnp.swapaxes(t, 1, 2)
        return t.reshape((B, S) + t.shape[3:])

    return unstrided(o), unstrided(lse)


def dilated_attention(h, wq, wo, k_sh, v_sh, cos, sin):
    B, S, D = h.shape
    q = (h @ wq).reshape(B, S, N_B_GROUPS, B_KV_HEADS, HEAD_DIM)
    q = partial_rope(q, cos, sin) * (HEAD_DIM ** -0.5)
    outs = []
    lses = []
    for g in range(N_B_GROUPS):
        o_g, lse_g = dilated_group(q[:, :, g], k_sh, v_sh, B_WINDOWS[g], B_DILATIONS[g])
        outs.append(o_g)
        lses.append(lse_g)
    alpha = jax.nn.softmax(jnp.stack(lses, axis=0), axis=0)
    o = jnp.sum(alpha[..., None] * jnp.stack(outs, axis=0), axis=0)
    return o.astype(h.dtype).reshape(B, S, D) @ wo


def shared_kv(x, gain, shift, scale, w_kv, cos, sin):
    B, S, D = x.shape
    h = modulate(x, gain, shift, scale)
    k, v = jnp.split(h @ w_kv, 2, axis=-1)
    k = partial_rope(k.reshape(B, S, B_KV_HEADS, HEAD_DIM), cos, sin)
    return k, v.reshape(B, S, B_KV_HEADS, HEAD_DIM)


def moe_ffn(h, w_router, b_router, w_gate_up, b_gate_up, w_down, b_down):
    T, D = h.shape
    logits = (h @ w_router + b_router).astype(jnp.float32)
    top_logits, top_idx = lax.top_k(logits, TOP_K)
    top_w = jax.nn.softmax(top_logits, axis=-1).astype(h.dtype)
    n = T * TOP_K
    flat_e = top_idx.reshape(n)
    order = jnp.argsort(flat_e, stable=True)
    sorted_e = flat_e[order]
    counts = jnp.bincount(flat_e, length=N_EXPERTS)
    starts = jnp.cumsum(counts) - counts
    padded = (counts + EXPERT_BLOCK - 1) // EXPERT_BLOCK * EXPERT_BLOCK
    pad_ends = jnp.cumsum(padded)
    dest = pad_ends[sorted_e] - padded[sorted_e] + jnp.arange(n) - starts[sorted_e]
    n_rows = -(-n // EXPERT_BLOCK) * EXPERT_BLOCK + N_EXPERTS * EXPERT_BLOCK
    n_blocks = n_rows // EXPERT_BLOCK
    row_token = jnp.zeros((n_rows,), jnp.int32).at[dest].set((order // TOP_K).astype(jnp.int32))
    row_gate = jnp.zeros((n_rows,), h.dtype).at[dest].set(top_w.reshape(n)[order])
    block_expert = jnp.minimum(
        jnp.searchsorted(pad_ends, jnp.arange(n_blocks) * EXPERT_BLOCK, side='right'), N_EXPERTS - 1)

    def expert_block(args):
        tok, g, e = args
        gu = h[tok] @ w_gate_up[e] + b_gate_up[e]
        gate = jnp.minimum(gu[:, :D_FF], SWIGLU_LIMIT)
        up = jnp.clip(gu[:, D_FF:], -SWIGLU_LIMIT, SWIGLU_LIMIT)
        glu = gate * jax.nn.sigmoid(SWIGLU_ALPHA * gate)
        y = ((up + 1) * glu) @ w_down[e] + b_down[e]
        return y * g[:, None]

    ys = lax.map(expert_block, (row_token.reshape(n_blocks, EXPERT_BLOCK),
                                row_gate.reshape(n_blocks, EXPERT_BLOCK), block_expert))
    return jnp.zeros_like(h).at[row_token].add(ys.reshape(n_rows, D))


def setup_inputs(seed: int = 0) -> dict:
    key = jax.random.key(seed)
    ks = jax.random.split(key, 26)
    D = D_MODEL

    def nrm(k, shape, scale):
        return jax.random.normal(k, shape, jnp.float32) * scale

    return {
        'x': nrm(ks[0], (BATCH, SEQ, D), 1.0),
        'c': nrm(ks[1], (BATCH, D), 1.0),
        'mod_w': nrm(ks[2], (DEPTH, D, 6 * D), 0.5 * D ** -0.5),
        'mod_b': nrm(ks[3], (DEPTH, 6 * D), 0.02),
        'mix_norm': 1.0 + nrm(ks[4], (DEPTH, D), 0.02),
        'ffn_norm': 1.0 + nrm(ks[5], (DEPTH, D), 0.02),
        'a_wqkv': nrm(ks[6], (N_A_LAYERS, D, 3 * A_HEADS * 2 * HEAD_DIM), D ** -0.5),
        'a_wo': nrm(ks[7], (N_A_LAYERS, A_HEADS * 2 * HEAD_DIM, D), D ** -0.5),
        'a_lambda': nrm(ks[8], (N_A_LAYERS, 4, HEAD_DIM), 0.1),
        'a_subln': 1.0 + nrm(ks[9], (N_A_LAYERS, 2 * HEAD_DIM), 0.02),
        'kv_norm': 1.0 + nrm(ks[10], (D,), 0.02),
        'kv_mod_w': nrm(ks[11], (D, 2 * D), 0.5 * D ** -0.5),
        'kv_mod_b': nrm(ks[12], (2 * D,), 0.02),
        'kv_w': nrm(ks[13], (D, 2 * B_KV_HEADS * HEAD_DIM), D ** -0.5),
        'b_wq': nrm(ks[14], (N_B_LAYERS, D, N_B_GROUPS * B_KV_HEADS * HEAD_DIM), D ** -0.5),
        'b_wo': nrm(ks[15], (N_B_LAYERS, B_KV_HEADS * HEAD_DIM, D), D ** -0.5),
        'router_w': nrm(ks[16], (DEPTH, D, N_EXPERTS), D ** -0.5),
        'router_b': nrm(ks[17], (DEPTH, N_EXPERTS), 0.01),
        'exp_w_gate_up': nrm(ks[18], (DEPTH, N_EXPERTS, D, 2 * D_FF), D ** -0.5),
        'exp_b_gate_up': nrm(ks[19], (DEPTH, N_EXPERTS, 2 * D_FF), 0.01),
        'exp_w_down': nrm(ks[20], (DEPTH, N_EXPERTS, D_FF, D), D_FF ** -0.5),
        'exp_b_down': nrm(ks[21], (DEPTH, N_EXPERTS, D), 0.01),
        'final_norm': 1.0 + nrm(ks[22], (D,), 0.02),
        'final_mod_w': nrm(ks[23], (D, 2 * D), 0.5 * D ** -0.5),
        'final_mod_b': nrm(ks[24], (2 * D,), 0.02),
    }


def reference(x, c, mod_w, mod_b, mix_norm, ffn_norm, a_wqkv, a_wo, a_lambda, a_subln,
              kv_norm, kv_mod_w, kv_mod_b, kv_w, b_wq, b_wo, router_w, router_b,
              exp_w_gate_up, exp_b_gate_up, exp_w_down, exp_b_down,
              final_norm, final_mod_w, final_mod_b):
    B, S, D = x.shape
    cos, sin = rope_tables(S, x.dtype)
    c_act = jax.nn.silu(c)
    k_sh = None
    v_sh = None
    for layer in range(DEPTH):
        mod = c_act @ mod_w[layer] + mod_b[layer]
        sh1, sc1, g1, sh2, sc2, g2 = jnp.split(mod, 6, axis=-1)
        h = modulate(x, mix_norm[layer], sh1, sc1)
        if layer < N_A_LAYERS:
            lambda_init = 0.8 - 0.6 * math.exp(-0.3 * layer)
            y = differential_attention(h, a_wqkv[layer], a_wo[layer], a_lambda[layer],
                                       a_subln[layer], cos, sin, lambda_init)
        else:
            j = layer - N_A_LAYERS
            y = dilated_attention(h, b_wq[j], b_wo[j], k_sh, v_sh, cos, sin)
        x = x + g1[:, None, :] * y
        h = modulate(x, ffn_norm[layer], sh2, sc2)
        f = moe_ffn(h.reshape(B * S, D), router_w[layer], router_b[layer], exp_w_gate_up[layer],
                    exp_b_gate_up[layer], exp_w_down[layer], exp_b_down[layer])
        x = x + g2[:, None, :] * f.reshape(B, S, D)
        if layer == N_A_LAYERS - 1:
            kv_sh, kv_sc = jnp.split(c_act @ kv_mod_w + kv_mod_b, 2, axis=-1)
            k_sh, v_sh = shared_kv(x, kv_norm, kv_sh, kv_sc, kv_w, cos, sin)
    f_sh, f_sc = jnp.split(c_act @ final_mod_w + final_mod_b, 2, axis=-1)
    return modulate(x, final_norm, f_sh, f_sc)
```

```python
import functools
import math

import jax
import jax.numpy as jnp
from jax import lax
from jax.experimental import pallas as pl
from jax.experimental.pallas import tpu as pltpu

F32 = jnp.float32
BF16 = jnp.bfloat16

D_MODEL = 1024
HEAD_DIM = 64
A_HEADS = 8
B_KV_HEADS = 16
B_DILATIONS = (1, 4, 16)
WINDOW_STEPS = 128
ROPE_THETA = 500000.0
ROPE_DIM = HEAD_DIM // 4
N_EXPERTS = 32
TOP_K = 4
SWIGLU_LIMIT = 7.0
SWIGLU_ALPHA = 1.702
NORM_EPS = 1e-5
N_A_LAYERS = 2
DEPTH = 4

LANES = 128
EXPERT_ROWS = 256
NEG_BIG = -1e30
VMEM_LIMIT = 56 * 1024 * 1024


def _params(sem, vmem=VMEM_LIMIT):
    return pltpu.CompilerParams(dimension_semantics=sem, vmem_limit_bytes=vmem)


def _adaln_kernel(c_ref, w_ref, b_ref, o_ref):
    c = c_ref[...]
    c_act = (c * jax.nn.sigmoid(c)).astype(BF16)
    o_ref[0] = jnp.dot(c_act, w_ref[0].astype(BF16), preferred_element_type=F32) + b_ref[0]


def _adaln_vectors(c, w, b):
    n_l, d, n = w.shape
    n_b = c.shape[0]
    tn = 1024
    return pl.pallas_call(
        _adaln_kernel,
        grid=(n_l, n // tn),
        in_specs=[
            pl.BlockSpec((n_b, d), lambda l, j: (0, 0)),
            pl.BlockSpec((1, d, tn), lambda l, j: (l, 0, j)),
            pl.BlockSpec((1, 1, tn), lambda l, j: (l, 0, j)),
        ],
        out_specs=pl.BlockSpec((1, n_b, tn), lambda l, j: (l, 0, j)),
        out_shape=jax.ShapeDtypeStruct((n_l, n_b, n), F32),
        compiler_params=_params(("parallel", "parallel")),
        name="adaln_vectors",
    )(c, w, b.reshape(n_l, 1, n))


def _modulated_norm(x, gain, shift, scale):
    y = x * lax.rsqrt(jnp.mean(x * x, axis=-1, keepdims=True) + NORM_EPS)
    return y * gain * (1.0 + scale) + shift


def _rope_tables(seq):
    inv = ROPE_THETA ** (-jnp.arange(0, ROPE_DIM, 2, dtype=F32) / ROPE_DIM)
    ang = jnp.arange(seq, dtype=F32)[:, None] * inv[None, :]
    cos, sin = jnp.cos(ang), jnp.sin(ang)
    half = ROPE_DIM // 2
    rest = HEAD_DIM - ROPE_DIM
    zeros = jnp.zeros((seq, half), F32)
    c_tab = jnp.concatenate([cos, cos, jnp.ones((seq, rest), F32)], axis=1)
    s1_tab = jnp.concatenate([zeros, sin, jnp.zeros((seq, rest), F32)], axis=1)
    s2_tab = jnp.concatenate([-sin, zeros, jnp.zeros((seq, rest), F32)], axis=1)
    rep = LANES // HEAD_DIM
    return tuple(jnp.tile(t, (1, rep)) for t in (c_tab, s1_tab, s2_tab))


def _apply_rope(y, c_tab, s1_tab, s2_tab):
    parts = []
    for j in range(y.shape[1] // LANES):
        yj = y[:, j * LANES:(j + 1) * LANES]
        parts.append(yj * c_tab + pltpu.roll(yj, ROPE_DIM // 2, 1) * s1_tab
                     + pltpu.roll(yj, LANES - ROPE_DIM // 2, 1) * s2_tab)
    return jnp.concatenate(parts, axis=1)


def _proj_kernel(x_ref, gain_ref, shift_ref, scale_ref, w_ref, rc_ref, rs1_ref, rs2_ref, *rest,
                 segs, tm):
    n_out = sum(len(s[3]) for s in segs)
    out_refs, scr_ref = rest[:n_out], rest[n_out]
    h = _modulated_norm(x_ref[...], gain_ref[...], shift_ref[0], scale_ref[0]).astype(BF16)
    oi = 0
    for chunk, rope, mult, dils in segs:
        y = jnp.dot(h, w_ref[:, chunk * D_MODEL:(chunk + 1) * D_MODEL], preferred_element_type=F32)
        if rope:
            y = _apply_rope(y, rc_ref[...], rs1_ref[...], rs2_ref[...])
        if mult != 1.0:
            y = y * mult
        for dil in dils:
            o_ref = out_refs[oi]
            oi += 1
            if dil == 1:
                o_ref[...] = y.astype(BF16)
            else:
                for j in range(D_MODEL // LANES):
                    scr_ref[j] = y[:, j * LANES:(j + 1) * LANES]
                for r in range(dil):
                    for j in range(D_MODEL // LANES):
                        o_ref[0, r, :, j * LANES:(j + 1) * LANES] = (
                            scr_ref[j, pl.ds(r, tm // dil, stride=dil), :].astype(BF16))


def _norm_project(x, gain, shift, scale, w, rope_tabs, segs, *, n_b, seq, name):
    t, d = x.shape
    tm = 512
    n_s = seq // tm
    out_shapes, out_specs = [], []
    for _, _, _, dils in segs:
        for dil in dils:
            if dil == 1:
                out_shapes.append(jax.ShapeDtypeStruct((t, D_MODEL), BF16))
                out_specs.append(pl.BlockSpec((tm, D_MODEL), lambda i: (i, 0)))
            else:
                out_shapes.append(jax.ShapeDtypeStruct((n_b, dil, seq // dil, D_MODEL), BF16))
                out_specs.append(pl.BlockSpec((1, dil, tm // dil, D_MODEL),
                                              lambda i: (i // n_s, 0, i % n_s, 0)))
    per_batch = pl.BlockSpec((1, 1, d), lambda i: (i // n_s, 0, 0))
    rope_spec = pl.BlockSpec((tm, LANES), lambda i: (i % n_s, 0))
    return pl.pallas_call(
        functools.partial(_proj_kernel, segs=segs, tm=tm),
        grid=(t // tm,),
        in_specs=[
            pl.BlockSpec((tm, d), lambda i: (i, 0)),
            pl.BlockSpec((1, d), lambda i: (0, 0)),
            per_batch, per_batch,
            pl.BlockSpec(w.shape, lambda i: (0, 0)),
            rope_spec, rope_spec, rope_spec,
        ],
        out_specs=out_specs,
        out_shape=out_shapes,
        scratch_shapes=[pltpu.VMEM((D_MODEL // LANES, tm, LANES), F32)],
        compiler_params=_params(("parallel",)),
        name=name,
    )(x, gain.reshape(1, d), shift, scale, w, *rope_tabs)


def _diff_attn_kernel(lam_ref, subln_ref, q_ref, k_ref, v_ref, o_ref, *, tq, lambda_init):
    i = pl.program_id(2)
    lf = lam_ref[...]
    lam = (jnp.exp(jnp.sum(lf[0:1] * lf[1:2], keepdims=True))
           - jnp.exp(jnp.sum(lf[2:3] * lf[3:4], keepdims=True)) + lambda_init)
    q = q_ref[...]
    lane_lo = lax.broadcasted_iota(jnp.int32, q.shape, 1) < HEAD_DIM
    zero = jnp.zeros_like(q)
    q1 = jnp.where(lane_lo, q, zero)
    q2 = jnp.where(lane_lo, zero, q)
    causal = (lax.broadcasted_iota(jnp.int32, (tq, tq), 0)
              >= lax.broadcasted_iota(jnp.int32, (tq, tq), 1))

    def scores(qm, kb):
        return lax.dot_general(qm, kb, (((1,), (1,)), ((), ())), preferred_element_type=F32)

    def online(s, m, l, acc, vb):
        m_new = jnp.maximum(m, jnp.max(s, axis=-1, keepdims=True))
        p = jnp.exp(s - m_new)
        alpha = jnp.exp(m - m_new)
        l = alpha * l + jnp.sum(p, axis=-1, keepdims=True)
        acc = alpha * acc + jnp.dot(p.astype(BF16), vb, preferred_element_type=F32)
        return m_new, l, acc

    def step(j, carry, masked):
        m1, l1, a1, m2, l2, a2 = carry
        rows = pl.ds(pl.multiple_of(j * tq, tq), tq)
        kb = k_ref[rows, :]
        vb = v_ref[rows, :]
        s1 = scores(q1, kb)
        s2 = scores(q2, kb)
        if masked:
            s1 = jnp.where(causal, s1, NEG_BIG)
            s2 = jnp.where(causal, s2, NEG_BIG)
        m1, l1, a1 = online(s1, m1, l1, a1, vb)
        m2, l2, a2 = online(s2, m2, l2, a2, vb)
        return m1, l1, a1, m2, l2, a2

    col = jnp.full((tq, 1), NEG_BIG, F32)
    zcol = jnp.zeros((tq, 1), F32)
    zacc = jnp.zeros((tq, 2 * HEAD_DIM), F32)
    carry = lax.fori_loop(0, i, lambda j, c: step(j, c, False), (col, zcol, zacc, col, zcol, zacc))
    m1, l1, a1, m2, l2, a2 = step(i, carry, True)
    o = a1 / l1 - lam * (a2 / l2)
    o = o * lax.rsqrt(jnp.mean(o * o, axis=-1, keepdims=True) + NORM_EPS)
    o_ref[...] = (o * subln_ref[...] * (1.0 - lambda_init)).astype(BF16)


def _diff_attention(q, k, v, lam_vecs, subln, lambda_init, *, n_b, seq):
    t = q.shape[0]
    tq = 512
    n_q = seq // tq
    width = 2 * HEAD_DIM
    kv_spec = pl.BlockSpec((seq, width), lambda b, h, i: (b, h))
    return pl.pallas_call(
        functools.partial(_diff_attn_kernel, tq=tq, lambda_init=lambda_init),
        grid=(n_b, A_HEADS, n_q),
        in_specs=[
            pl.BlockSpec((4, HEAD_DIM), lambda b, h, i: (0, 0)),
            pl.BlockSpec((1, width), lambda b, h, i: (0, 0)),
            pl.BlockSpec((tq, width), lambda b, h, i: (b * n_q + i, h)),
            kv_spec, kv_spec,
        ],
        out_specs=pl.BlockSpec((tq, width), lambda b, h, i: (b * n_q + i, h)),
        out_shape=jax.ShapeDtypeStruct((t, D_MODEL), BF16),
        compiler_params=_params(("parallel", "parallel", "parallel")),
        name="diff_attention",
    )(lam_vecs, subln.reshape(1, width), q, k, v)


def _window_unit(q, kc, vc, valid, lane_lo):
    zero = jnp.zeros_like(q)
    qq = jnp.concatenate([jnp.where(lane_lo, q, zero), jnp.where(lane_lo, zero, q)], axis=0)
    s = lax.dot_general(qq, kc, (((1,), (1,)), ((), ())), preferred_element_type=F32)
    s = jnp.where(valid, s, NEG_BIG)
    m = jnp.max(s, axis=-1, keepdims=True)
    p = jnp.exp(s - m)
    l = jnp.sum(p, axis=-1, keepdims=True)
    o = jnp.dot(p.astype(BF16), vc, preferred_element_type=F32) / l
    lse = jnp.broadcast_to(m + jnp.log(l), o.shape)
    n = WINDOW_STEPS
    return jnp.where(lane_lo, o[:n], o[n:]), jnp.where(lane_lo, lse[:n], lse[n:])


def _dil_attn_kernel(q0_ref, q1_ref, q2_ref, k0_ref, v0_ref, k1_ref, v1_ref, k2_ref, v2_ref,
                     o_ref, acc_ref, lse_ref, *, seq):
    n = WINDOW_STEPS
    lane_lo = lax.broadcasted_iota(jnp.int32, (n, LANES), 1) < HEAD_DIM
    qi = lax.broadcasted_iota(jnp.int32, (2 * n, 2 * n), 0) % n
    kj = lax.broadcasted_iota(jnp.int32, (2 * n, 2 * n), 1)
    band = (kj >= qi) & (kj <= qi + n)
    first = (lax.broadcasted_iota(jnp.int32, (2 * n, n), 1)
             <= lax.broadcasted_iota(jnp.int32, (2 * n, n), 0) % n)

    def unit(g, q_rows, k_rows, v_rows, valid, dst):
        o, lse = _window_unit(q_rows, k_rows, v_rows, valid, lane_lo)
        acc_ref[g, dst, :] = o
        lse_ref[g, dst, :] = lse

    unit(0, q0_ref[0:n, :], k0_ref[0:n, :], v0_ref[0:n, :], first, pl.ds(0, n))

    def g0_body(blk, carry):
        q_start = pl.multiple_of(blk * n, n)
        k_start = pl.multiple_of(blk * n - n, n)
        unit(0, q0_ref[pl.ds(q_start, n), :], k0_ref[pl.ds(k_start, 2 * n), :],
             v0_ref[pl.ds(k_start, 2 * n), :], band, pl.ds(q_start, n))
        return carry

    lax.fori_loop(1, seq // n, g0_body, 0)

    for g, (q_ref, k_ref, v_ref) in ((1, (q1_ref, k1_ref, v1_ref)), (2, (q2_ref, k2_ref, v2_ref))):
        dil = B_DILATIONS[g]
        for r in range(dil):
            for blk in range(seq // dil // n):
                dst = pl.ds(r + blk * n * dil, n, stride=dil)
                if blk == 0:
                    unit(g, q_ref[0, r, 0:n, :], k_ref[0, r, 0:n, :], v_ref[0, r, 0:n, :],
                         first, dst)
                else:
                    keys = slice((blk - 1) * n, (blk + 1) * n)
                    unit(g, q_ref[0, r, blk * n:(blk + 1) * n, :], k_ref[0, r, keys, :],
                         v_ref[0, r, keys, :], band, dst)

    l0, l1, l2 = lse_ref[0], lse_ref[1], lse_ref[2]
    top = jnp.maximum(jnp.maximum(l0, l1), l2)
    w0, w1, w2 = jnp.exp(l0 - top), jnp.exp(l1 - top), jnp.exp(l2 - top)
    o = (w0 * acc_ref[0] + w1 * acc_ref[1] + w2 * acc_ref[2]) / (w0 + w1 + w2)
    o_ref[...] = o.astype(BF16)


def _dilated_attention(q0, q1, q2, k0, v0, k1, v1, k2, v2, *, n_b, seq):
    t = q0.shape[0]
    pairs = D_MODEL // LANES
    nat = pl.BlockSpec((seq, LANES), lambda b, p: (b, p))

    def res(dil):
        return pl.BlockSpec((1, dil, seq // dil, LANES), lambda b, p: (b, 0, 0, p))

    d1, d2 = B_DILATIONS[1], B_DILATIONS[2]
    return pl.pallas_call(
        functools.partial(_dil_attn_kernel, seq=seq),
        grid=(n_b, pairs),
        in_specs=[nat, res(d1), res(d2), nat, nat, res(d1), res(d1), res(d2), res(d2)],
        out_specs=nat,
        out_shape=jax.ShapeDtypeStruct((t, D_MODEL), BF16),
        scratch_shapes=[pltpu.VMEM((3, seq, LANES), F32), pltpu.VMEM((3, seq, LANES), F32)],
        compiler_params=_params(("parallel", "parallel")),
        name="dilated_attention",
    )(q0, q1, q2, k0, v0, k1, v1, k2, v2)


def _mix_out_kernel(o_ref, wo_ref, x_ref, gate_ref, gain_ref, shift_ref, scale_ref, rw_ref, rb_ref,
                    xn_ref, h_ref, idx_ref, wgt_ref, rank_ref, cnt_ref, run_ref, *, tm):
    i = pl.program_id(0)

    @pl.when(i == 0)
    def _():
        run_ref[...] = jnp.zeros_like(run_ref)

    y = jnp.dot(o_ref[...], wo_ref[...], preferred_element_type=F32)
    xn = x_ref[...] + gate_ref[0] * y
    xn_ref[...] = xn
    h = _modulated_norm(xn, gain_ref[...], shift_ref[0], scale_ref[0])
    h_ref[...] = h

    h_hi = h.astype(BF16)
    h_lo = (h - h_hi.astype(F32)).astype(BF16)
    rw = rw_ref[...]
    rw_hi = rw.astype(BF16)
    rw_lo = (rw - rw_hi.astype(F32)).astype(BF16)
    nt = (((1,), (1,)), ((), ()))
    logits = (lax.dot_general(rw_hi, h_hi, nt, preferred_element_type=F32)
              + lax.dot_general(rw_lo, h_hi, nt, preferred_element_type=F32)
              + lax.dot_general(rw_hi, h_lo, nt, preferred_element_type=F32)) + rb_ref[...]

    e_iota = lax.broadcasted_iota(jnp.int32, logits.shape, 0)
    work = logits
    sels, tops, idxs = [], [], []
    for _ in range(TOP_K):
        mk = jnp.max(work, axis=0, keepdims=True)
        ik = jnp.min(jnp.where(work == mk, e_iota, N_EXPERTS), axis=0, keepdims=True)
        sel = e_iota == ik
        work = jnp.where(sel, -jnp.inf, work)
        sels.append(sel)
        tops.append(mk)
        idxs.append(ik)
    exps = [jnp.exp(m - tops[0]) for m in tops]
    denom = exps[0] + exps[1] + exps[2] + exps[3]
    for k in range(TOP_K):
        idx_ref[k:k + 1, :] = idxs[k]
        wgt_ref[k:k + 1, :] = exps[k] / denom

    chosen = jnp.zeros(logits.shape, F32)
    for sel in sels:
        chosen = chosen + jnp.where(sel, 1.0, 0.0)
    before = (lax.broadcasted_iota(jnp.int32, (tm, tm), 0)
              < lax.broadcasted_iota(jnp.int32, (tm, tm), 1))
    upper = jnp.where(before, 1.0, 0.0).astype(BF16)
    prefix = jnp.dot(chosen.astype(BF16), upper, preferred_element_type=F32) + run_ref[...]
    for k in range(TOP_K):
        rank_ref[k:k + 1, :] = jnp.sum(jnp.where(sels[k], prefix, 0.0), axis=0,
                                       keepdims=True).astype(jnp.int32)
    run_ref[...] = run_ref[...] + jnp.sum(chosen, axis=1, keepdims=True)
    cnt_ref[...] = jnp.broadcast_to(run_ref[...], cnt_ref.shape)


def _mix_out_and_route(o, wo, x, gate, gain, shift, scale, router_w, router_b, *, n_b, seq):
    t, d = x.shape
    tm = 512
    n_s = seq // tm
    per_batch = pl.BlockSpec((1, 1, d), lambda i: (i // n_s, 0, 0))
    row = pl.BlockSpec((tm, d), lambda i: (i, 0))
    sel = pl.BlockSpec((TOP_K, tm), lambda i: (0, i))
    return pl.pallas_call(
        functools.partial(_mix_out_kernel, tm=tm),
        grid=(t // tm,),
        in_specs=[
            row,
            pl.BlockSpec(wo.shape, lambda i: (0, 0)),
            row, per_batch,
            pl.BlockSpec((1, d), lambda i: (0, 0)),
            per_batch, per_batch,
            pl.BlockSpec((N_EXPERTS, d), lambda i: (0, 0)),
            pl.BlockSpec((N_EXPERTS, 1), lambda i: (0, 0)),
        ],
        out_specs=[row, row, sel, sel, sel, pl.BlockSpec((N_EXPERTS, LANES), lambda i: (0, 0))],
        out_shape=[
            jax.ShapeDtypeStruct((t, d), F32),
            jax.ShapeDtypeStruct((t, d), F32),
            jax.ShapeDtypeStruct((TOP_K, t), jnp.int32),
            jax.ShapeDtypeStruct((TOP_K, t), F32),
            jax.ShapeDtypeStruct((TOP_K, t), jnp.int32),
            jax.ShapeDtypeStruct((N_EXPERTS, LANES), F32),
        ],
        scratch_shapes=[pltpu.VMEM((N_EXPERTS, 1), F32)],
        compiler_params=_params(("arbitrary",)),
        name="mix_out_route",
    )(o, wo, x, gate, gain.reshape(1, d), shift, scale, router_w.T, router_b.reshape(N_EXPERTS, 1))


def _row_copy_wait(src_ref, dst_ref, sem, n_groups):
    for _ in range(n_groups):
        pltpu.make_async_copy(src_ref, dst_ref, sem).wait()


def _dispatch_kernel(fill_row_ref, fill_on_ref, dest_ref, h_ref, xs_ref, zero_ref, sem, *, tm):
    @pl.when(pl.program_id(0) == 0)
    def _():
        zero_ref[...] = jnp.zeros_like(zero_ref)

        def fill(j):
            row = pl.multiple_of(fill_row_ref[j], 8)
            return pltpu.make_async_copy(zero_ref, xs_ref.at[pl.ds(row, EXPERT_ROWS)], sem)

        for j in range(2 * N_EXPERTS):
            @pl.when(fill_on_ref[j] == 1)
            def _():
                fill(j).start()
        for j in range(2 * N_EXPERTS):
            @pl.when(fill_on_ref[j] == 1)
            def _():
                fill(j).wait()

    def body(r, carry):
        for k in range(TOP_K):
            pltpu.make_async_copy(h_ref.at[pl.ds(r, 1)], xs_ref.at[pl.ds(dest_ref[k, r], 1)],
                                  sem).start()
        return carry

    lax.fori_loop(0, tm, body, 0)
    _row_copy_wait(h_ref, xs_ref.at[pl.ds(0, tm)], sem, TOP_K)


def _dispatch(dest, h, n_rows, fill_row, fill_on):
    t, d = h.shape
    tm = 256
    grid_spec = pltpu.PrefetchScalarGridSpec(
        num_scalar_prefetch=2,
        grid=(t // tm,),
        in_specs=[
            pl.BlockSpec((TOP_K, tm), lambda i, fr, fo: (0, i), memory_space=pltpu.SMEM),
            pl.BlockSpec((tm, d), lambda i, fr, fo: (i, 0)),
        ],
        out_specs=pl.BlockSpec(memory_space=pl.ANY),
        scratch_shapes=[pltpu.VMEM((EXPERT_ROWS, d), F32), pltpu.SemaphoreType.DMA(())],
    )
    return pl.pallas_call(
        functools.partial(_dispatch_kernel, tm=tm),
        grid_spec=grid_spec,
        out_shape=jax.ShapeDtypeStruct((n_rows, d), F32),
        compiler_params=_params(("arbitrary",)),
        name="moe_dispatch",
    )(fill_row, fill_on, dest, h)


def _expert_kernel(be_ref, first_ref, nvalid_ref, xs_ref, wgu_ref, bgu_ref, wd_ref, bd_ref,
                   ys_ref, wgu_bf, wd_bf):
    b = pl.program_id(0)

    @pl.when(first_ref[b] == 1)
    def _():
        wgu_bf[...] = wgu_ref[0].astype(BF16)
        wd_bf[...] = wd_ref[0].astype(BF16)

    @pl.when(b < nvalid_ref[0])
    def _():
        d_ff = wd_bf.shape[0]
        x = xs_ref[...].astype(BF16)
        gu = jnp.dot(x, wgu_bf[...], preferred_element_type=F32) + bgu_ref[0]
        gate = jnp.minimum(gu[:, :d_ff], SWIGLU_LIMIT)
        up = jnp.clip(gu[:, d_ff:], -SWIGLU_LIMIT, SWIGLU_LIMIT)
        glu = gate * jax.nn.sigmoid(SWIGLU_ALPHA * gate)
        act = ((up + 1.0) * glu).astype(BF16)
        ys_ref[...] = jnp.dot(act, wd_bf[...], preferred_element_type=F32) + bd_ref[0]

    @pl.when(b >= nvalid_ref[0])
    def _():
        ys_ref[...] = jnp.zeros_like(ys_ref)


def _experts(xs, block_expert, block_first, n_valid, w_gu, b_gu, w_d, b_d):
    n_rows, d = xs.shape
    bm = EXPERT_ROWS
    d_ff = w_d.shape[1]
    grid_spec = pltpu.PrefetchScalarGridSpec(
        num_scalar_prefetch=3,
        grid=(n_rows // bm,),
        in_specs=[
            pl.BlockSpec((bm, d), lambda b, be, fi, nv: (b, 0)),
            pl.BlockSpec((1, d, 2 * d_ff), lambda b, be, fi, nv: (be[b], 0, 0)),
            pl.BlockSpec((1, 1, 2 * d_ff), lambda b, be, fi, nv: (be[b], 0, 0)),
            pl.BlockSpec((1, d_ff, d), lambda b, be, fi, nv: (be[b], 0, 0)),
            pl.BlockSpec((1, 1, d), lambda b, be, fi, nv: (be[b], 0, 0)),
        ],
        out_specs=pl.BlockSpec((bm, d), lambda b, be, fi, nv: (b, 0)),
        scratch_shapes=[pltpu.VMEM((d, 2 * d_ff), BF16), pltpu.VMEM((d_ff, d), BF16)],
    )
    return pl.pallas_call(
        _expert_kernel,
        grid_spec=grid_spec,
        out_shape=jax.ShapeDtypeStruct((n_rows, d), F32),
        compiler_params=_params(("arbitrary",)),
        name="moe_experts",
    )(block_expert, block_first, n_valid, xs, w_gu, b_gu.reshape(N_EXPERTS, 1, 2 * d_ff),
      w_d, b_d.reshape(N_EXPERTS, 1, d))


def _combine_kernel(dest_ref, wgt_ref, ys_ref, x_ref, gate_ref, o_ref, buf_ref, sem, *, tm):
    def body(r, carry):
        for k in range(TOP_K):
            pltpu.make_async_copy(ys_ref.at[pl.ds(dest_ref[k, r], 1)],
                                  buf_ref.at[k, pl.ds(r, 1)], sem).start()
        return carry

    lax.fori_loop(0, tm, body, 0)
    _row_copy_wait(ys_ref.at[pl.ds(0, tm)], buf_ref.at[0], sem, TOP_K)
    wgt = wgt_ref[...]
    f = wgt[:, 0:1] * buf_ref[0]
    for k in range(1, TOP_K):
        f = f + wgt[:, k:k + 1] * buf_ref[k]
    o_ref[...] = x_ref[...] + gate_ref[0] * f


def _combine(dest, wgt_tk, ys, x, gate, *, seq):
    t, d = x.shape
    tm = 128
    n_s = seq // tm
    return pl.pallas_call(
        functools.partial(_combine_kernel, tm=tm),
        grid=(t // tm,),
        in_specs=[
            pl.BlockSpec((TOP_K, tm), lambda i: (0, i), memory_space=pltpu.SMEM),
            pl.BlockSpec((tm, TOP_K), lambda i: (i, 0)),
            pl.BlockSpec(memory_space=pl.ANY),
            pl.BlockSpec((tm, d), lambda i: (i, 0)),
            pl.BlockSpec((1, 1, d), lambda i: (i // n_s, 0, 0)),
        ],
        out_specs=pl.BlockSpec((tm, d), lambda i: (i, 0)),
        out_shape=jax.ShapeDtypeStruct((t, d), F32),
        scratch_shapes=[pltpu.VMEM((TOP_K, tm, d), F32), pltpu.SemaphoreType.DMA(())],
        compiler_params=_params(("arbitrary",)),
        name="moe_combine",
    )(dest, wgt_tk, ys, x, gate)


def _moe(h, idx, wgt, rank, counts, x, gate, w_gu, b_gu, w_d, b_d, *, seq):
    t = h.shape[0]
    bm = EXPERT_ROWS
    n_rows = t * TOP_K + N_EXPERTS * bm
    n_blocks = n_rows // bm
    cnt = counts[:, 0].astype(jnp.int32)
    padded = (cnt + bm - 1) // bm * bm
    pad_end = jnp.cumsum(padded)
    pad_start = pad_end - padded
    dest = pad_start[idx] + rank
    block_expert = jnp.minimum(
        jnp.searchsorted(pad_end, jnp.arange(n_blocks, dtype=jnp.int32) * bm, side="right"),
        N_EXPERTS - 1).astype(jnp.int32)
    block_first = jnp.concatenate(
        [jnp.ones((1,), jnp.int32), (block_expert[1:] != block_expert[:-1]).astype(jnp.int32)])
    n_valid = (pad_end[-1:] // bm).astype(jnp.int32)
    tail = n_valid + jnp.arange(N_EXPERTS, dtype=jnp.int32)
    fill_row = jnp.concatenate([jnp.maximum(pad_end - bm, 0),
                                jnp.minimum(tail, n_blocks - 1) * bm]).astype(jnp.int32)
    fill_on = jnp.concatenate([padded > 0, tail < n_blocks]).astype(jnp.int32)
    xs = _dispatch(dest, h, n_rows, fill_row, fill_on)
    ys = _experts(xs, block_expert, block_first, n_valid, w_gu, b_gu, w_d, b_d)
    return _combine(dest, wgt.T, ys, x, gate, seq=seq)


def _final_kernel(x_ref, gain_ref, shift_ref, scale_ref, o_ref):
    o_ref[...] = _modulated_norm(x_ref[...], gain_ref[...], shift_ref[0], scale_ref[0])


def _final_modulate(x, gain, shift, scale, *, seq):
    t, d = x.shape
    tm = 512
    n_s = seq // tm
    per_batch = pl.BlockSpec((1, 1, d), lambda i: (i // n_s, 0, 0))
    return pl.pallas_call(
        _final_kernel,
        grid=(t // tm,),
        in_specs=[pl.BlockSpec((tm, d), lambda i: (i, 0)), pl.BlockSpec((1, d), lambda i: (0, 0)),
                  per_batch, per_batch],
        out_specs=pl.BlockSpec((tm, d), lambda i: (i, 0)),
        out_shape=jax.ShapeDtypeStruct((t, d), F32),
        compiler_params=_params(("parallel",)),
        name="final_modulate",
    )(x, gain.reshape(1, d), shift, scale)


def kernel(x, c, mod_w, mod_b, mix_norm, ffn_norm, a_wqkv, a_wo, a_lambda, a_subln, kv_norm, kv_mod_w, kv_mod_b, kv_w, b_wq, b_wo, router_w, router_b, exp_w_gate_up, exp_b_gate_up, exp_w_down, exp_b_down, final_norm, final_mod_w, final_mod_b):
    n_b, seq, d = x.shape
    t = n_b * seq
    sizes = dict(n_b=n_b, seq=seq)
    rope_tabs = _rope_tables(seq)
    q_scale = HEAD_DIM ** -0.5

    def per_batch(v):
        return v.reshape(n_b, 1, d)

    mod = _adaln_vectors(c, mod_w, mod_b)
    kv_mod = _adaln_vectors(c, kv_mod_w[None], kv_mod_b[None])[0]
    fin_mod = _adaln_vectors(c, final_mod_w[None], final_mod_b[None])[0]

    xt = x.reshape(t, d)
    shared = None
    for layer in range(DEPTH):
        sh1, sc1, g1, sh2, sc2, g2 = (per_batch(mod[layer, :, j * d:(j + 1) * d]) for j in range(6))
        if layer < N_A_LAYERS:
            lambda_init = 0.8 - 0.6 * math.exp(-0.3 * layer)
            segs = ((0, True, q_scale, (1,)), (1, True, 1.0, (1,)), (2, False, 1.0, (1,)))
            q, k, v = _norm_project(xt, mix_norm[layer], sh1, sc1, a_wqkv[layer].astype(BF16),
                                    rope_tabs, segs, name="a_qkv_proj", **sizes)
            o = _diff_attention(q, k, v, a_lambda[layer], a_subln[layer], lambda_init, **sizes)
            wo = a_wo[layer]
        else:
            j = layer - N_A_LAYERS
            segs = tuple((g, True, q_scale, (B_DILATIONS[g],)) for g in range(3))
            q0, q1, q2 = _norm_project(xt, mix_norm[layer], sh1, sc1, b_wq[j].astype(BF16),
                                       rope_tabs, segs, name="b_q_proj", **sizes)
            o = _dilated_attention(q0, q1, q2, *shared, **sizes)
            wo = b_wo[j]
        xt, h, idx, wgt, rank, counts = _mix_out_and_route(
            o, wo.astype(BF16), xt, g1, ffn_norm[layer], sh2, sc2, router_w[layer],
            router_b[layer], **sizes)
        xt = _moe(h, idx, wgt, rank, counts, xt, g2, exp_w_gate_up[layer], exp_b_gate_up[layer],
                  exp_w_down[layer], exp_b_down[layer], seq=seq)
        if layer == N_A_LAYERS - 1:
            segs = ((0, True, 1.0, B_DILATIONS), (1, False, 1.0, B_DILATIONS))
            k0, k1, k2, v0, v1, v2 = _norm_project(
                xt, kv_norm, per_batch(kv_mod[:, :d]), per_batch(kv_mod[:, d:]),
                kv_w.astype(BF16), rope_tabs, segs, name="shared_kv_proj", **sizes)
            shared = (k0, v0, k1, v1, k2, v2)
    out = _final_modulate(xt, final_norm, per_batch(fin_mod[:, :d]), per_batch(fin_mod[:, d:]),
                          seq=seq)
    return out.reshape(n_b, seq, d)
```

```python
import functools
import math

import jax
import jax.numpy as jnp
from jax import lax
from jax.experimental import pallas as pl
from jax.experimental.pallas import tpu as pltpu

F32 = jnp.float32
BF16 = jnp.bfloat16

D_MODEL = 1024
HEAD_DIM = 64
A_HEADS = 8
B_KV_HEADS = 16
B_DILATIONS = (1, 4, 16)
WINDOW_STEPS = 128
ROPE_THETA = 500000.0
ROPE_DIM = HEAD_DIM // 4
N_EXPERTS = 32
TOP_K = 4
SWIGLU_LIMIT = 7.0
SWIGLU_ALPHA = 1.702
NORM_EPS = 1e-5
N_A_LAYERS = 2
DEPTH = 4

LANES = 128
EXPERT_ROWS = 256
ROUTE_TILE = 512
GROUP_ALIGN = 8
COMPACT_ROWS = ROUTE_TILE * TOP_K + N_EXPERTS * GROUP_ALIGN
BIG_COPY = 64
NEG_BIG = -1e30
VMEM_LIMIT = 56 * 1024 * 1024


def _params(sem, vmem=VMEM_LIMIT):
    return pltpu.CompilerParams(dimension_semantics=sem, vmem_limit_bytes=vmem)


def _adaln_kernel(c_ref, w_ref, b_ref, o_ref):
    c = c_ref[...]
    c_act = (c * jax.nn.sigmoid(c)).astype(BF16)
    o_ref[0] = jnp.dot(c_act, w_ref[0].astype(BF16), preferred_element_type=F32) + b_ref[0]


def _adaln_vectors(c, w, b):
    n_l, d, n = w.shape
    n_b = c.shape[0]
    tn = 1024
    return pl.pallas_call(
        _adaln_kernel,
        grid=(n_l, n // tn),
        in_specs=[
            pl.BlockSpec((n_b, d), lambda l, j: (0, 0)),
            pl.BlockSpec((1, d, tn), lambda l, j: (l, 0, j)),
            pl.BlockSpec((1, 1, tn), lambda l, j: (l, 0, j)),
        ],
        out_specs=pl.BlockSpec((1, n_b, tn), lambda l, j: (l, 0, j)),
        out_shape=jax.ShapeDtypeStruct((n_l, n_b, n), F32),
        compiler_params=_params(("parallel", "parallel")),
        name="adaln_vectors",
    )(c, w, b.reshape(n_l, 1, n))


def _modulated_norm(x, gain, shift, scale):
    y = x * lax.rsqrt(jnp.mean(x * x, axis=-1, keepdims=True) + NORM_EPS)
    return y * gain * (1.0 + scale) + shift


def _rope_tables(seq):
    inv = ROPE_THETA ** (-jnp.arange(0, ROPE_DIM, 2, dtype=F32) / ROPE_DIM)
    ang = jnp.arange(seq, dtype=F32)[:, None] * inv[None, :]
    cos, sin = jnp.cos(ang), jnp.sin(ang)
    half = ROPE_DIM // 2
    rest = HEAD_DIM - ROPE_DIM
    zeros = jnp.zeros((seq, half), F32)
    c_tab = jnp.concatenate([cos, cos, jnp.ones((seq, rest), F32)], axis=1)
    s1_tab = jnp.concatenate([zeros, sin, jnp.zeros((seq, rest), F32)], axis=1)
    s2_tab = jnp.concatenate([-sin, zeros, jnp.zeros((seq, rest), F32)], axis=1)
    rep = LANES // HEAD_DIM
    return tuple(jnp.tile(t, (1, rep)) for t in (c_tab, s1_tab, s2_tab))


def _apply_rope(y, c_tab, s1_tab, s2_tab):
    parts = []
    for j in range(y.shape[1] // LANES):
        yj = y[:, j * LANES:(j + 1) * LANES]
        parts.append(yj * c_tab + pltpu.roll(yj, ROPE_DIM // 2, 1) * s1_tab
                     + pltpu.roll(yj, LANES - ROPE_DIM // 2, 1) * s2_tab)
    return jnp.concatenate(parts, axis=1)


def _proj_kernel(x_ref, gain_ref, shift_ref, scale_ref, w_ref, rc_ref, rs1_ref, rs2_ref, *rest,
                 segs, tm):
    n_out = sum(len(s[3]) for s in segs)
    out_refs, scr_ref = rest[:n_out], rest[n_out]
    h = _modulated_norm(x_ref[...], gain_ref[...], shift_ref[0], scale_ref[0]).astype(BF16)
    oi = 0
    for chunk, rope, mult, dils in segs:
        y = jnp.dot(h, w_ref[:, chunk * D_MODEL:(chunk + 1) * D_MODEL], preferred_element_type=F32)
        if rope:
            y = _apply_rope(y, rc_ref[...], rs1_ref[...], rs2_ref[...])
        if mult != 1.0:
            y = y * mult
        for dil in dils:
            o_ref = out_refs[oi]
            oi += 1
            if dil == 1:
                o_ref[...] = y.astype(BF16)
            else:
                for j in range(D_MODEL // LANES):
                    scr_ref[j] = y[:, j * LANES:(j + 1) * LANES]
                for r in range(dil):
                    for j in range(D_MODEL // LANES):
                        o_ref[0, r, :, j * LANES:(j + 1) * LANES] = (
                            scr_ref[j, pl.ds(r, tm // dil, stride=dil), :].astype(BF16))


def _norm_project(x, gain, shift, scale, w, layer, rope_tabs, segs, *, n_b, seq, name):
    t, d = x.shape
    tm = 512
    n_s = seq // tm
    out_shapes, out_specs = [], []
    for _, _, _, dils in segs:
        for dil in dils:
            if dil == 1:
                out_shapes.append(jax.ShapeDtypeStruct((t, D_MODEL), BF16))
                out_specs.append(pl.BlockSpec((tm, D_MODEL), lambda i: (i, 0)))
            else:
                out_shapes.append(jax.ShapeDtypeStruct((n_b, dil, seq // dil, D_MODEL), BF16))
                out_specs.append(pl.BlockSpec((1, dil, tm // dil, D_MODEL),
                                              lambda i: (i // n_s, 0, i % n_s, 0)))
    per_batch = pl.BlockSpec((1, 1, d), lambda i: (i // n_s, 0, 0))
    rope_spec = pl.BlockSpec((tm, LANES), lambda i: (i % n_s, 0))
    return pl.pallas_call(
        functools.partial(_proj_kernel, segs=segs, tm=tm),
        grid=(t // tm,),
        in_specs=[
            pl.BlockSpec((tm, d), lambda i: (i, 0)),
            pl.BlockSpec((1, d), lambda i: (0, 0)),
            per_batch, per_batch,
            pl.BlockSpec((None,) + w.shape[1:], lambda i: (layer, 0, 0)),
            rope_spec, rope_spec, rope_spec,
        ],
        out_specs=out_specs,
        out_shape=out_shapes,
        scratch_shapes=[pltpu.VMEM((D_MODEL // LANES, tm, LANES), F32)],
        compiler_params=_params(("parallel",)),
        name=name,
    )(x, gain.reshape(1, d), shift, scale, w, *rope_tabs)


def _diff_attn_kernel(lam_ref, subln_ref, q_ref, k_ref, v_ref, o_ref, *, tq, lambda_init):
    i = pl.program_id(2)
    lf = lam_ref[...]
    lam = (jnp.exp(jnp.sum(lf[0:1] * lf[1:2], keepdims=True))
           - jnp.exp(jnp.sum(lf[2:3] * lf[3:4], keepdims=True)) + lambda_init)
    q = q_ref[...]
    lane_lo = lax.broadcasted_iota(jnp.int32, q.shape, 1) < HEAD_DIM
    zero = jnp.zeros_like(q)
    q1 = jnp.where(lane_lo, q, zero)
    q2 = jnp.where(lane_lo, zero, q)
    causal = (lax.broadcasted_iota(jnp.int32, (tq, tq), 0)
              >= lax.broadcasted_iota(jnp.int32, (tq, tq), 1))

    def scores(qm, kb):
        return lax.dot_general(qm, kb, (((1,), (1,)), ((), ())), preferred_element_type=F32)

    def online(s, m, l, acc, vb):
        m_new = jnp.maximum(m, jnp.max(s, axis=-1, keepdims=True))
        p = jnp.exp(s - m_new)
        alpha = jnp.exp(m - m_new)
        l = alpha * l + jnp.sum(p, axis=-1, keepdims=True)
        acc = alpha * acc + jnp.dot(p.astype(BF16), vb, preferred_element_type=F32)
        return m_new, l, acc

    def step(j, carry, masked):
        m1, l1, a1, m2, l2, a2 = carry
        rows = pl.ds(pl.multiple_of(j * tq, tq), tq)
        kb = k_ref[rows, :]
        vb = v_ref[rows, :]
        s1 = scores(q1, kb)
        s2 = scores(q2, kb)
        if masked:
            s1 = jnp.where(causal, s1, NEG_BIG)
            s2 = jnp.where(causal, s2, NEG_BIG)
        m1, l1, a1 = online(s1, m1, l1, a1, vb)
        m2, l2, a2 = online(s2, m2, l2, a2, vb)
        return m1, l1, a1, m2, l2, a2

    col = jnp.full((tq, 1), NEG_BIG, F32)
    zcol = jnp.zeros((tq, 1), F32)
    zacc = jnp.zeros((tq, 2 * HEAD_DIM), F32)
    carry = lax.fori_loop(0, i, lambda j, c: step(j, c, False), (col, zcol, zacc, col, zcol, zacc))
    m1, l1, a1, m2, l2, a2 = step(i, carry, True)
    o = a1 / l1 - lam * (a2 / l2)
    o = o * lax.rsqrt(jnp.mean(o * o, axis=-1, keepdims=True) + NORM_EPS)
    o_ref[...] = (o * subln_ref[...] * (1.0 - lambda_init)).astype(BF16)


def _diff_attention(q, k, v, lam_vecs, subln, lambda_init, *, n_b, seq):
    t = q.shape[0]
    tq = 512
    n_q = seq // tq
    width = 2 * HEAD_DIM
    kv_spec = pl.BlockSpec((seq, width), lambda b, h, i: (b, h))
    return pl.pallas_call(
        functools.partial(_diff_attn_kernel, tq=tq, lambda_init=lambda_init),
        grid=(n_b, A_HEADS, n_q),
        in_specs=[
            pl.BlockSpec((4, HEAD_DIM), lambda b, h, i: (0, 0)),
            pl.BlockSpec((1, width), lambda b, h, i: (0, 0)),
            pl.BlockSpec((tq, width), lambda b, h, i: (b * n_q + i, h)),
            kv_spec, kv_spec,
        ],
        out_specs=pl.BlockSpec((tq, width), lambda b, h, i: (b * n_q + i, h)),
        out_shape=jax.ShapeDtypeStruct((t, D_MODEL), BF16),
        compiler_params=_params(("parallel", "parallel", "parallel")),
        name="diff_attention",
    )(lam_vecs, subln.reshape(1, width), q, k, v)


def _window_unit(q, kc, vc, valid, lane_lo):
    zero = jnp.zeros_like(q)
    qq = jnp.concatenate([jnp.where(lane_lo, q, zero), jnp.where(lane_lo, zero, q)], axis=0)
    s = lax.dot_general(qq, kc, (((1,), (1,)), ((), ())), preferred_element_type=F32)
    s = jnp.where(valid, s, NEG_BIG)
    m = jnp.max(s, axis=-1, keepdims=True)
    p = jnp.exp(s - m)
    l = jnp.sum(p, axis=-1, keepdims=True)
    o = jnp.dot(p.astype(BF16), vc, preferred_element_type=F32) / l
    lse = jnp.broadcast_to(m + jnp.log(l), o.shape)
    n = WINDOW_STEPS
    return jnp.where(lane_lo, o[:n], o[n:]), jnp.where(lane_lo, lse[:n], lse[n:])


def _dil_attn_kernel(q0_ref, q1_ref, q2_ref, k0_ref, v0_ref, k1_ref, v1_ref, k2_ref, v2_ref,
                     o_ref, acc_ref, lse_ref, *, seq):
    n = WINDOW_STEPS
    lane_lo = lax.broadcasted_iota(jnp.int32, (n, LANES), 1) < HEAD_DIM
    qi = lax.broadcasted_iota(jnp.int32, (2 * n, 2 * n), 0) % n
    kj = lax.broadcasted_iota(jnp.int32, (2 * n, 2 * n), 1)
    band = (kj >= qi) & (kj <= qi + n)
    first = (lax.broadcasted_iota(jnp.int32, (2 * n, n), 1)
             <= lax.broadcasted_iota(jnp.int32, (2 * n, n), 0) % n)

    def unit(g, q_rows, k_rows, v_rows, valid, dst):
        o, lse = _window_unit(q_rows, k_rows, v_rows, valid, lane_lo)
        acc_ref[g, dst, :] = o
        lse_ref[g, dst, :] = lse

    unit(0, q0_ref[0:n, :], k0_ref[0:n, :], v0_ref[0:n, :], first, pl.ds(0, n))

    def g0_body(blk, carry):
        q_start = pl.multiple_of(blk * n, n)
        k_start = pl.multiple_of(blk * n - n, n)
        unit(0, q0_ref[pl.ds(q_start, n), :], k0_ref[pl.ds(k_start, 2 * n), :],
             v0_ref[pl.ds(k_start, 2 * n), :], band, pl.ds(q_start, n))
        return carry

    lax.fori_loop(1, seq // n, g0_body, 0)

    for g, (q_ref, k_ref, v_ref) in ((1, (q1_ref, k1_ref, v1_ref)), (2, (q2_ref, k2_ref, v2_ref))):
        dil = B_DILATIONS[g]
        for r in range(dil):
            for blk in range(seq // dil // n):
                dst = pl.ds(r + blk * n * dil, n, stride=dil)
                if blk == 0:
                    unit(g, q_ref[0, r, 0:n, :], k_ref[0, r, 0:n, :], v_ref[0, r, 0:n, :],
                         first, dst)
                else:
                    keys = slice((blk - 1) * n, (blk + 1) * n)
                    unit(g, q_ref[0, r, blk * n:(blk + 1) * n, :], k_ref[0, r, keys, :],
                         v_ref[0, r, keys, :], band, dst)

    l0, l1, l2 = lse_ref[0], lse_ref[1], lse_ref[2]
    top = jnp.maximum(jnp.maximum(l0, l1), l2)
    w0, w1, w2 = jnp.exp(l0 - top), jnp.exp(l1 - top), jnp.exp(l2 - top)
    o = (w0 * acc_ref[0] + w1 * acc_ref[1] + w2 * acc_ref[2]) / (w0 + w1 + w2)
    o_ref[...] = o.astype(BF16)


def _dilated_attention(q0, q1, q2, k0, v0, k1, v1, k2, v2, *, n_b, seq):
    t = q0.shape[0]
    pairs = D_MODEL // LANES
    nat = pl.BlockSpec((seq, LANES), lambda b, p: (b, p))

    def res(dil):
        return pl.BlockSpec((1, dil, seq // dil, LANES), lambda b, p: (b, 0, 0, p))

    d1, d2 = B_DILATIONS[1], B_DILATIONS[2]
    return pl.pallas_call(
        functools.partial(_dil_attn_kernel, seq=seq),
        grid=(n_b, pairs),
        in_specs=[nat, res(d1), res(d2), nat, nat, res(d1), res(d1), res(d2), res(d2)],
        out_specs=nat,
        out_shape=jax.ShapeDtypeStruct((t, D_MODEL), BF16),
        scratch_shapes=[pltpu.VMEM((3, seq, LANES), F32), pltpu.VMEM((3, seq, LANES), F32)],
        compiler_params=_params(("parallel", "parallel")),
        name="dilated_attention",
    )(q0, q1, q2, k0, v0, k1, v1, k2, v2)


def _mix_out_kernel(o_ref, wo_ref, x_ref, gate_ref, gain_ref, shift_ref, scale_ref, rw_ref, rb_ref,
                    xn_ref, h_ref, idx_ref, wgt_ref, rank_ref, cnt_ref, *, tm):
    y = jnp.dot(o_ref[...], wo_ref[...], preferred_element_type=F32)
    xn = x_ref[...] + gate_ref[0] * y
    xn_ref[...] = xn
    h = _modulated_norm(xn, gain_ref[...], shift_ref[0], scale_ref[0])

    h_hi = h.astype(BF16)
    h_ref[...] = h_hi
    h_lo = (h - h_hi.astype(F32)).astype(BF16)
    rw = rw_ref[...]
    rw_hi = rw.astype(BF16)
    rw_lo = (rw - rw_hi.astype(F32)).astype(BF16)
    nt = (((1,), (1,)), ((), ()))
    logits = (lax.dot_general(rw_hi, h_hi, nt, preferred_element_type=F32)
              + lax.dot_general(rw_lo, h_hi, nt, preferred_element_type=F32)
              + lax.dot_general(rw_hi, h_lo, nt, preferred_element_type=F32)) + rb_ref[...]

    e_iota = lax.broadcasted_iota(jnp.int32, logits.shape, 0)
    work = logits
    sels, tops, idxs = [], [], []
    for _ in range(TOP_K):
        mk = jnp.max(work, axis=0, keepdims=True)
        ik = jnp.min(jnp.where(work == mk, e_iota, N_EXPERTS), axis=0, keepdims=True)
        sel = e_iota == ik
        work = jnp.where(sel, -jnp.inf, work)
        sels.append(sel)
        tops.append(mk)
        idxs.append(ik)
    exps = [jnp.exp(m - tops[0]) for m in tops]
    denom = exps[0] + exps[1] + exps[2] + exps[3]
    for k in range(TOP_K):
        idx_ref[k:k + 1, :] = idxs[k]
        wgt_ref[k:k + 1, :] = exps[k] / denom

    chosen = jnp.zeros(logits.shape, F32)
    for sel in sels:
        chosen = chosen + jnp.where(sel, 1.0, 0.0)
    before = (lax.broadcasted_iota(jnp.int32, (tm, tm), 0)
              < lax.broadcasted_iota(jnp.int32, (tm, tm), 1))
    upper = jnp.where(before, 1.0, 0.0).astype(BF16)
    prefix = jnp.dot(chosen.astype(BF16), upper, preferred_element_type=F32)
    for k in range(TOP_K):
        rank_ref[k:k + 1, :] = jnp.sum(jnp.where(sels[k], prefix, 0.0), axis=0,
                                       keepdims=True).astype(jnp.int32)
    counts = jnp.sum(chosen, axis=1, keepdims=True).astype(jnp.int32)
    cnt_ref[0] = jnp.broadcast_to(counts, cnt_ref.shape[1:])


def _mix_out_and_route(o, wo, layer, x, gate, gain, shift, scale, router_w, router_b, *, n_b, seq):
    t, d = x.shape
    tm = ROUTE_TILE
    n_s = seq // tm
    per_batch = pl.BlockSpec((1, 1, d), lambda i: (i // n_s, 0, 0))
    row = pl.BlockSpec((tm, d), lambda i: (i, 0))
    sel = pl.BlockSpec((TOP_K, tm), lambda i: (0, i))
    return pl.pallas_call(
        functools.partial(_mix_out_kernel, tm=tm),
        grid=(t // tm,),
        in_specs=[
            row,
            pl.BlockSpec((None,) + wo.shape[1:], lambda i: (layer, 0, 0)),
            row, per_batch,
            pl.BlockSpec((1, d), lambda i: (0, 0)),
            per_batch, per_batch,
            pl.BlockSpec((N_EXPERTS, d), lambda i: (0, 0)),
            pl.BlockSpec((N_EXPERTS, 1), lambda i: (0, 0)),
        ],
        out_specs=[row, row, sel, sel, sel,
                   pl.BlockSpec((1, N_EXPERTS, LANES), lambda i: (i, 0, 0))],
        out_shape=[
            jax.ShapeDtypeStruct((t, d), F32),
            jax.ShapeDtypeStruct((t, d), BF16),
            jax.ShapeDtypeStruct((TOP_K, t), jnp.int32),
            jax.ShapeDtypeStruct((TOP_K, t), F32),
            jax.ShapeDtypeStruct((TOP_K, t), jnp.int32),
            jax.ShapeDtypeStruct((t // tm, N_EXPERTS, LANES), jnp.int32),
        ],
        compiler_params=_params(("parallel",)),
        name="mix_out_route",
    )(o, wo, x, gate, gain.reshape(1, d), shift, scale, router_w.T, router_b.reshape(N_EXPERTS, 1))


def _group_copies(tile, tstart_ref, toff_ref, tnp_ref, make_copy, act):
    def per_expert(e, carry):
        j = tile * N_EXPERTS + e
        local, glob, n = toff_ref[j], tstart_ref[j], tnp_ref[j]
        n_big = n // BIG_COPY

        def big(c, carry):
            off = c * BIG_COPY
            act(make_copy(pl.multiple_of(local + off, GROUP_ALIGN),
                          pl.multiple_of(glob + off, GROUP_ALIGN), BIG_COPY))
            return carry

        lax.fori_loop(0, n_big, big, 0)
        done = n_big * BIG_COPY

        def small(c, carry):
            off = done + c * GROUP_ALIGN
            act(make_copy(pl.multiple_of(local + off, GROUP_ALIGN),
                          pl.multiple_of(glob + off, GROUP_ALIGN), GROUP_ALIGN))
            return carry

        lax.fori_loop(0, (n - done) // GROUP_ALIGN, small, 0)
        return carry

    lax.fori_loop(0, N_EXPERTS, per_expert, 0)


def _dispatch_kernel(tstart_ref, toff_ref, tnp_ref, fill_row_ref, fill_on_ref, nvalid_ref,
                     h_ref, idx_ref, rank_ref, off_ref, xs_ref, loc_ref, z_ref, zero_ref, sem,
                     *, n_blocks):
    i = pl.program_id(0)
    tm = h_ref.shape[0]

    @pl.when(i == 0)
    def _():
        zero_ref[...] = jnp.zeros_like(zero_ref)

        def fill(row):
            return pltpu.make_async_copy(
                zero_ref, xs_ref.at[pl.ds(pl.multiple_of(row, EXPERT_ROWS), EXPERT_ROWS)], sem)

        def fills(act):
            def last_block(e, carry):
                @pl.when(fill_on_ref[e] == 1)
                def _():
                    act(fill(fill_row_ref[e]))
                return carry

            lax.fori_loop(0, N_EXPERTS, last_block, 0)

            def tail_block(b, carry):
                act(fill(b * EXPERT_ROWS))
                return carry

            lax.fori_loop(nvalid_ref[0], n_blocks, tail_block, 0)

        fills(lambda cp: cp.start())
        fills(lambda cp: cp.wait())

    e_iota = lax.broadcasted_iota(jnp.int32, (N_EXPERTS, tm), 0)
    off = off_ref[0][:, 0:1]
    locs = []
    for k in range(TOP_K):
        sel = e_iota == idx_ref[k:k + 1, :]
        loc = jnp.sum(jnp.where(sel, off, 0), axis=0, keepdims=True) + rank_ref[k:k + 1, :]
        loc_ref[k:k + 1, :] = loc
        locs.append(loc)

    h = h_ref[...]
    rows = 256
    for c in range(COMPACT_ROWS // rows):
        r_iota = lax.broadcasted_iota(jnp.int32, (rows, tm), 0) + c * rows
        hit = jnp.zeros((rows, tm), F32)
        for loc in locs:
            hit = jnp.where(r_iota == loc, 1.0, hit)
        z_ref[c * rows:(c + 1) * rows, :] = jnp.dot(hit.astype(BF16), h,
                                                    preferred_element_type=F32)

    def make_copy(local, glob, n):
        return pltpu.make_async_copy(z_ref.at[pl.ds(local, n)], xs_ref.at[pl.ds(glob, n)], sem)

    _group_copies(i, tstart_ref, toff_ref, tnp_ref, make_copy, lambda cp: cp.start())
    _group_copies(i, tstart_ref, toff_ref, tnp_ref, make_copy, lambda cp: cp.wait())


def _dispatch(h, idx, rank, off_b, tables, fills, n_rows):
    t, d = h.shape
    tm = ROUTE_TILE
    sel = pl.BlockSpec((TOP_K, tm), lambda i, *_: (0, i))
    grid_spec = pltpu.PrefetchScalarGridSpec(
        num_scalar_prefetch=6,
        grid=(t // tm,),
        in_specs=[
            pl.BlockSpec((tm, d), lambda i, *_: (i, 0)),
            sel, sel,
            pl.BlockSpec((1, N_EXPERTS, LANES), lambda i, *_: (i, 0, 0)),
        ],
        out_specs=[pl.BlockSpec(memory_space=pl.ANY), sel],
        scratch_shapes=[pltpu.VMEM((COMPACT_ROWS, d), F32), pltpu.VMEM((EXPERT_ROWS, d), F32),
                        pltpu.SemaphoreType.DMA(())],
    )
    return pl.pallas_call(
        functools.partial(_dispatch_kernel, n_blocks=n_rows // EXPERT_ROWS),
        grid_spec=grid_spec,
        out_shape=[jax.ShapeDtypeStruct((n_rows, d), F32),
                   jax.ShapeDtypeStruct((TOP_K, t), jnp.int32)],
        compiler_params=_params(("arbitrary",)),
        name="moe_dispatch",
    )(*tables, *fills, h, idx, rank, off_b)


def _expert_kernel(be_ref, first_ref, nvalid_ref, xs_ref, wgu_ref, bgu_ref, wd_ref, bd_ref,
                   ys_ref, wgu_bf, wd_bf):
    b = pl.program_id(0)

    @pl.when(first_ref[b] == 1)
    def _():
        wgu_bf[...] = wgu_ref[0].astype(BF16)
        wd_bf[...] = wd_ref[0].astype(BF16)

    @pl.when(b < nvalid_ref[0])
    def _():
        d_ff = wd_bf.shape[0]
        x = xs_ref[...].astype(BF16)
        gu = jnp.dot(x, wgu_bf[...], preferred_element_type=F32) + bgu_ref[0]
        gate = jnp.minimum(gu[:, :d_ff], SWIGLU_LIMIT)
        up = jnp.clip(gu[:, d_ff:], -SWIGLU_LIMIT, SWIGLU_LIMIT)
        glu = gate * jax.nn.sigmoid(SWIGLU_ALPHA * gate)
        act = ((up + 1.0) * glu).astype(BF16)
        ys_ref[...] = jnp.dot(act, wd_bf[...], preferred_element_type=F32) + bd_ref[0]

    @pl.when(b >= nvalid_ref[0])
    def _():
        ys_ref[...] = jnp.zeros_like(ys_ref)


def _experts(xs, block_expert, block_first, n_valid, layer, w_gu, b_gu, w_d, b_d):
    n_rows, d = xs.shape
    bm = EXPERT_ROWS
    d_ff = w_d.shape[2]
    n_l = w_gu.shape[0]
    grid_spec = pltpu.PrefetchScalarGridSpec(
        num_scalar_prefetch=3,
        grid=(n_rows // bm,),
        in_specs=[
            pl.BlockSpec((bm, d), lambda b, be, fi, nv: (jnp.minimum(b, nv[0] - 1), 0)),
            pl.BlockSpec((None, 1, d, 2 * d_ff), lambda b, be, fi, nv: (layer, be[b], 0, 0)),
            pl.BlockSpec((None, 1, 1, 2 * d_ff), lambda b, be, fi, nv: (layer, be[b], 0, 0)),
            pl.BlockSpec((None, 1, d_ff, d), lambda b, be, fi, nv: (layer, be[b], 0, 0)),
            pl.BlockSpec((None, 1, 1, d), lambda b, be, fi, nv: (layer, be[b], 0, 0)),
        ],
        out_specs=pl.BlockSpec((bm, d), lambda b, be, fi, nv: (b, 0)),
        scratch_shapes=[pltpu.VMEM((d, 2 * d_ff), BF16), pltpu.VMEM((d_ff, d), BF16)],
    )
    return pl.pallas_call(
        _expert_kernel,
        grid_spec=grid_spec,
        out_shape=jax.ShapeDtypeStruct((n_rows, d), F32),
        compiler_params=_params(("arbitrary",)),
        name="moe_experts",
    )(block_expert, block_first, n_valid, xs, w_gu, b_gu.reshape(n_l, N_EXPERTS, 1, 2 * d_ff),
      w_d, b_d.reshape(n_l, N_EXPERTS, 1, d))


def _combine_kernel(tstart_ref, toff_ref, tnp_ref, loc_ref, wgt_ref, ys_ref, x_ref, gate_ref,
                    o_ref, y_ref, sem):
    i = pl.program_id(0)
    tm = x_ref.shape[0]

    @pl.when(i == 0)
    def _():
        y_ref[...] = jnp.zeros_like(y_ref)

    def make_copy(local, glob, n):
        return pltpu.make_async_copy(ys_ref.at[pl.ds(glob, n)], y_ref.at[pl.ds(local, n)], sem)

    _group_copies(i, tstart_ref, toff_ref, tnp_ref, make_copy, lambda cp: cp.start())
    _group_copies(i, tstart_ref, toff_ref, tnp_ref, make_copy, lambda cp: cp.wait())

    c_iota = lax.broadcasted_iota(jnp.int32, (tm, COMPACT_ROWS), 1)
    q = jnp.zeros((tm, COMPACT_ROWS), F32)
    for k in range(TOP_K):
        q = jnp.where(c_iota == loc_ref[:, k:k + 1], wgt_ref[:, k:k + 1], q)
    q_hi = q.astype(BF16)
    q_lo = (q - q_hi.astype(F32)).astype(BF16)
    y = y_ref[...].astype(BF16)
    f = (jnp.dot(q_hi, y, preferred_element_type=F32)
         + jnp.dot(q_lo, y, preferred_element_type=F32))
    o_ref[...] = x_ref[...] + gate_ref[0] * f


def _combine(loc_tk, wgt_tk, ys, x, gate, tables, *, seq):
    t, d = x.shape
    tm = ROUTE_TILE
    n_s = seq // tm
    sel = pl.BlockSpec((tm, TOP_K), lambda i, *_: (i, 0))
    grid_spec = pltpu.PrefetchScalarGridSpec(
        num_scalar_prefetch=3,
        grid=(t // tm,),
        in_specs=[
            sel, sel,
            pl.BlockSpec(memory_space=pl.ANY),
            pl.BlockSpec((tm, d), lambda i, *_: (i, 0)),
            pl.BlockSpec((1, 1, d), lambda i, *_: (i // n_s, 0, 0)),
        ],
        out_specs=pl.BlockSpec((tm, d), lambda i, *_: (i, 0)),
        scratch_shapes=[pltpu.VMEM((COMPACT_ROWS, d), F32), pltpu.SemaphoreType.DMA(())],
    )
    return pl.pallas_call(
        _combine_kernel,
        grid_spec=grid_spec,
        out_shape=jax.ShapeDtypeStruct((t, d), F32),
        compiler_params=_params(("arbitrary",)),
        name="moe_combine",
    )(*tables, loc_tk, wgt_tk, ys, x, gate)


def _moe(h, idx, wgt, rank, tile_cnt, x, gate, layer, w_gu, b_gu, w_d, b_d, *, seq):
    t = h.shape[0]
    bm = EXPERT_ROWS
    n_tiles = t // ROUTE_TILE
    n_rows = t * TOP_K + n_tiles * N_EXPERTS * GROUP_ALIGN + N_EXPERTS * bm
    n_blocks = n_rows // bm
    cnt = tile_cnt[:, :, 0]
    grp = (cnt + GROUP_ALIGN - 1) // GROUP_ALIGN * GROUP_ALIGN
    padded = (jnp.sum(grp, axis=0) + bm - 1) // bm * bm
    pad_end = jnp.cumsum(padded)
    tstart = (pad_end - padded)[None, :] + jnp.cumsum(grp, axis=0) - grp
    toff = jnp.cumsum(grp, axis=1) - grp
    tables = tuple(a.reshape(-1).astype(jnp.int32) for a in (tstart, toff, grp))
    off_b = jnp.broadcast_to(toff[:, :, None], (n_tiles, N_EXPERTS, LANES)).astype(jnp.int32)
    block_row = jnp.arange(n_blocks, dtype=jnp.int32) * bm
    block_expert = jnp.minimum(
        jnp.sum((pad_end[None, :] <= block_row[:, None]).astype(jnp.int32), axis=1),
        N_EXPERTS - 1).astype(jnp.int32)
    block_first = jnp.concatenate(
        [jnp.ones((1,), jnp.int32), (block_expert[1:] != block_expert[:-1]).astype(jnp.int32)])
    n_valid = (pad_end[-1:] // bm).astype(jnp.int32)
    fills = (jnp.maximum(pad_end - bm, 0).astype(jnp.int32), (padded > 0).astype(jnp.int32),
             n_valid)
    xs, loc = _dispatch(h, idx, rank, off_b, tables, fills, n_rows)
    ys = _experts(xs, block_expert, block_first, n_valid, layer, w_gu, b_gu, w_d, b_d)
    return _combine(loc.T, wgt.T, ys, x, gate, tables, seq=seq)


def _final_kernel(x_ref, gain_ref, shift_ref, scale_ref, o_ref):
    o_ref[...] = _modulated_norm(x_ref[...], gain_ref[...], shift_ref[0], scale_ref[0])


def _final_modulate(x, gain, shift, scale, *, seq):
    t, d = x.shape
    tm = 512
    n_s = seq // tm
    per_batch = pl.BlockSpec((1, 1, d), lambda i: (i // n_s, 0, 0))
    return pl.pallas_call(
        _final_kernel,
        grid=(t // tm,),
        in_specs=[pl.BlockSpec((tm, d), lambda i: (i, 0)), pl.BlockSpec((1, d), lambda i: (0, 0)),
                  per_batch, per_batch],
        out_specs=pl.BlockSpec((tm, d), lambda i: (i, 0)),
        out_shape=jax.ShapeDtypeStruct((t, d), F32),
        compiler_params=_params(("parallel",)),
        name="final_modulate",
    )(x, gain.reshape(1, d), shift, scale)


def kernel(x, c, mod_w, mod_b, mix_norm, ffn_norm, a_wqkv, a_wo, a_lambda, a_subln, kv_norm, kv_mod_w, kv_mod_b, kv_w, b_wq, b_wo, router_w, router_b, exp_w_gate_up, exp_b_gate_up, exp_w_down, exp_b_down, final_norm, final_mod_w, final_mod_b):
    n_b, seq, d = x.shape
    t = n_b * seq
    sizes = dict(n_b=n_b, seq=seq)
    rope_tabs = _rope_tables(seq)
    q_scale = HEAD_DIM ** -0.5

    def per_batch(v):
        return v.reshape(n_b, 1, d)

    mod = _adaln_vectors(c, mod_w, mod_b)
    kv_mod = _adaln_vectors(c, kv_mod_w[None], kv_mod_b[None])[0]
    fin_mod = _adaln_vectors(c, final_mod_w[None], final_mod_b[None])[0]

    a_wqkv, a_wo, b_wq, b_wo = (w.astype(BF16) for w in (a_wqkv, a_wo, b_wq, b_wo))
    kv_w = kv_w[None].astype(BF16)

    xt = x.reshape(t, d)
    shared = None
    for layer in range(DEPTH):
        sh1, sc1, g1, sh2, sc2, g2 = (per_batch(mod[layer, :, j * d:(j + 1) * d]) for j in range(6))
        if layer < N_A_LAYERS:
            lambda_init = 0.8 - 0.6 * math.exp(-0.3 * layer)
            segs = ((0, True, q_scale, (1,)), (1, True, 1.0, (1,)), (2, False, 1.0, (1,)))
            q, k, v = _norm_project(xt, mix_norm[layer], sh1, sc1, a_wqkv, layer,
                                    rope_tabs, segs, name="a_qkv_proj", **sizes)
            o = _diff_attention(q, k, v, a_lambda[layer], a_subln[layer], lambda_init, **sizes)
            wo, wo_layer = a_wo, layer
        else:
            j = layer - N_A_LAYERS
            segs = tuple((g, True, q_scale, (B_DILATIONS[g],)) for g in range(3))
            q0, q1, q2 = _norm_project(xt, mix_norm[layer], sh1, sc1, b_wq, j,
                                       rope_tabs, segs, name="b_q_proj", **sizes)
            o = _dilated_attention(q0, q1, q2, *shared, **sizes)
            wo, wo_layer = b_wo, j
        xt, h, idx, wgt, rank, tile_cnt = _mix_out_and_route(
            o, wo, wo_layer, xt, g1, ffn_norm[layer], sh2, sc2, router_w[layer],
            router_b[layer], **sizes)
        xt = _moe(h, idx, wgt, rank, tile_cnt, xt, g2, layer, exp_w_gate_up, exp_b_gate_up,
                  exp_w_down, exp_b_down, seq=seq)
        if layer == N_A_LAYERS - 1:
            segs = ((0, True, 1.0, B_DILATIONS), (1, False, 1.0, B_DILATIONS))
            k0, k1, k2, v0, v1, v2 = _norm_project(
                xt, kv_norm, per_batch(kv_mod[:, :d]), per_batch(kv_mod[:, d:]),
                kv_w, 0, rope_tabs, segs, name="shared_kv_proj", **sizes)
            shared = (k0, v0, k1, v1, k2, v2)
    out = _final_modulate(xt, final_norm, per_batch(fin_mod[:, :d]), per_batch(fin_mod[:, d:]),
                          seq=seq)
    return out.reshape(n_b, seq, d)
```

```python
import functools
import math

import jax
import jax.numpy as jnp
from jax import lax
from jax.experimental import pallas as pl
from jax.experimental.pallas import tpu as pltpu

F32 = jnp.float32
BF16 = jnp.bfloat16

D_MODEL = 1024
HEAD_DIM = 64
A_HEADS = 8
B_KV_HEADS = 16
B_DILATIONS = (1, 4, 16)
WINDOW_STEPS = 128
ROPE_THETA = 500000.0
ROPE_DIM = HEAD_DIM // 4
N_EXPERTS = 32
TOP_K = 4
SWIGLU_LIMIT = 7.0
SWIGLU_ALPHA = 1.702
NORM_EPS = 1e-5
N_A_LAYERS = 2
DEPTH = 4

LANES = 128
EXPERT_ROWS = 512
ROUTE_TILE = 512
GROUP_ALIGN = 8
COMPACT_ROWS = ROUTE_TILE * TOP_K + N_EXPERTS * GROUP_ALIGN
BIG_COPY = 64
NEG_BIG = -1e30
VMEM_LIMIT = 56 * 1024 * 1024


def _params(sem, vmem=VMEM_LIMIT):
    return pltpu.CompilerParams(dimension_semantics=sem, vmem_limit_bytes=vmem)


def _adaln_kernel(c_ref, w_ref, b_ref, o_ref):
    c = c_ref[...]
    c_act = (c * jax.nn.sigmoid(c)).astype(BF16)
    o_ref[0] = jnp.dot(c_act, w_ref[0].astype(BF16), preferred_element_type=F32) + b_ref[0]


def _adaln_vectors(c, w, b):
    n_l, d, n = w.shape
    n_b = c.shape[0]
    tn = 1024
    return pl.pallas_call(
        _adaln_kernel,
        grid=(n_l, n // tn),
        in_specs=[
            pl.BlockSpec((n_b, d), lambda l, j: (0, 0)),
            pl.BlockSpec((1, d, tn), lambda l, j: (l, 0, j)),
            pl.BlockSpec((1, 1, tn), lambda l, j: (l, 0, j)),
        ],
        out_specs=pl.BlockSpec((1, n_b, tn), lambda l, j: (l, 0, j)),
        out_shape=jax.ShapeDtypeStruct((n_l, n_b, n), F32),
        compiler_params=_params(("parallel", "parallel")),
        name="adaln_vectors",
    )(c, w, b.reshape(n_l, 1, n))


def _modulated_norm(x, gain, shift, scale):
    y = x * lax.rsqrt(jnp.mean(x * x, axis=-1, keepdims=True) + NORM_EPS)
    return y * gain * (1.0 + scale) + shift


def _rope_tables(seq):
    inv = ROPE_THETA ** (-jnp.arange(0, ROPE_DIM, 2, dtype=F32) / ROPE_DIM)
    ang = jnp.arange(seq, dtype=F32)[:, None] * inv[None, :]
    cos, sin = jnp.cos(ang), jnp.sin(ang)
    half = ROPE_DIM // 2
    rest = HEAD_DIM - ROPE_DIM
    zeros = jnp.zeros((seq, half), F32)
    c_tab = jnp.concatenate([cos, cos, jnp.ones((seq, rest), F32)], axis=1)
    s1_tab = jnp.concatenate([zeros, sin, jnp.zeros((seq, rest), F32)], axis=1)
    s2_tab = jnp.concatenate([-sin, zeros, jnp.zeros((seq, rest), F32)], axis=1)
    rep = LANES // HEAD_DIM
    return tuple(jnp.tile(t, (1, rep)) for t in (c_tab, s1_tab, s2_tab))


def _apply_rope(y, c_tab, s1_tab, s2_tab):
    parts = []
    for j in range(y.shape[1] // LANES):
        yj = y[:, j * LANES:(j + 1) * LANES]
        parts.append(yj * c_tab + pltpu.roll(yj, ROPE_DIM // 2, 1) * s1_tab
                     + pltpu.roll(yj, LANES - ROPE_DIM // 2, 1) * s2_tab)
    return jnp.concatenate(parts, axis=1)


def _proj_kernel(x_ref, gain_ref, shift_ref, scale_ref, w_ref, rc_ref, rs1_ref, rs2_ref, *rest,
                 segs, tm):
    n_out = sum(len(s[3]) for s in segs)
    out_refs, scr_ref = rest[:n_out], rest[n_out]
    h = _modulated_norm(x_ref[...], gain_ref[...], shift_ref[0], scale_ref[0]).astype(BF16)
    oi = 0
    for chunk, rope, mult, dils in segs:
        y = jnp.dot(h, w_ref[:, chunk * D_MODEL:(chunk + 1) * D_MODEL], preferred_element_type=F32)
        if rope:
            y = _apply_rope(y, rc_ref[...], rs1_ref[...], rs2_ref[...])
        if mult != 1.0:
            y = y * mult
        for dil in dils:
            o_ref = out_refs[oi]
            oi += 1
            if dil == 1:
                o_ref[...] = y.astype(BF16)
            else:
                for j in range(D_MODEL // LANES):
                    scr_ref[j] = y[:, j * LANES:(j + 1) * LANES]
                for r in range(dil):
                    for j in range(D_MODEL // LANES):
                        o_ref[0, r, :, j * LANES:(j + 1) * LANES] = (
                            scr_ref[j, pl.ds(r, tm // dil, stride=dil), :].astype(BF16))


def _norm_project(x, gain, shift, scale, w, layer, rope_tabs, segs, *, n_b, seq, name):
    t, d = x.shape
    tm = 512
    n_s = seq // tm
    out_shapes, out_specs = [], []
    for _, _, _, dils in segs:
        for dil in dils:
            if dil == 1:
                out_shapes.append(jax.ShapeDtypeStruct((t, D_MODEL), BF16))
                out_specs.append(pl.BlockSpec((tm, D_MODEL), lambda i: (i, 0)))
            else:
                out_shapes.append(jax.ShapeDtypeStruct((n_b, dil, seq // dil, D_MODEL), BF16))
                out_specs.append(pl.BlockSpec((1, dil, tm // dil, D_MODEL),
                                              lambda i: (i // n_s, 0, i % n_s, 0)))
    per_batch = pl.BlockSpec((1, 1, d), lambda i: (i // n_s, 0, 0))
    rope_spec = pl.BlockSpec((tm, LANES), lambda i: (i % n_s, 0))
    return pl.pallas_call(
        functools.partial(_proj_kernel, segs=segs, tm=tm),
        grid=(t // tm,),
        in_specs=[
            pl.BlockSpec((tm, d), lambda i: (i, 0)),
            pl.BlockSpec((1, d), lambda i: (0, 0)),
            per_batch, per_batch,
            pl.BlockSpec((None,) + w.shape[1:], lambda i: (layer, 0, 0)),
            rope_spec, rope_spec, rope_spec,
        ],
        out_specs=out_specs,
        out_shape=out_shapes,
        scratch_shapes=[pltpu.VMEM((D_MODEL // LANES, tm, LANES), F32)],
        compiler_params=_params(("parallel",)),
        name=name,
    )(x, gain.reshape(1, d), shift, scale, w, *rope_tabs)


def _diff_attn_kernel(lam_ref, subln_ref, q_ref, k_ref, v_ref, o_ref, *, tq, lambda_init):
    i = pl.program_id(2)
    lf = lam_ref[...]
    lam = (jnp.exp(jnp.sum(lf[0:1] * lf[1:2], keepdims=True))
           - jnp.exp(jnp.sum(lf[2:3] * lf[3:4], keepdims=True)) + lambda_init)
    q = q_ref[...]
    lane_lo = lax.broadcasted_iota(jnp.int32, q.shape, 1) < HEAD_DIM
    zero = jnp.zeros_like(q)
    q1 = jnp.where(lane_lo, q, zero)
    q2 = jnp.where(lane_lo, zero, q)
    causal = (lax.broadcasted_iota(jnp.int32, (tq, tq), 0)
              >= lax.broadcasted_iota(jnp.int32, (tq, tq), 1))

    def scores(qm, kb):
        return lax.dot_general(qm, kb, (((1,), (1,)), ((), ())), preferred_element_type=F32)

    def online(s, m, l, acc, vb):
        m_new = jnp.maximum(m, jnp.max(s, axis=-1, keepdims=True))
        p = jnp.exp2(s - m_new)
        alpha = jnp.exp2(m - m_new)
        l = alpha * l + jnp.sum(p, axis=-1, keepdims=True)
        acc = alpha * acc + jnp.dot(p.astype(BF16), vb, preferred_element_type=F32)
        return m_new, l, acc

    def step(j, carry, masked):
        m1, l1, a1, m2, l2, a2 = carry
        rows = pl.ds(pl.multiple_of(j * tq, tq), tq)
        kb = k_ref[rows, :]
        vb = v_ref[rows, :]
        s1 = scores(q1, kb)
        s2 = scores(q2, kb)
        if masked:
            s1 = jnp.where(causal, s1, NEG_BIG)
            s2 = jnp.where(causal, s2, NEG_BIG)
        m1, l1, a1 = online(s1, m1, l1, a1, vb)
        m2, l2, a2 = online(s2, m2, l2, a2, vb)
        return m1, l1, a1, m2, l2, a2

    col = jnp.full((tq, 1), NEG_BIG, F32)
    zcol = jnp.zeros((tq, 1), F32)
    zacc = jnp.zeros((tq, 2 * HEAD_DIM), F32)
    carry = lax.fori_loop(0, i, lambda j, c: step(j, c, False), (col, zcol, zacc, col, zcol, zacc))
    m1, l1, a1, m2, l2, a2 = step(i, carry, True)
    o = a1 / l1 - lam * (a2 / l2)
    o = o * lax.rsqrt(jnp.mean(o * o, axis=-1, keepdims=True) + NORM_EPS)
    o_ref[...] = (o * subln_ref[...] * (1.0 - lambda_init)).astype(BF16)


def _diff_attention(q, k, v, lam_vecs, subln, lambda_init, *, n_b, seq):
    t = q.shape[0]
    tq = 512
    n_q = seq // tq
    width = 2 * HEAD_DIM
    kv_spec = pl.BlockSpec((seq, width), lambda b, h, i: (b, h))
    return pl.pallas_call(
        functools.partial(_diff_attn_kernel, tq=tq, lambda_init=lambda_init),
        grid=(n_b, A_HEADS, n_q),
        in_specs=[
            pl.BlockSpec((4, HEAD_DIM), lambda b, h, i: (0, 0)),
            pl.BlockSpec((1, width), lambda b, h, i: (0, 0)),
            pl.BlockSpec((tq, width), lambda b, h, i: (b * n_q + i, h)),
            kv_spec, kv_spec,
        ],
        out_specs=pl.BlockSpec((tq, width), lambda b, h, i: (b * n_q + i, h)),
        out_shape=jax.ShapeDtypeStruct((t, D_MODEL), BF16),
        compiler_params=_params(("parallel", "parallel", "parallel")),
        name="diff_attention",
    )(lam_vecs, subln.reshape(1, width), q, k, v)


def _window_unit(q, kc, vc, bias, lane_lo):
    zero = jnp.zeros_like(q)
    qq = jnp.concatenate([jnp.where(lane_lo, q, zero), jnp.where(lane_lo, zero, q)], axis=0)
    s = lax.dot_general(qq, kc, (((1,), (1,)), ((), ())), preferred_element_type=F32) + bias
    m = jnp.max(s, axis=-1, keepdims=True)
    p = jnp.exp2(s - m).astype(BF16)
    v_aug = jnp.concatenate([vc, jnp.ones(vc.shape, BF16)], axis=1)
    o_aug = jnp.dot(p, v_aug, preferred_element_type=F32)
    n = WINDOW_STEPS
    o = jnp.where(lane_lo, o_aug[:n, :LANES], o_aug[n:, :LANES])
    l = jnp.where(lane_lo, o_aug[:n, LANES:], o_aug[n:, LANES:])
    m_sel = jnp.where(lane_lo, m[:n], m[n:])
    return o / l, m_sel + jnp.log2(l)


def _dil_attn_kernel(q0_ref, q1_ref, q2_ref, k0_ref, v0_ref, k1_ref, v1_ref, k2_ref, v2_ref,
                     o_ref, acc_ref, lse_ref, *, seq):
    n = WINDOW_STEPS
    lane_lo = lax.broadcasted_iota(jnp.int32, (n, LANES), 1) < HEAD_DIM
    qi = lax.broadcasted_iota(jnp.int32, (2 * n, 2 * n), 0) % n
    kj = lax.broadcasted_iota(jnp.int32, (2 * n, 2 * n), 1)
    band = jnp.where((kj >= qi) & (kj <= qi + n), 0.0, NEG_BIG)
    first = jnp.where(lax.broadcasted_iota(jnp.int32, (2 * n, n), 1)
                      <= lax.broadcasted_iota(jnp.int32, (2 * n, n), 0) % n, 0.0, NEG_BIG)

    def unit(g, q_rows, k_rows, v_rows, bias, dst):
        o, lse = _window_unit(q_rows, k_rows, v_rows, bias, lane_lo)
        acc_ref[g, dst, :] = o
        lse_ref[g, dst, :] = lse

    def rows(ref, r, lo, hi):
        return ref[lo:hi, :] if len(ref.shape) == 2 else ref[0, r, lo:hi, :]

    for g, (q_ref, k_ref, v_ref) in enumerate(((q0_ref, k0_ref, v0_ref), (q1_ref, k1_ref, v1_ref),
                                               (q2_ref, k2_ref, v2_ref))):
        dil = B_DILATIONS[g]
        for r in range(dil):
            for blk in range(seq // dil // n):
                dst = pl.ds(r + blk * n * dil, n, stride=dil) if dil > 1 else pl.ds(blk * n, n)
                k_lo = max(blk - 1, 0) * n
                unit(g, rows(q_ref, r, blk * n, (blk + 1) * n), rows(k_ref, r, k_lo, (blk + 1) * n),
                     rows(v_ref, r, k_lo, (blk + 1) * n), first if blk == 0 else band, dst)

    l0, l1, l2 = lse_ref[0], lse_ref[1], lse_ref[2]
    top = jnp.maximum(jnp.maximum(l0, l1), l2)
    w0, w1, w2 = jnp.exp2(l0 - top), jnp.exp2(l1 - top), jnp.exp2(l2 - top)
    o = (w0 * acc_ref[0] + w1 * acc_ref[1] + w2 * acc_ref[2]) / (w0 + w1 + w2)
    o_ref[...] = o.astype(BF16)


def _dilated_attention(q0, q1, q2, k0, v0, k1, v1, k2, v2, *, n_b, seq):
    t = q0.shape[0]
    pairs = D_MODEL // LANES
    nat = pl.BlockSpec((seq, LANES), lambda b, p: (b, p))

    def res(dil):
        return pl.BlockSpec((1, dil, seq // dil, LANES), lambda b, p: (b, 0, 0, p))

    d1, d2 = B_DILATIONS[1], B_DILATIONS[2]
    return pl.pallas_call(
        functools.partial(_dil_attn_kernel, seq=seq),
        grid=(n_b, pairs),
        in_specs=[nat, res(d1), res(d2), nat, nat, res(d1), res(d1), res(d2), res(d2)],
        out_specs=nat,
        out_shape=jax.ShapeDtypeStruct((t, D_MODEL), BF16),
        scratch_shapes=[pltpu.VMEM((3, seq, LANES), F32), pltpu.VMEM((3, seq, LANES), F32)],
        compiler_params=_params(("parallel", "parallel")),
        name="dilated_attention",
    )(q0, q1, q2, k0, v0, k1, v1, k2, v2)


def _mix_out_kernel(o_ref, wo_ref, x_ref, gate_ref, gain_ref, shift_ref, scale_ref, rw_ref, rb_ref,
                    xn_ref, h_ref, idx_ref, wgt_ref, rank_ref, cnt_ref, *, tm):
    y = jnp.dot(o_ref[...], wo_ref[...], preferred_element_type=F32)
    xn = x_ref[...] + gate_ref[0] * y
    xn_ref[...] = xn
    h = _modulated_norm(xn, gain_ref[...], shift_ref[0], scale_ref[0])

    h_hi = h.astype(BF16)
    h_ref[...] = h_hi
    h_lo = (h - h_hi.astype(F32)).astype(BF16)
    rw = rw_ref[...]
    rw_hi = rw.astype(BF16)
    rw_lo = (rw - rw_hi.astype(F32)).astype(BF16)
    nt = (((1,), (1,)), ((), ()))
    logits = (lax.dot_general(rw_hi, h_hi, nt, preferred_element_type=F32)
              + lax.dot_general(rw_lo, h_hi, nt, preferred_element_type=F32)
              + lax.dot_general(rw_hi, h_lo, nt, preferred_element_type=F32)) + rb_ref[...]

    e_iota = lax.broadcasted_iota(jnp.int32, logits.shape, 0)
    work = logits
    sels, tops, idxs = [], [], []
    for _ in range(TOP_K):
        mk = jnp.max(work, axis=0, keepdims=True)
        ik = jnp.min(jnp.where(work == mk, e_iota, N_EXPERTS), axis=0, keepdims=True)
        sel = e_iota == ik
        work = jnp.where(sel, -jnp.inf, work)
        sels.append(sel)
        tops.append(mk)
        idxs.append(ik)
    exps = [jnp.exp(m - tops[0]) for m in tops]
    denom = exps[0] + exps[1] + exps[2] + exps[3]
    for k in range(TOP_K):
        idx_ref[k:k + 1, :] = idxs[k]
        wgt_ref[k:k + 1, :] = exps[k] / denom

    chosen = jnp.zeros(logits.shape, F32)
    for sel in sels:
        chosen = chosen + jnp.where(sel, 1.0, 0.0)
    before = (lax.broadcasted_iota(jnp.int32, (tm, tm), 0)
              < lax.broadcasted_iota(jnp.int32, (tm, tm), 1))
    upper = jnp.where(before, 1.0, 0.0).astype(BF16)
    prefix = jnp.dot(chosen.astype(BF16), upper, preferred_element_type=F32)
    for k in range(TOP_K):
        rank_ref[k:k + 1, :] = jnp.sum(jnp.where(sels[k], prefix, 0.0), axis=0,
                                       keepdims=True).astype(jnp.int32)
    counts = jnp.sum(chosen, axis=1, keepdims=True).astype(jnp.int32)
    cnt_ref[0] = jnp.broadcast_to(counts, cnt_ref.shape[1:])


def _mix_out_and_route(o, wo, layer, x, gate, gain, shift, scale, router_w, router_b, *, n_b, seq):
    t, d = x.shape
    tm = ROUTE_TILE
    n_s = seq // tm
    per_batch = pl.BlockSpec((1, 1, d), lambda i: (i // n_s, 0, 0))
    row = pl.BlockSpec((tm, d), lambda i: (i, 0))
    sel = pl.BlockSpec((TOP_K, tm), lambda i: (0, i))
    return pl.pallas_call(
        functools.partial(_mix_out_kernel, tm=tm),
        grid=(t // tm,),
        in_specs=[
            row,
            pl.BlockSpec((None,) + wo.shape[1:], lambda i: (layer, 0, 0)),
            row, per_batch,
            pl.BlockSpec((1, d), lambda i: (0, 0)),
            per_batch, per_batch,
            pl.BlockSpec((N_EXPERTS, d), lambda i: (0, 0)),
            pl.BlockSpec((N_EXPERTS, 1), lambda i: (0, 0)),
        ],
        out_specs=[row, row, sel, sel, sel,
                   pl.BlockSpec((1, N_EXPERTS, LANES), lambda i: (i, 0, 0))],
        out_shape=[
            jax.ShapeDtypeStruct((t, d), F32),
            jax.ShapeDtypeStruct((t, d), BF16),
            jax.ShapeDtypeStruct((TOP_K, t), jnp.int32),
            jax.ShapeDtypeStruct((TOP_K, t), F32),
            jax.ShapeDtypeStruct((TOP_K, t), jnp.int32),
            jax.ShapeDtypeStruct((t // tm, N_EXPERTS, LANES), jnp.int32),
        ],
        compiler_params=_params(("parallel",)),
        name="mix_out_route",
    )(o, wo, x, gate, gain.reshape(1, d), shift, scale, router_w.T, router_b.reshape(N_EXPERTS, 1))


def _group_copies(tile, tstart_ref, toff_ref, tnp_ref, make_copy, act):
    def per_expert(e, carry):
        j = tile * N_EXPERTS + e
        local, glob, n = toff_ref[j], tstart_ref[j], tnp_ref[j]
        n_big = n // BIG_COPY

        def big(c, carry):
            off = c * BIG_COPY
            act(make_copy(pl.multiple_of(local + off, GROUP_ALIGN),
                          pl.multiple_of(glob + off, GROUP_ALIGN), BIG_COPY))
            return carry

        lax.fori_loop(0, n_big, big, 0)
        done = n_big * BIG_COPY

        def small(c, carry):
            off = done + c * GROUP_ALIGN
            act(make_copy(pl.multiple_of(local + off, GROUP_ALIGN),
                          pl.multiple_of(glob + off, GROUP_ALIGN), GROUP_ALIGN))
            return carry

        lax.fori_loop(0, (n - done) // GROUP_ALIGN, small, 0)
        return carry

    lax.fori_loop(0, N_EXPERTS, per_expert, 0)


def _dispatch_kernel(tstart_ref, toff_ref, tnp_ref, fill_row_ref, fill_on_ref, nvalid_ref,
                     h_ref, idx_ref, rank_ref, off_ref, xs_ref, loc_ref, z_ref, zero_ref, sems,
                     *, n_blocks):
    i = pl.program_id(0)
    last = pl.num_programs(0) - 1
    slot = i % 2
    tm = h_ref.shape[0]

    def copies(tile, buf, act):
        def make_copy(local, glob, n):
            return pltpu.make_async_copy(z_ref.at[buf, pl.ds(local, n)],
                                         xs_ref.at[pl.ds(glob, n)], sems.at[buf])

        _group_copies(tile, tstart_ref, toff_ref, tnp_ref, make_copy, act)

    @pl.when(i == 0)
    def _():
        zero_ref[...] = jnp.zeros_like(zero_ref)

        def fill(row):
            return pltpu.make_async_copy(
                zero_ref, xs_ref.at[pl.ds(pl.multiple_of(row, EXPERT_ROWS), EXPERT_ROWS)],
                sems.at[2])

        def fills(act):
            def last_block(e, carry):
                @pl.when(fill_on_ref[e] == 1)
                def _():
                    act(fill(fill_row_ref[e]))
                return carry

            lax.fori_loop(0, N_EXPERTS, last_block, 0)

            def tail_block(b, carry):
                act(fill(b * EXPERT_ROWS))
                return carry

            lax.fori_loop(nvalid_ref[0], n_blocks, tail_block, 0)

        fills(lambda cp: cp.start())
        fills(lambda cp: cp.wait())

    e_iota = lax.broadcasted_iota(jnp.int32, (N_EXPERTS, tm), 0)
    off = off_ref[0][:, 0:1]
    locs = []
    for k in range(TOP_K):
        sel = e_iota == idx_ref[k:k + 1, :]
        loc = jnp.sum(jnp.where(sel, off, 0), axis=0, keepdims=True) + rank_ref[k:k + 1, :]
        loc_ref[k:k + 1, :] = loc
        locs.append(loc)

    @pl.when(i >= 2)
    def _():
        copies(i - 2, slot, lambda cp: cp.wait())

    h = h_ref[...]
    rows = 256
    for c in range(COMPACT_ROWS // rows):
        r_iota = lax.broadcasted_iota(jnp.int32, (rows, tm), 0) + c * rows
        hit = jnp.zeros((rows, tm), F32)
        for loc in locs:
            hit = jnp.where(r_iota == loc, 1.0, hit)
        z_ref[slot, c * rows:(c + 1) * rows, :] = jnp.dot(hit.astype(BF16), h,
                                                          preferred_element_type=F32)

    copies(i, slot, lambda cp: cp.start())

    @pl.when(i == last)
    def _():
        @pl.when(i >= 1)
        def _():
            copies(i - 1, 1 - slot, lambda cp: cp.wait())

        copies(i, slot, lambda cp: cp.wait())


def _dispatch(h, idx, rank, off_b, tables, fills, n_rows):
    t, d = h.shape
    tm = ROUTE_TILE
    sel = pl.BlockSpec((TOP_K, tm), lambda i, *_: (0, i))
    grid_spec = pltpu.PrefetchScalarGridSpec(
        num_scalar_prefetch=6,
        grid=(t // tm,),
        in_specs=[
            pl.BlockSpec((tm, d), lambda i, *_: (i, 0)),
            sel, sel,
            pl.BlockSpec((1, N_EXPERTS, LANES), lambda i, *_: (i, 0, 0)),
        ],
        out_specs=[pl.BlockSpec(memory_space=pl.ANY), sel],
        scratch_shapes=[pltpu.VMEM((2, COMPACT_ROWS, d), F32), pltpu.VMEM((EXPERT_ROWS, d), F32),
                        pltpu.SemaphoreType.DMA((3,))],
    )
    return pl.pallas_call(
        functools.partial(_dispatch_kernel, n_blocks=n_rows // EXPERT_ROWS),
        grid_spec=grid_spec,
        out_shape=[jax.ShapeDtypeStruct((n_rows, d), F32),
                   jax.ShapeDtypeStruct((TOP_K, t), jnp.int32)],
        compiler_params=_params(("arbitrary",)),
        name="moe_dispatch",
    )(*tables, *fills, h, idx, rank, off_b)


def _expert_kernel(be_ref, first_ref, nvalid_ref, xs_ref, wgu_ref, bgu_ref, wd_ref, bd_ref,
                   ys_ref, wgu_bf, wd_bf):
    b = pl.program_id(0)

    @pl.when(first_ref[b] == 1)
    def _():
        wgu_bf[...] = wgu_ref[0].astype(BF16)
        wd_bf[...] = wd_ref[0].astype(BF16)

    @pl.when(b < nvalid_ref[0])
    def _():
        d_ff = wd_bf.shape[0]
        x = xs_ref[...].astype(BF16)
        gu = jnp.dot(x, wgu_bf[...], preferred_element_type=F32) + bgu_ref[0]
        gate = jnp.minimum(gu[:, :d_ff], SWIGLU_LIMIT)
        up = jnp.clip(gu[:, d_ff:], -SWIGLU_LIMIT, SWIGLU_LIMIT)
        glu = gate * jax.nn.sigmoid(SWIGLU_ALPHA * gate)
        act = ((up + 1.0) * glu).astype(BF16)
        ys_ref[...] = jnp.dot(act, wd_bf[...], preferred_element_type=F32) + bd_ref[0]

    @pl.when(b >= nvalid_ref[0])
    def _():
        ys_ref[...] = jnp.zeros_like(ys_ref)


def _experts(xs, block_expert, block_first, n_valid, layer, w_gu, b_gu, w_d, b_d):
    n_rows, d = xs.shape
    bm = EXPERT_ROWS
    d_ff = w_d.shape[2]
    n_l = w_gu.shape[0]
    grid_spec = pltpu.PrefetchScalarGridSpec(
        num_scalar_prefetch=3,
        grid=(n_rows // bm,),
        in_specs=[
            pl.BlockSpec((bm, d), lambda b, be, fi, nv: (jnp.minimum(b, nv[0] - 1), 0)),
            pl.BlockSpec((None, 1, d, 2 * d_ff), lambda b, be, fi, nv: (layer, be[b], 0, 0)),
            pl.BlockSpec((None, 1, 1, 2 * d_ff), lambda b, be, fi, nv: (layer, be[b], 0, 0)),
            pl.BlockSpec((None, 1, d_ff, d), lambda b, be, fi, nv: (layer, be[b], 0, 0)),
            pl.BlockSpec((None, 1, 1, d), lambda b, be, fi, nv: (layer, be[b], 0, 0)),
        ],
        out_specs=pl.BlockSpec((bm, d), lambda b, be, fi, nv: (b, 0)),
        scratch_shapes=[pltpu.VMEM((d, 2 * d_ff), BF16), pltpu.VMEM((d_ff, d), BF16)],
    )
    return pl.pallas_call(
        _expert_kernel,
        grid_spec=grid_spec,
        out_shape=jax.ShapeDtypeStruct((n_rows, d), F32),
        compiler_params=_params(("arbitrary",)),
        name="moe_experts",
    )(block_expert, block_first, n_valid, xs, w_gu, b_gu.reshape(n_l, N_EXPERTS, 1, 2 * d_ff),
      w_d, b_d.reshape(n_l, N_EXPERTS, 1, d))


def _combine_kernel(tstart_ref, toff_ref, tnp_ref, loc_ref, wgt_ref, ys_ref, x_ref, gate_ref,
                    o_ref, y_ref, sems):
    i = pl.program_id(0)
    slot = i % 2
    tm = x_ref.shape[0]

    def copies(tile, buf, act):
        def make_copy(local, glob, n):
            return pltpu.make_async_copy(ys_ref.at[pl.ds(glob, n)],
                                         y_ref.at[buf, pl.ds(local, n)], sems.at[buf])

        _group_copies(tile, tstart_ref, toff_ref, tnp_ref, make_copy, act)

    @pl.when(i == 0)
    def _():
        y_ref[...] = jnp.zeros_like(y_ref)
        copies(0, 0, lambda cp: cp.start())

    copies(i, slot, lambda cp: cp.wait())

    @pl.when(i + 1 < pl.num_programs(0))
    def _():
        copies(i + 1, 1 - slot, lambda cp: cp.start())

    rows = COMPACT_ROWS // 3
    f = jnp.zeros((tm, y_ref.shape[2]), F32)
    for c in range(COMPACT_ROWS // rows):
        c_iota = lax.broadcasted_iota(jnp.int32, (tm, rows), 1) + c * rows
        q = jnp.zeros((tm, rows), F32)
        for k in range(TOP_K):
            q = jnp.where(c_iota == loc_ref[:, k:k + 1], wgt_ref[:, k:k + 1], q)
        q_hi = q.astype(BF16)
        q_lo = (q - q_hi.astype(F32)).astype(BF16)
        y = y_ref[slot, c * rows:(c + 1) * rows, :].astype(BF16)
        f = f + (jnp.dot(q_hi, y, preferred_element_type=F32)
                 + jnp.dot(q_lo, y, preferred_element_type=F32))
    o_ref[...] = x_ref[...] + gate_ref[0] * f


def _combine(loc_tk, wgt_tk, ys, x, gate, tables, *, seq):
    t, d = x.shape
    tm = ROUTE_TILE
    n_s = seq // tm
    sel = pl.BlockSpec((tm, TOP_K), lambda i, *_: (i, 0))
    grid_spec = pltpu.PrefetchScalarGridSpec(
        num_scalar_prefetch=3,
        grid=(t // tm,),
        in_specs=[
            sel, sel,
            pl.BlockSpec(memory_space=pl.ANY),
            pl.BlockSpec((tm, d), lambda i, *_: (i, 0)),
            pl.BlockSpec((1, 1, d), lambda i, *_: (i // n_s, 0, 0)),
        ],
        out_specs=pl.BlockSpec((tm, d), lambda i, *_: (i, 0)),
        scratch_shapes=[pltpu.VMEM((2, COMPACT_ROWS, d), F32), pltpu.SemaphoreType.DMA((2,))],
    )
    return pl.pallas_call(
        _combine_kernel,
        grid_spec=grid_spec,
        out_shape=jax.ShapeDtypeStruct((t, d), F32),
        compiler_params=_params(("arbitrary",)),
        name="moe_combine",
    )(*tables, loc_tk, wgt_tk, ys, x, gate)


def _moe(h, idx, wgt, rank, tile_cnt, x, gate, layer, w_gu, b_gu, w_d, b_d, *, seq):
    t = h.shape[0]
    bm = EXPERT_ROWS
    n_tiles = t // ROUTE_TILE
    n_rows = t * TOP_K + n_tiles * N_EXPERTS * GROUP_ALIGN + N_EXPERTS * bm
    n_blocks = n_rows // bm
    cnt = tile_cnt[:, :, 0]
    grp = (cnt + GROUP_ALIGN - 1) // GROUP_ALIGN * GROUP_ALIGN
    padded = (jnp.sum(grp, axis=0) + bm - 1) // bm * bm
    pad_end = jnp.cumsum(padded)
    tstart = (pad_end - padded)[None, :] + jnp.cumsum(grp, axis=0) - grp
    toff = jnp.cumsum(grp, axis=1) - grp
    tables = tuple(a.reshape(-1).astype(jnp.int32) for a in (tstart, toff, grp))
    off_b = jnp.broadcast_to(toff[:, :, None], (n_tiles, N_EXPERTS, LANES)).astype(jnp.int32)
    block_row = jnp.arange(n_blocks, dtype=jnp.int32) * bm
    block_expert = jnp.minimum(
        jnp.sum((pad_end[None, :] <= block_row[:, None]).astype(jnp.int32), axis=1),
        N_EXPERTS - 1).astype(jnp.int32)
    block_first = jnp.concatenate(
        [jnp.ones((1,), jnp.int32), (block_expert[1:] != block_expert[:-1]).astype(jnp.int32)])
    n_valid = (pad_end[-1:] // bm).astype(jnp.int32)
    fills = (jnp.maximum(pad_end - bm, 0).astype(jnp.int32), (padded > 0).astype(jnp.int32),
             n_valid)
    xs, loc = _dispatch(h, idx, rank, off_b, tables, fills, n_rows)
    ys = _experts(xs, block_expert, block_first, n_valid, layer, w_gu, b_gu, w_d, b_d)
    return _combine(loc.T, wgt.T, ys, x, gate, tables, seq=seq)


def _final_kernel(x_ref, gain_ref, shift_ref, scale_ref, o_ref):
    o_ref[...] = _modulated_norm(x_ref[...], gain_ref[...], shift_ref[0], scale_ref[0])


def _final_modulate(x, gain, shift, scale, *, seq):
    t, d = x.shape
    tm = 512
    n_s = seq // tm
    per_batch = pl.BlockSpec((1, 1, d), lambda i: (i // n_s, 0, 0))
    return pl.pallas_call(
        _final_kernel,
        grid=(t // tm,),
        in_specs=[pl.BlockSpec((tm, d), lambda i: (i, 0)), pl.BlockSpec((1, d), lambda i: (0, 0)),
                  per_batch, per_batch],
        out_specs=pl.BlockSpec((tm, d), lambda i: (i, 0)),
        out_shape=jax.ShapeDtypeStruct((t, d), F32),
        compiler_params=_params(("parallel",)),
        name="final_modulate",
    )(x, gain.reshape(1, d), shift, scale)


def kernel(x, c, mod_w, mod_b, mix_norm, ffn_norm, a_wqkv, a_wo, a_lambda, a_subln, kv_norm, kv_mod_w, kv_mod_b, kv_w, b_wq, b_wo, router_w, router_b, exp_w_gate_up, exp_b_gate_up, exp_w_down, exp_b_down, final_norm, final_mod_w, final_mod_b):
    n_b, seq, d = x.shape
    t = n_b * seq
    sizes = dict(n_b=n_b, seq=seq)
    rope_tabs = _rope_tables(seq)
    q_scale = HEAD_DIM ** -0.5 * math.log2(math.e)

    def per_batch(v):
        return v.reshape(n_b, 1, d)

    mod = _adaln_vectors(c, mod_w, mod_b)
    kv_mod = _adaln_vectors(c, kv_mod_w[None], kv_mod_b[None])[0]
    fin_mod = _adaln_vectors(c, final_mod_w[None], final_mod_b[None])[0]

    a_wqkv, a_wo, b_wq, b_wo = (w.astype(BF16) for w in (a_wqkv, a_wo, b_wq, b_wo))
    kv_w = kv_w[None].astype(BF16)

    xt = x.reshape(t, d)
    shared = None
    for layer in range(DEPTH):
        sh1, sc1, g1, sh2, sc2, g2 = (per_batch(mod[layer, :, j * d:(j + 1) * d]) for j in range(6))
        if layer < N_A_LAYERS:
            lambda_init = 0.8 - 0.6 * math.exp(-0.3 * layer)
            segs = ((0, True, q_scale, (1,)), (1, True, 1.0, (1,)), (2, False, 1.0, (1,)))
            q, k, v = _norm_project(xt, mix_norm[layer], sh1, sc1, a_wqkv, layer,
                                    rope_tabs, segs, name="a_qkv_proj", **sizes)
            o = _diff_attention(q, k, v, a_lambda[layer], a_subln[layer], lambda_init, **sizes)
            wo, wo_layer = a_wo, layer
        else:
            j = layer - N_A_LAYERS
            segs = tuple((g, True, q_scale, (B_DILATIONS[g],)) for g in range(3))
            q0, q1, q2 = _norm_project(xt, mix_norm[layer], sh1, sc1, b_wq, j,
                                       rope_tabs, segs, name="b_q_proj", **sizes)
            o = _dilated_attention(q0, q1, q2, *shared, **sizes)
            wo, wo_layer = b_wo, j
        xt, h, idx, wgt, rank, tile_cnt = _mix_out_and_route(
            o, wo, wo_layer, xt, g1, ffn_norm[layer], sh2, sc2, router_w[layer],
            router_b[layer], **sizes)
        xt = _moe(h, idx, wgt, rank, tile_cnt, xt, g2, layer, exp_w_gate_up, exp_b_gate_up,
                  exp_w_down, exp_b_down, seq=seq)
        if layer == N_A_LAYERS - 1:
            segs = ((0, True, 1.0, B_DILATIONS), (1, False, 1.0, B_DILATIONS))
            k0, k1, k2, v0, v1, v2 = _norm_project(
                xt, kv_norm, per_batch(kv_mod[:, :d]), per_batch(kv_mod[:, d:]),
                kv_w, 0, rope_tabs, segs, name="shared_kv_proj", **sizes)
            shared = (k0, v0, k1, v1, k2, v2)
    out = _final_modulate(xt, final_norm, per_batch(fin_mod[:, :d]), per_batch(fin_mod[:, d:]),
                          seq=seq)
    return out.reshape(n_b, seq, d)
```

```python
import functools
import math

import jax
import jax.numpy as jnp
from jax import lax
from jax.experimental import pallas as pl
from jax.experimental.pallas import tpu as pltpu

F32 = jnp.float32
BF16 = jnp.bfloat16

D_MODEL = 1024
HEAD_DIM = 64
A_HEADS = 8
B_KV_HEADS = 16
B_DILATIONS = (1, 4, 16)
WINDOW_STEPS = 128
ROPE_THETA = 500000.0
ROPE_DIM = HEAD_DIM // 4
N_EXPERTS = 32
TOP_K = 4
SWIGLU_LIMIT = 7.0
SWIGLU_ALPHA = 1.702
NORM_EPS = 1e-5
N_A_LAYERS = 2
DEPTH = 4

LANES = 128
EXPERT_ROWS = 512
ROUTE_TILE = 512
GROUP_ALIGN = 8
COMPACT_ROWS = ROUTE_TILE * TOP_K + N_EXPERTS * GROUP_ALIGN
COMBINE_CHUNK = COMPACT_ROWS // 2
BIG_COPY = 32
MAX_BIG = COMPACT_ROWS // BIG_COPY
MAX_SMALL = N_EXPERTS * (BIG_COPY // GROUP_ALIGN - 1)
ROW_BITS = 12
NEG_BIG = -1e30
VMEM_LIMIT = 56 * 1024 * 1024


def _params(sem, vmem=VMEM_LIMIT):
    return pltpu.CompilerParams(dimension_semantics=sem, vmem_limit_bytes=vmem)


def _adaln_kernel(c_ref, w_ref, b_ref, o_ref):
    c = c_ref[...]
    c_act = (c * jax.nn.sigmoid(c)).astype(BF16)
    o_ref[0] = jnp.dot(c_act, w_ref[0].astype(BF16), preferred_element_type=F32) + b_ref[0]


def _adaln_vectors(c, w, b):
    n_l, d, n = w.shape
    n_b = c.shape[0]
    tn = 1024
    return pl.pallas_call(
        _adaln_kernel,
        grid=(n_l, n // tn),
        in_specs=[
            pl.BlockSpec((n_b, d), lambda l, j: (0, 0)),
            pl.BlockSpec((1, d, tn), lambda l, j: (l, 0, j)),
            pl.BlockSpec((1, 1, tn), lambda l, j: (l, 0, j)),
        ],
        out_specs=pl.BlockSpec((1, n_b, tn), lambda l, j: (l, 0, j)),
        out_shape=jax.ShapeDtypeStruct((n_l, n_b, n), F32),
        compiler_params=_params(("parallel", "parallel")),
        name="adaln_vectors",
    )(c, w, b.reshape(n_l, 1, n))


def _modulated_norm(x, gain, shift, scale):
    y = x * lax.rsqrt(jnp.mean(x * x, axis=-1, keepdims=True) + NORM_EPS)
    return y * gain * (1.0 + scale) + shift


def _rope_tables(seq):
    inv = ROPE_THETA ** (-jnp.arange(0, ROPE_DIM, 2, dtype=F32) / ROPE_DIM)
    ang = jnp.arange(seq, dtype=F32)[:, None] * inv[None, :]
    cos, sin = jnp.cos(ang), jnp.sin(ang)
    half = ROPE_DIM // 2
    rest = HEAD_DIM - ROPE_DIM
    zeros = jnp.zeros((seq, half), F32)
    c_tab = jnp.concatenate([cos, cos, jnp.ones((seq, rest), F32)], axis=1)
    s1_tab = jnp.concatenate([zeros, sin, jnp.zeros((seq, rest), F32)], axis=1)
    s2_tab = jnp.concatenate([-sin, zeros, jnp.zeros((seq, rest), F32)], axis=1)
    rep = LANES // HEAD_DIM
    return tuple(jnp.tile(t, (1, rep)) for t in (c_tab, s1_tab, s2_tab))


def _apply_rope(y, c_tab, s1_tab, s2_tab):
    parts = []
    for j in range(y.shape[1] // LANES):
        yj = y[:, j * LANES:(j + 1) * LANES]
        parts.append(yj * c_tab + pltpu.roll(yj, ROPE_DIM // 2, 1) * s1_tab
                     + pltpu.roll(yj, LANES - ROPE_DIM // 2, 1) * s2_tab)
    return jnp.concatenate(parts, axis=1)


def _proj_kernel(x_ref, gain_ref, shift_ref, scale_ref, w_ref, rc_ref, rs1_ref, rs2_ref, *rest,
                 segs, tm):
    n_out = sum(len(s[3]) for s in segs)
    out_refs, scr_ref = rest[:n_out], rest[n_out]
    h = _modulated_norm(x_ref[...], gain_ref[...], shift_ref[0], scale_ref[0]).astype(BF16)
    oi = 0
    for chunk, rope, mult, dils in segs:
        y = jnp.dot(h, w_ref[:, chunk * D_MODEL:(chunk + 1) * D_MODEL], preferred_element_type=F32)
        if rope:
            y = _apply_rope(y, rc_ref[...], rs1_ref[...], rs2_ref[...])
        if mult != 1.0:
            y = y * mult
        for dil in dils:
            o_ref = out_refs[oi]
            oi += 1
            if dil == 1:
                o_ref[...] = y.astype(BF16)
            else:
                for j in range(D_MODEL // LANES):
                    scr_ref[j] = y[:, j * LANES:(j + 1) * LANES]
                for r in range(dil):
                    for j in range(D_MODEL // LANES):
                        o_ref[0, r, :, j * LANES:(j + 1) * LANES] = (
                            scr_ref[j, pl.ds(r, tm // dil, stride=dil), :].astype(BF16))


def _norm_project(x, gain, shift, scale, w, layer, rope_tabs, segs, *, n_b, seq, name):
    t, d = x.shape
    tm = 512
    n_s = seq // tm
    out_shapes, out_specs = [], []
    for _, _, _, dils in segs:
        for dil in dils:
            if dil == 1:
                out_shapes.append(jax.ShapeDtypeStruct((t, D_MODEL), BF16))
                out_specs.append(pl.BlockSpec((tm, D_MODEL), lambda i: (i, 0)))
            else:
                out_shapes.append(jax.ShapeDtypeStruct((n_b, dil, seq // dil, D_MODEL), BF16))
                out_specs.append(pl.BlockSpec((1, dil, tm // dil, D_MODEL),
                                              lambda i: (i // n_s, 0, i % n_s, 0)))
    per_batch = pl.BlockSpec((1, 1, d), lambda i: (i // n_s, 0, 0))
    rope_spec = pl.BlockSpec((tm, LANES), lambda i: (i % n_s, 0))
    return pl.pallas_call(
        functools.partial(_proj_kernel, segs=segs, tm=tm),
        grid=(t // tm,),
        in_specs=[
            pl.BlockSpec((tm, d), lambda i: (i, 0)),
            pl.BlockSpec((1, d), lambda i: (0, 0)),
            per_batch, per_batch,
            pl.BlockSpec((None,) + w.shape[1:], lambda i: (layer, 0, 0)),
            rope_spec, rope_spec, rope_spec,
        ],
        out_specs=out_specs,
        out_shape=out_shapes,
        scratch_shapes=[pltpu.VMEM((D_MODEL // LANES, tm, LANES), F32)],
        compiler_params=_params(("parallel",)),
        name=name,
    )(x, gain.reshape(1, d), shift, scale, w, *rope_tabs)


def _diff_attn_kernel(lam_ref, subln_ref, q_ref, k_ref, v_ref, o_ref, *, tq, lambda_init):
    lf = lam_ref[...]
    lam = (jnp.exp(jnp.sum(lf[0:1] * lf[1:2], keepdims=True))
           - jnp.exp(jnp.sum(lf[2:3] * lf[3:4], keepdims=True)) + lambda_init)
    lane_lo = lax.broadcasted_iota(jnp.int32, (tq, 2 * HEAD_DIM), 1) < HEAD_DIM
    causal = (lax.broadcasted_iota(jnp.int32, (tq, tq), 0)
              <= lax.broadcasted_iota(jnp.int32, (tq, tq), 1))

    def scores(kb, qm):
        return lax.dot_general(kb, qm, (((1,), (1,)), ((), ())), preferred_element_type=F32)

    def online(s, m, l, acc, vbt):
        m_new = jnp.maximum(m, jnp.max(s, axis=0, keepdims=True))
        p = jnp.exp2(s - m_new)
        alpha = jnp.exp2(m - m_new)
        l = alpha * l + jnp.sum(p, axis=0, keepdims=True)
        acc = alpha * acc + jnp.dot(vbt, p.astype(BF16), preferred_element_type=F32)
        return m_new, l, acc

    for i in range(q_ref.shape[0] // tq):
        q = q_ref[i * tq:(i + 1) * tq, :]
        zero = jnp.zeros_like(q)
        q1 = jnp.where(lane_lo, q, zero)
        q2 = jnp.where(lane_lo, zero, q)
        m1 = m2 = jnp.full((1, tq), NEG_BIG, F32)
        l1 = l2 = jnp.zeros((1, tq), F32)
        a1 = a2 = jnp.zeros((2 * HEAD_DIM, tq), F32)
        for j in range(i + 1):
            kb = k_ref[j * tq:(j + 1) * tq, :]
            vbt = v_ref[j * tq:(j + 1) * tq, :].T
            s1 = scores(kb, q1)
            s2 = scores(kb, q2)
            if j == i:
                s1 = jnp.where(causal, s1, NEG_BIG)
                s2 = jnp.where(causal, s2, NEG_BIG)
            m1, l1, a1 = online(s1, m1, l1, a1, vbt)
            m2, l2, a2 = online(s2, m2, l2, a2, vbt)
        o = (a1 / l1 - lam * (a2 / l2)).T
        o = o * lax.rsqrt(jnp.mean(o * o, axis=-1, keepdims=True) + NORM_EPS)
        o_ref[i * tq:(i + 1) * tq, :] = (o * subln_ref[...] * (1.0 - lambda_init)).astype(BF16)


def _diff_attention(q, k, v, lam_vecs, subln, lambda_init, *, n_b, seq):
    t = q.shape[0]
    width = 2 * HEAD_DIM
    head_seq = pl.BlockSpec((seq, width), lambda b, h: (b, h))
    return pl.pallas_call(
        functools.partial(_diff_attn_kernel, tq=512, lambda_init=lambda_init),
        grid=(n_b, A_HEADS),
        in_specs=[
            pl.BlockSpec((4, HEAD_DIM), lambda b, h: (0, 0)),
            pl.BlockSpec((1, width), lambda b, h: (0, 0)),
            head_seq, head_seq, head_seq,
        ],
        out_specs=head_seq,
        out_shape=jax.ShapeDtypeStruct((t, D_MODEL), BF16),
        compiler_params=_params(("parallel", "parallel")),
        name="diff_attention",
    )(lam_vecs, subln.reshape(1, width), q, k, v)


def _window_unit(q, kc, vc, bias, lane_lo):
    zero = jnp.zeros_like(q)
    qq = jnp.concatenate([jnp.where(lane_lo, q, zero), jnp.where(lane_lo, zero, q)], axis=0)
    s = lax.dot_general(qq, kc, (((1,), (1,)), ((), ())), preferred_element_type=F32) + bias
    m = jnp.max(s, axis=-1, keepdims=True)
    p = jnp.exp2(s - m).astype(BF16)
    v_aug = jnp.concatenate([vc, jnp.ones(vc.shape, BF16)], axis=1)
    o_aug = jnp.dot(p, v_aug, preferred_element_type=F32)
    n = WINDOW_STEPS
    o = jnp.where(lane_lo, o_aug[:n, :LANES], o_aug[n:, :LANES])
    l = jnp.where(lane_lo, o_aug[:n, LANES:], o_aug[n:, LANES:])
    m_sel = jnp.where(lane_lo, m[:n], m[n:])
    return o / l, m_sel + jnp.log2(l)


def _dil_attn_kernel(q0_ref, q1_ref, q2_ref, k0_ref, v0_ref, k1_ref, v1_ref, k2_ref, v2_ref,
                     o_ref, acc_ref, lse_ref, *, seq):
    n = WINDOW_STEPS
    lane_lo = lax.broadcasted_iota(jnp.int32, (n, LANES), 1) < HEAD_DIM
    qi = lax.broadcasted_iota(jnp.int32, (2 * n, 2 * n), 0) % n
    kj = lax.broadcasted_iota(jnp.int32, (2 * n, 2 * n), 1)
    band = jnp.where((kj >= qi) & (kj <= qi + n), 0.0, NEG_BIG)
    first = jnp.where(lax.broadcasted_iota(jnp.int32, (2 * n, n), 1)
                      <= lax.broadcasted_iota(jnp.int32, (2 * n, n), 0) % n, 0.0, NEG_BIG)

    def unit(g, q_rows, k_rows, v_rows, bias, dst):
        o, lse = _window_unit(q_rows, k_rows, v_rows, bias, lane_lo)
        acc_ref[g, dst, :] = o
        lse_ref[g, dst, :] = lse

    def rows(ref, r, lo, hi):
        return ref[lo:hi, :] if len(ref.shape) == 2 else ref[0, r, lo:hi, :]

    for g, (q_ref, k_ref, v_ref) in enumerate(((q0_ref, k0_ref, v0_ref), (q1_ref, k1_ref, v1_ref),
                                               (q2_ref, k2_ref, v2_ref))):
        dil = B_DILATIONS[g]
        for r in range(dil):
            for blk in range(seq // dil // n):
                dst = pl.ds(r + blk * n * dil, n, stride=dil) if dil > 1 else pl.ds(blk * n, n)
                k_lo = max(blk - 1, 0) * n
                unit(g, rows(q_ref, r, blk * n, (blk + 1) * n), rows(k_ref, r, k_lo, (blk + 1) * n),
                     rows(v_ref, r, k_lo, (blk + 1) * n), first if blk == 0 else band, dst)

    l0, l1, l2 = lse_ref[0], lse_ref[1], lse_ref[2]
    top = jnp.maximum(jnp.maximum(l0, l1), l2)
    w0, w1, w2 = jnp.exp2(l0 - top), jnp.exp2(l1 - top), jnp.exp2(l2 - top)
    o = (w0 * acc_ref[0] + w1 * acc_ref[1] + w2 * acc_ref[2]) / (w0 + w1 + w2)
    o_ref[...] = o.astype(BF16)


def _dilated_attention(q0, q1, q2, k0, v0, k1, v1, k2, v2, *, n_b, seq):
    t = q0.shape[0]
    pairs = D_MODEL // LANES
    nat = pl.BlockSpec((seq, LANES), lambda b, p: (b, p))

    def res(dil):
        return pl.BlockSpec((1, dil, seq // dil, LANES), lambda b, p: (b, 0, 0, p))

    d1, d2 = B_DILATIONS[1], B_DILATIONS[2]
    return pl.pallas_call(
        functools.partial(_dil_attn_kernel, seq=seq),
        grid=(n_b, pairs),
        in_specs=[nat, res(d1), res(d2), nat, nat, res(d1), res(d1), res(d2), res(d2)],
        out_specs=nat,
        out_shape=jax.ShapeDtypeStruct((t, D_MODEL), BF16),
        scratch_shapes=[pltpu.VMEM((3, seq, LANES), F32), pltpu.VMEM((3, seq, LANES), F32)],
        compiler_params=_params(("parallel", "parallel")),
        name="dilated_attention",
    )(q0, q1, q2, k0, v0, k1, v1, k2, v2)


def _mix_out_kernel(o_ref, wo_ref, x_ref, gate_ref, gain_ref, shift_ref, scale_ref, rw_ref, rb_ref,
                    xn_ref, h_ref, idx_ref, wgt_ref, rank_ref, cnt_ref, *, tm):
    y = jnp.dot(o_ref[...], wo_ref[...], preferred_element_type=F32)
    xn = x_ref[...] + gate_ref[0] * y
    xn_ref[...] = xn
    h = _modulated_norm(xn, gain_ref[...], shift_ref[0], scale_ref[0])

    h_hi = h.astype(BF16)
    h_ref[...] = h_hi
    h_lo = (h - h_hi.astype(F32)).astype(BF16)
    rw = rw_ref[...]
    rw_hi = rw.astype(BF16)
    rw_lo = (rw - rw_hi.astype(F32)).astype(BF16)
    nt = (((1,), (1,)), ((), ()))
    logits = (lax.dot_general(rw_hi, h_hi, nt, preferred_element_type=F32)
              + lax.dot_general(rw_lo, h_hi, nt, preferred_element_type=F32)
              + lax.dot_general(rw_hi, h_lo, nt, preferred_element_type=F32)) + rb_ref[...]

    e_iota = lax.broadcasted_iota(jnp.int32, logits.shape, 0)
    work = logits
    sels, tops, idxs = [], [], []
    for _ in range(TOP_K):
        mk = jnp.max(work, axis=0, keepdims=True)
        ik = jnp.min(jnp.where(work == mk, e_iota, N_EXPERTS), axis=0, keepdims=True)
        sel = e_iota == ik
        work = jnp.where(sel, -jnp.inf, work)
        sels.append(sel)
        tops.append(mk)
        idxs.append(ik)
    exps = [jnp.exp(m - tops[0]) for m in tops]
    denom = exps[0] + exps[1] + exps[2] + exps[3]
    for k in range(TOP_K):
        idx_ref[k:k + 1, :] = idxs[k]
        wgt_ref[k:k + 1, :] = exps[k] / denom

    chosen = jnp.zeros(logits.shape, F32)
    for sel in sels:
        chosen = chosen + jnp.where(sel, 1.0, 0.0)
    before = (lax.broadcasted_iota(jnp.int32, (tm, tm), 0)
              < lax.broadcasted_iota(jnp.int32, (tm, tm), 1))
    upper = jnp.where(before, 1.0, 0.0).astype(BF16)
    prefix = jnp.dot(chosen.astype(BF16), upper, preferred_element_type=F32)
    for k in range(TOP_K):
        rank_ref[k:k + 1, :] = jnp.sum(jnp.where(sels[k], prefix, 0.0), axis=0,
                                       keepdims=True).astype(jnp.int32)
    counts = jnp.sum(chosen, axis=1, keepdims=True).astype(jnp.int32)
    cnt_ref[0] = jnp.broadcast_to(counts, cnt_ref.shape[1:])


def _mix_out_and_route(o, wo, layer, x, gate, gain, shift, scale, router_w, router_b, *, n_b, seq):
    t, d = x.shape
    tm = ROUTE_TILE
    n_s = seq // tm
    per_batch = pl.BlockSpec((1, 1, d), lambda i: (i // n_s, 0, 0))
    row = pl.BlockSpec((tm, d), lambda i: (i, 0))
    sel = pl.BlockSpec((TOP_K, tm), lambda i: (0, i))
    return pl.pallas_call(
        functools.partial(_mix_out_kernel, tm=tm),
        grid=(t // tm,),
        in_specs=[
            row,
            pl.BlockSpec((None,) + wo.shape[1:], lambda i: (layer, 0, 0)),
            row, per_batch,
            pl.BlockSpec((1, d), lambda i: (0, 0)),
            per_batch, per_batch,
            pl.BlockSpec((N_EXPERTS, d), lambda i: (0, 0)),
            pl.BlockSpec((N_EXPERTS, 1), lambda i: (0, 0)),
        ],
        out_specs=[row, row, sel, sel, sel,
                   pl.BlockSpec((1, N_EXPERTS, LANES), lambda i: (i, 0, 0))],
        out_shape=[
            jax.ShapeDtypeStruct((t, d), F32),
            jax.ShapeDtypeStruct((t, d), BF16),
            jax.ShapeDtypeStruct((TOP_K, t), jnp.int32),
            jax.ShapeDtypeStruct((TOP_K, t), F32),
            jax.ShapeDtypeStruct((TOP_K, t), jnp.int32),
            jax.ShapeDtypeStruct((t // tm, N_EXPERTS, LANES), jnp.int32),
        ],
        compiler_params=_params(("parallel",)),
        name="mix_out_route",
    )(o, wo, x, gate, gain.reshape(1, d), shift, scale, router_w.T, router_b.reshape(N_EXPERTS, 1))


HIGH_HALF = 0xFFFF0000


def _pack_bf16_pairs(x):
    c = x.shape[1] // 2
    lo = lax.bitcast_convert_type(x[:, :c], jnp.uint32)
    hi = lax.bitcast_convert_type(x[:, c:], jnp.uint32)
    return (lo >> 16) | (hi & jnp.uint32(HIGH_HALF))


def _unpack_bf16_pairs(w):
    lo = lax.bitcast_convert_type(w << 16, F32).astype(BF16)
    hi = lax.bitcast_convert_type(w & jnp.uint32(HIGH_HALF), F32).astype(BF16)
    return lo, hi


def _copy_tables(grp, toff, tstart):
    experts = jnp.arange(N_EXPERTS, dtype=jnp.int32)

    def words(count, first_row, rows, width):
        end = jnp.cumsum(count, axis=1)
        j = jnp.arange(width, dtype=jnp.int32)
        e_of = jnp.sum((end[:, None, :] <= j[None, :, None]).astype(jnp.int32), axis=2)
        onehot = e_of[:, :, None] == experts[None, None, :]

        def pick(a):
            return jnp.sum(jnp.where(onehot, a[:, None, :], 0), axis=2)

        row = pick(first_row) + (j[None, :] - pick(end - count)) * rows
        word = ((pick(tstart) + row) << ROW_BITS) | (pick(toff) + row)
        return word.reshape(-1).astype(jnp.int32), end[:, -1]

    n_big = grp // BIG_COPY
    big, total_big = words(n_big, jnp.zeros_like(grp), BIG_COPY, MAX_BIG)
    small, total_small = words((grp - n_big * BIG_COPY) // GROUP_ALIGN, n_big * BIG_COPY,
                               GROUP_ALIGN, MAX_SMALL)
    counts = jnp.stack([total_big, total_small, jnp.sum(grp, axis=1)], axis=1)
    return counts.reshape(-1).astype(jnp.int32), big, small


def _tile_copies(tile, cnt_ref, big_ref, small_ref, make_copy, wait):
    if wait:
        total = cnt_ref[3 * tile + 2]

        def wait_rows(rows):
            def body(j, carry):
                make_copy(0, 0, rows).wait()
                return carry
            return body

        lax.fori_loop(0, total // BIG_COPY, wait_rows(BIG_COPY), 0)
        lax.fori_loop(0, (total % BIG_COPY) // GROUP_ALIGN, wait_rows(GROUP_ALIGN), 0)
        return

    def start_rows(table_ref, width, rows):
        def body(j, carry):
            word = table_ref[tile * width + j]
            make_copy(pl.multiple_of(word & ((1 << ROW_BITS) - 1), GROUP_ALIGN),
                      pl.multiple_of(word >> ROW_BITS, GROUP_ALIGN), rows).start()
            return carry
        return body

    lax.fori_loop(0, cnt_ref[3 * tile], start_rows(big_ref, MAX_BIG, BIG_COPY), 0)
    lax.fori_loop(0, cnt_ref[3 * tile + 1], start_rows(small_ref, MAX_SMALL, GROUP_ALIGN), 0)


def _dispatch_kernel(cnt_ref, big_ref, small_ref, fill_row_ref, fill_on_ref, nvalid_ref,
                     h_ref, idx_ref, rank_ref, wgt_ref, off_ref, xs_ref, loc_ref, z_ref, zero_ref,
                     sems, *, n_blocks):
    i = pl.program_id(0)
    last = pl.num_programs(0) - 1
    slot = i % 2
    tm = h_ref.shape[0]

    def copies(tile, buf, wait):
        def make_copy(local, glob, n):
            return pltpu.make_async_copy(z_ref.at[buf, pl.ds(local, n)],
                                         xs_ref.at[pl.ds(glob, n)], sems.at[buf])

        _tile_copies(tile, cnt_ref, big_ref, small_ref, make_copy, wait)

    @pl.when(i == 0)
    def _():
        zero_ref[...] = jnp.zeros_like(zero_ref)

        def fill(row):
            return pltpu.make_async_copy(
                zero_ref, xs_ref.at[pl.ds(pl.multiple_of(row, EXPERT_ROWS), EXPERT_ROWS)],
                sems.at[2])

        def fills(act):
            def last_block(e, carry):
                @pl.when(fill_on_ref[e] == 1)
                def _():
                    act(fill(fill_row_ref[e]))
                return carry

            lax.fori_loop(0, N_EXPERTS, last_block, 0)

            def tail_block(b, carry):
                act(fill(b * EXPERT_ROWS))
                return carry

            lax.fori_loop(nvalid_ref[0], n_blocks, tail_block, 0)

        fills(lambda cp: cp.start())
        fills(lambda cp: cp.wait())

    e_iota = lax.broadcasted_iota(jnp.int32, (N_EXPERTS, tm), 0)
    off = off_ref[0][:, 0:1]
    locs = []
    for k in range(TOP_K):
        sel = e_iota == idx_ref[k:k + 1, :]
        loc = jnp.sum(jnp.where(sel, off, 0), axis=0, keepdims=True) + rank_ref[k:k + 1, :]
        loc_ref[k:k + 1, :] = loc
        locs.append(loc)

    @pl.when(i >= 2)
    def _():
        copies(i - 2, slot, wait=True)

    h = h_ref[...]
    half = h.shape[1] // 2
    rows = 256
    for c in range(COMPACT_ROWS // rows):
        r_iota = lax.broadcasted_iota(jnp.int32, (rows, tm), 0) + c * rows
        hit = jnp.zeros((rows, tm), F32)
        gate = jnp.zeros((rows, tm), F32)
        for k, loc in enumerate(locs):
            here = r_iota == loc
            hit = jnp.where(here, 1.0, hit)
            gate = jnp.where(here, wgt_ref[k:k + 1, :], gate)
        z_ref[slot, c * rows:(c + 1) * rows, :half] = _pack_bf16_pairs(
            jnp.dot(hit.astype(BF16), h, preferred_element_type=F32))
        row_gate = jnp.sum(gate, axis=1, keepdims=True)
        z_ref[slot, c * rows:(c + 1) * rows, half:] = lax.bitcast_convert_type(
            jnp.broadcast_to(row_gate, (rows, LANES)), jnp.uint32)

    copies(i, slot, wait=False)

    @pl.when(i == last)
    def _():
        @pl.when(i >= 1)
        def _():
            copies(i - 1, 1 - slot, wait=True)

        copies(i, slot, wait=True)


def _dispatch(h, idx, rank, wgt, off_b, tables, fills, n_rows):
    t, d = h.shape
    tm = ROUTE_TILE
    width = d // 2 + LANES
    sel = pl.BlockSpec((TOP_K, tm), lambda i, *_: (0, i))
    grid_spec = pltpu.PrefetchScalarGridSpec(
        num_scalar_prefetch=6,
        grid=(t // tm,),
        in_specs=[
            pl.BlockSpec((tm, d), lambda i, *_: (i, 0)),
            sel, sel, sel,
            pl.BlockSpec((1, N_EXPERTS, LANES), lambda i, *_: (i, 0, 0)),
        ],
        out_specs=[pl.BlockSpec(memory_space=pl.ANY), sel],
        scratch_shapes=[pltpu.VMEM((2, COMPACT_ROWS, width), jnp.uint32),
                        pltpu.VMEM((EXPERT_ROWS, width), jnp.uint32),
                        pltpu.SemaphoreType.DMA((3,))],
    )
    return pl.pallas_call(
        functools.partial(_dispatch_kernel, n_blocks=n_rows // EXPERT_ROWS),
        grid_spec=grid_spec,
        out_shape=[jax.ShapeDtypeStruct((n_rows, width), jnp.uint32),
                   jax.ShapeDtypeStruct((TOP_K, t), jnp.int32)],
        compiler_params=_params(("arbitrary",)),
        name="moe_dispatch",
    )(*tables, *fills, h, idx, rank, wgt, off_b)


def _expert_kernel(be_ref, first_ref, nvalid_ref, xs_ref, wgu_ref, bgu_ref, wd_ref, bd_ref,
                   ys_ref, wgu_bf, wd_bf):
    b = pl.program_id(0)

    @pl.when(first_ref[b] == 1)
    def _():
        wgu_bf[...] = wgu_ref[0].astype(BF16)
        wd_bf[...] = wd_ref[0].astype(BF16)

    @pl.when(b < nvalid_ref[0])
    def _():
        d_ff = wd_bf.shape[0]
        half = wd_bf.shape[1] // 2
        x = jnp.concatenate(_unpack_bf16_pairs(xs_ref[:, :half]), axis=1)
        row_gate = lax.bitcast_convert_type(xs_ref[:, half:], F32)
        gu = jnp.dot(x, wgu_bf[...], preferred_element_type=F32) + bgu_ref[0]
        gate = jnp.minimum(gu[:, :d_ff], SWIGLU_LIMIT)
        up = jnp.clip(gu[:, d_ff:], -SWIGLU_LIMIT, SWIGLU_LIMIT)
        glu = gate * jax.nn.sigmoid(SWIGLU_ALPHA * gate)
        act = ((up + 1.0) * glu).astype(BF16)
        y = jnp.dot(act, wd_bf[...], preferred_element_type=F32) + bd_ref[0]
        y = y * jnp.concatenate([row_gate] * (y.shape[1] // LANES), axis=1)
        ys_ref[...] = _pack_bf16_pairs(y.astype(BF16).astype(F32))

    @pl.when(b >= nvalid_ref[0])
    def _():
        ys_ref[...] = jnp.zeros_like(ys_ref)


def _experts(xs, block_expert, block_first, n_valid, layer, w_gu, b_gu, w_d, b_d):
    n_rows, in_width = xs.shape
    bm = EXPERT_ROWS
    d, d_ff = w_d.shape[3], w_d.shape[2]
    half = d // 2
    n_l = w_gu.shape[0]
    grid_spec = pltpu.PrefetchScalarGridSpec(
        num_scalar_prefetch=3,
        grid=(n_rows // bm,),
        in_specs=[
            pl.BlockSpec((bm, in_width), lambda b, be, fi, nv: (jnp.minimum(b, nv[0] - 1), 0)),
            pl.BlockSpec((None, 1, d, 2 * d_ff), lambda b, be, fi, nv: (layer, be[b], 0, 0)),
            pl.BlockSpec((None, 1, 1, 2 * d_ff), lambda b, be, fi, nv: (layer, be[b], 0, 0)),
            pl.BlockSpec((None, 1, d_ff, d), lambda b, be, fi, nv: (layer, be[b], 0, 0)),
            pl.BlockSpec((None, 1, 1, d), lambda b, be, fi, nv: (layer, be[b], 0, 0)),
        ],
        out_specs=pl.BlockSpec((bm, half), lambda b, be, fi, nv: (b, 0)),
        scratch_shapes=[pltpu.VMEM((d, 2 * d_ff), BF16), pltpu.VMEM((d_ff, d), BF16)],
    )
    return pl.pallas_call(
        _expert_kernel,
        grid_spec=grid_spec,
        out_shape=jax.ShapeDtypeStruct((n_rows, half), jnp.uint32),
        compiler_params=_params(("arbitrary",)),
        name="moe_experts",
    )(block_expert, block_first, n_valid, xs, w_gu, b_gu.reshape(n_l, N_EXPERTS, 1, 2 * d_ff),
      w_d, b_d.reshape(n_l, N_EXPERTS, 1, d))


def _combine_kernel(cnt_ref, big_ref, small_ref, loc_ref, ys_ref, x_ref, gate_ref,
                    *rest, final):
    fin_refs, (o_ref, y_ref, sems) = rest[:-3], rest[-3:]
    i = pl.program_id(0)
    slot = i % 2
    tm = x_ref.shape[0]

    def copies(tile, buf, wait):
        def make_copy(local, glob, n):
            return pltpu.make_async_copy(ys_ref.at[pl.ds(glob, n)],
                                         y_ref.at[buf, pl.ds(local, n)], sems.at[buf])

        _tile_copies(tile, cnt_ref, big_ref, small_ref, make_copy, wait)

    @pl.when(i == 0)
    def _():
        y_ref[...] = jnp.zeros_like(y_ref)
        copies(0, 0, wait=False)

    copies(i, slot, wait=True)

    @pl.when(i + 1 < pl.num_programs(0))
    def _():
        copies(i + 1, 1 - slot, wait=False)

    rows = COMBINE_CHUNK
    half = y_ref.shape[2]
    f_lo = jnp.zeros((tm, half), F32)
    f_hi = jnp.zeros((tm, half), F32)
    for c in range(COMPACT_ROWS // rows):
        c_iota = lax.broadcasted_iota(jnp.int32, (tm, rows), 1) + c * rows
        q = jnp.zeros((tm, rows), F32)
        for k in range(TOP_K):
            q = jnp.where(c_iota == loc_ref[:, k:k + 1], 1.0, q)
        q = q.astype(BF16)
        y_lo, y_hi = _unpack_bf16_pairs(y_ref[slot, c * rows:(c + 1) * rows, :])
        f_lo = f_lo + jnp.dot(q, y_lo, preferred_element_type=F32)
        f_hi = f_hi + jnp.dot(q, y_hi, preferred_element_type=F32)
    out = x_ref[...] + gate_ref[0] * jnp.concatenate([f_lo, f_hi], axis=1)
    if final:
        gain_ref, shift_ref, scale_ref = fin_refs
        out = _modulated_norm(out, gain_ref[...], shift_ref[0], scale_ref[0])
    o_ref[...] = out


def _combine(loc_tk, ys, x, gate, tables, final_mod, *, seq):
    t, d = x.shape
    tm = ROUTE_TILE
    n_s = seq // tm
    sel = pl.BlockSpec((tm, TOP_K), lambda i, *_: (i, 0))
    per_batch = pl.BlockSpec((1, 1, d), lambda i, *_: (i // n_s, 0, 0))
    in_specs = [sel, pl.BlockSpec(memory_space=pl.ANY),
                pl.BlockSpec((tm, d), lambda i, *_: (i, 0)), per_batch]
    extra = ()
    if final_mod is not None:
        gain, shift, scale = final_mod
        extra = (gain.reshape(1, d), shift, scale)
        in_specs += [pl.BlockSpec((1, d), lambda i, *_: (0, 0)), per_batch, per_batch]
    grid_spec = pltpu.PrefetchScalarGridSpec(
        num_scalar_prefetch=3,
        grid=(t // tm,),
        in_specs=in_specs,
        out_specs=pl.BlockSpec((tm, d), lambda i, *_: (i, 0)),
        scratch_shapes=[pltpu.VMEM((2, COMPACT_ROWS, d // 2), jnp.uint32),
                        pltpu.SemaphoreType.DMA((2,))],
    )
    return pl.pallas_call(
        functools.partial(_combine_kernel, final=final_mod is not None),
        grid_spec=grid_spec,
        out_shape=jax.ShapeDtypeStruct((t, d), F32),
        compiler_params=_params(("arbitrary",)),
        name="moe_combine",
    )(*tables, loc_tk, ys, x, gate, *extra)


def _moe(h, idx, wgt, rank, tile_cnt, x, gate, layer, w_gu, b_gu, w_d, b_d, final_mod, *, seq):
    t = h.shape[0]
    bm = EXPERT_ROWS
    n_tiles = t // ROUTE_TILE
    n_rows = t * TOP_K + n_tiles * N_EXPERTS * GROUP_ALIGN + N_EXPERTS * bm
    n_blocks = n_rows // bm
    cnt = tile_cnt[:, :, 0]
    grp = (cnt + GROUP_ALIGN - 1) // GROUP_ALIGN * GROUP_ALIGN
    padded = (jnp.sum(grp, axis=0) + bm - 1) // bm * bm
    pad_end = jnp.cumsum(padded)
    tstart = (pad_end - padded)[None, :] + jnp.cumsum(grp, axis=0) - grp
    toff = jnp.cumsum(grp, axis=1) - grp
    tables = _copy_tables(grp, toff, tstart)
    off_b = jnp.broadcast_to(toff[:, :, None], (n_tiles, N_EXPERTS, LANES)).astype(jnp.int32)
    block_row = jnp.arange(n_blocks, dtype=jnp.int32) * bm
    block_expert = jnp.minimum(
        jnp.sum((pad_end[None, :] <= block_row[:, None]).astype(jnp.int32), axis=1),
        N_EXPERTS - 1).astype(jnp.int32)
    block_first = jnp.concatenate(
        [jnp.ones((1,), jnp.int32), (block_expert[1:] != block_expert[:-1]).astype(jnp.int32)])
    n_valid = (pad_end[-1:] // bm).astype(jnp.int32)
    fills = (jnp.maximum(pad_end - bm, 0).astype(jnp.int32), (padded > 0).astype(jnp.int32),
             n_valid)
    xs, loc = _dispatch(h, idx, rank, wgt, off_b, tables, fills, n_rows)
    ys = _experts(xs, block_expert, block_first, n_valid, layer, w_gu, b_gu, w_d, b_d)
    return _combine(loc.T, ys, x, gate, tables, final_mod, seq=seq)


def kernel(x, c, mod_w, mod_b, mix_norm, ffn_norm, a_wqkv, a_wo, a_lambda, a_subln, kv_norm, kv_mod_w, kv_mod_b, kv_w, b_wq, b_wo, router_w, router_b, exp_w_gate_up, exp_b_gate_up, exp_w_down, exp_b_down, final_norm, final_mod_w, final_mod_b):
    n_b, seq, d = x.shape
    t = n_b * seq
    sizes = dict(n_b=n_b, seq=seq)
    rope_tabs = _rope_tables(seq)
    q_scale = HEAD_DIM ** -0.5 * math.log2(math.e)

    def per_batch(v):
        return v.reshape(n_b, 1, d)

    mod = _adaln_vectors(c, mod_w, mod_b)
    kv_mod = _adaln_vectors(c, kv_mod_w[None], kv_mod_b[None])[0]
    fin_mod = _adaln_vectors(c, final_mod_w[None], final_mod_b[None])[0]

    a_wqkv, a_wo, b_wq, b_wo = (w.astype(BF16) for w in (a_wqkv, a_wo, b_wq, b_wo))
    kv_w = kv_w[None].astype(BF16)

    xt = x.reshape(t, d)
    shared = None
    for layer in range(DEPTH):
        sh1, sc1, g1, sh2, sc2, g2 = (per_batch(mod[layer, :, j * d:(j + 1) * d]) for j in range(6))
        if layer < N_A_LAYERS:
            lambda_init = 0.8 - 0.6 * math.exp(-0.3 * layer)
            segs = ((0, True, q_scale, (1,)), (1, True, 1.0, (1,)), (2, False, 1.0, (1,)))
            q, k, v = _norm_project(xt, mix_norm[layer], sh1, sc1, a_wqkv, layer,
                                    rope_tabs, segs, name="a_qkv_proj", **sizes)
            o = _diff_attention(q, k, v, a_lambda[layer], a_subln[layer], lambda_init, **sizes)
            wo, wo_layer = a_wo, layer
        else:
            j = layer - N_A_LAYERS
            segs = tuple((g, True, q_scale, (B_DILATIONS[g],)) for g in range(3))
            q0, q1, q2 = _norm_project(xt, mix_norm[layer], sh1, sc1, b_wq, j,
                                       rope_tabs, segs, name="b_q_proj", **sizes)
            o = _dilated_attention(q0, q1, q2, *shared, **sizes)
            wo, wo_layer = b_wo, j
        xt, h, idx, wgt, rank, tile_cnt = _mix_out_and_route(
            o, wo, wo_layer, xt, g1, ffn_norm[layer], sh2, sc2, router_w[layer],
            router_b[layer], **sizes)
        final_mod = None
        if layer == DEPTH - 1:
            final_mod = (final_norm, per_batch(fin_mod[:, :d]), per_batch(fin_mod[:, d:]))
        xt = _moe(h, idx, wgt, rank, tile_cnt, xt, g2, layer, exp_w_gate_up, exp_b_gate_up,
                  exp_w_down, exp_b_down, final_mod, seq=seq)
        if layer == N_A_LAYERS - 1:
            segs = ((0, True, 1.0, B_DILATIONS), (1, False, 1.0, B_DILATIONS))
            k0, k1, k2, v0, v1, v2 = _norm_project(
                xt, kv_norm, per_batch(kv_mod[:, :d]), per_batch(kv_mod[:, d:]),
                kv_w, 0, rope_tabs, segs, name="shared_kv_proj", **sizes)
            shared = (k0, v0, k1, v1, k2, v2)
    return xt.reshape(n_b, seq, d)
```

```python
import functools
import math

import jax
import jax.numpy as jnp
from jax import lax
from jax.experimental import pallas as pl
from jax.experimental.pallas import tpu as pltpu

F32 = jnp.float32
BF16 = jnp.bfloat16

D_MODEL = 1024
HEAD_DIM = 64
A_HEADS = 8
B_KV_HEADS = 16
B_DILATIONS = (1, 4, 16)
WINDOW_STEPS = 128
ROPE_THETA = 500000.0
ROPE_DIM = HEAD_DIM // 4
N_EXPERTS = 32
TOP_K = 4
SWIGLU_LIMIT = 7.0
SWIGLU_ALPHA = 1.702
NORM_EPS = 1e-5
N_A_LAYERS = 2
DEPTH = 4

LANES = 128
EXPERT_ROWS = 1024
EXPERT_PART = 256
ROUTE_TILE = 512
GROUP_ALIGN = 8
COMPACT_ROWS = ROUTE_TILE * TOP_K + N_EXPERTS * GROUP_ALIGN
COMBINE_CHUNK = COMPACT_ROWS // 2
BIG_COPY = 32
MAX_BIG = COMPACT_ROWS // BIG_COPY
MAX_SMALL = N_EXPERTS * (BIG_COPY // GROUP_ALIGN - 1)
ROW_BITS = 12
NEG_BIG = -1e30
VMEM_LIMIT = 56 * 1024 * 1024


def _params(sem, vmem=VMEM_LIMIT):
    return pltpu.CompilerParams(dimension_semantics=sem, vmem_limit_bytes=vmem)


def _adaln_kernel(c_ref, w_ref, b_ref, o_ref):
    c = c_ref[...]
    c_act = (c * jax.nn.sigmoid(c)).astype(BF16)
    o_ref[0] = jnp.dot(c_act, w_ref[0].astype(BF16), preferred_element_type=F32) + b_ref[0]


def _adaln_vectors(c, w, b):
    n_l, d, n = w.shape
    n_b = c.shape[0]
    tn = 1024
    return pl.pallas_call(
        _adaln_kernel,
        grid=(n_l, n // tn),
        in_specs=[
            pl.BlockSpec((n_b, d), lambda l, j: (0, 0)),
            pl.BlockSpec((1, d, tn), lambda l, j: (l, 0, j)),
            pl.BlockSpec((1, 1, tn), lambda l, j: (l, 0, j)),
        ],
        out_specs=pl.BlockSpec((1, n_b, tn), lambda l, j: (l, 0, j)),
        out_shape=jax.ShapeDtypeStruct((n_l, n_b, n), F32),
        compiler_params=_params(("parallel", "parallel")),
        name="adaln_vectors",
    )(c, w, b.reshape(n_l, 1, n))


def _modulated_norm(x, gain, shift, scale):
    y = x * lax.rsqrt(jnp.mean(x * x, axis=-1, keepdims=True) + NORM_EPS)
    return y * gain * (1.0 + scale) + shift


def _rope_tables(seq):
    inv = ROPE_THETA ** (-jnp.arange(0, ROPE_DIM, 2, dtype=F32) / ROPE_DIM)
    ang = jnp.arange(seq, dtype=F32)[:, None] * inv[None, :]
    cos, sin = jnp.cos(ang), jnp.sin(ang)
    half = ROPE_DIM // 2
    rest = HEAD_DIM - ROPE_DIM
    zeros = jnp.zeros((seq, half), F32)
    c_tab = jnp.concatenate([cos, cos, jnp.ones((seq, rest), F32)], axis=1)
    s1_tab = jnp.concatenate([zeros, sin, jnp.zeros((seq, rest), F32)], axis=1)
    s2_tab = jnp.concatenate([-sin, zeros, jnp.zeros((seq, rest), F32)], axis=1)
    rep = LANES // HEAD_DIM
    return tuple(jnp.tile(t, (1, rep)) for t in (c_tab, s1_tab, s2_tab))


def _apply_rope(y, c_tab, s1_tab, s2_tab):
    parts = []
    for j in range(y.shape[1] // LANES):
        yj = y[:, j * LANES:(j + 1) * LANES]
        parts.append(yj * c_tab + pltpu.roll(yj, ROPE_DIM // 2, 1) * s1_tab
                     + pltpu.roll(yj, LANES - ROPE_DIM // 2, 1) * s2_tab)
    return jnp.concatenate(parts, axis=1)


def _proj_kernel(x_ref, gain_ref, shift_ref, scale_ref, w_ref, rc_ref, rs1_ref, rs2_ref, *rest,
                 segs, tm):
    n_out = sum(len(s[3]) for s in segs)
    out_refs, scr_ref = rest[:n_out], rest[n_out]
    h = _modulated_norm(x_ref[...], gain_ref[...], shift_ref[0], scale_ref[0]).astype(BF16)
    oi = 0
    for chunk, rope, mult, dils in segs:
        y = jnp.dot(h, w_ref[:, chunk * D_MODEL:(chunk + 1) * D_MODEL], preferred_element_type=F32)
        if rope:
            y = _apply_rope(y, rc_ref[...], rs1_ref[...], rs2_ref[...])
        if mult != 1.0:
            y = y * mult
        for dil in dils:
            o_ref = out_refs[oi]
            oi += 1
            if dil == 1:
                o_ref[...] = y.astype(BF16)
            else:
                for j in range(D_MODEL // LANES):
                    scr_ref[j] = y[:, j * LANES:(j + 1) * LANES]
                for r in range(dil):
                    for j in range(D_MODEL // LANES):
                        o_ref[0, r, :, j * LANES:(j + 1) * LANES] = (
                            scr_ref[j, pl.ds(r, tm // dil, stride=dil), :].astype(BF16))


def _norm_project(x, gain, shift, scale, w, layer, rope_tabs, segs, *, n_b, seq, name):
    t, d = x.shape
    tm = 512
    n_s = seq // tm
    out_shapes, out_specs = [], []
    for _, _, _, dils in segs:
        for dil in dils:
            if dil == 1:
                out_shapes.append(jax.ShapeDtypeStruct((t, D_MODEL), BF16))
                out_specs.append(pl.BlockSpec((tm, D_MODEL), lambda i: (i, 0)))
            else:
                out_shapes.append(jax.ShapeDtypeStruct((n_b, dil, seq // dil, D_MODEL), BF16))
                out_specs.append(pl.BlockSpec((1, dil, tm // dil, D_MODEL),
                                              lambda i: (i // n_s, 0, i % n_s, 0)))
    per_batch = pl.BlockSpec((1, 1, d), lambda i: (i // n_s, 0, 0))
    rope_spec = pl.BlockSpec((tm, LANES), lambda i: (i % n_s, 0))
    return pl.pallas_call(
        functools.partial(_proj_kernel, segs=segs, tm=tm),
        grid=(t // tm,),
        in_specs=[
            pl.BlockSpec((tm, d), lambda i: (i, 0)),
            pl.BlockSpec((1, d), lambda i: (0, 0)),
            per_batch, per_batch,
            pl.BlockSpec((None,) + w.shape[1:], lambda i: (layer, 0, 0)),
            rope_spec, rope_spec, rope_spec,
        ],
        out_specs=out_specs,
        out_shape=out_shapes,
        scratch_shapes=[pltpu.VMEM((D_MODEL // LANES, tm, LANES), F32)],
        compiler_params=_params(("parallel",)),
        name=name,
    )(x, gain.reshape(1, d), shift, scale, w, *rope_tabs)


def _diff_attn_kernel(lam_ref, subln_ref, q_ref, k_ref, v_ref, o_ref, *, tq, lambda_init):
    lf = lam_ref[...]
    lam = (jnp.exp(jnp.sum(lf[0:1] * lf[1:2], keepdims=True))
           - jnp.exp(jnp.sum(lf[2:3] * lf[3:4], keepdims=True)) + lambda_init)
    lane_lo = lax.broadcasted_iota(jnp.int32, (tq, 2 * HEAD_DIM), 1) < HEAD_DIM
    causal = (lax.broadcasted_iota(jnp.int32, (tq, tq), 0)
              <= lax.broadcasted_iota(jnp.int32, (tq, tq), 1))

    def scores(kb, qm):
        return lax.dot_general(kb, qm, (((1,), (1,)), ((), ())), preferred_element_type=F32)

    def online(s, m, l, acc, vbt):
        m_new = jnp.maximum(m, jnp.max(s, axis=0, keepdims=True))
        p = jnp.exp2(s - m_new)
        alpha = jnp.exp2(m - m_new)
        l = alpha * l + jnp.sum(p, axis=0, keepdims=True)
        acc = alpha * acc + jnp.dot(vbt, p.astype(BF16), preferred_element_type=F32)
        return m_new, l, acc

    for i in range(q_ref.shape[0] // tq):
        q = q_ref[i * tq:(i + 1) * tq, :]
        zero = jnp.zeros_like(q)
        q1 = jnp.where(lane_lo, q, zero)
        q2 = jnp.where(lane_lo, zero, q)
        m1 = m2 = jnp.full((1, tq), NEG_BIG, F32)
        l1 = l2 = jnp.zeros((1, tq), F32)
        a1 = a2 = jnp.zeros((2 * HEAD_DIM, tq), F32)
        for j in range(i + 1):
            kb = k_ref[j * tq:(j + 1) * tq, :]
            vbt = v_ref[j * tq:(j + 1) * tq, :].T
            s1 = scores(kb, q1)
            s2 = scores(kb, q2)
            if j == i:
                s1 = jnp.where(causal, s1, NEG_BIG)
                s2 = jnp.where(causal, s2, NEG_BIG)
            m1, l1, a1 = online(s1, m1, l1, a1, vbt)
            m2, l2, a2 = online(s2, m2, l2, a2, vbt)
        o = (a1 / l1 - lam * (a2 / l2)).T
        o = o * lax.rsqrt(jnp.mean(o * o, axis=-1, keepdims=True) + NORM_EPS)
        o_ref[i * tq:(i + 1) * tq, :] = (o * subln_ref[...] * (1.0 - lambda_init)).astype(BF16)


def _diff_attention(q, k, v, lam_vecs, subln, lambda_init, *, n_b, seq):
    t = q.shape[0]
    width = 2 * HEAD_DIM
    head_seq = pl.BlockSpec((seq, width), lambda b, h: (b, h))
    return pl.pallas_call(
        functools.partial(_diff_attn_kernel, tq=512, lambda_init=lambda_init),
        grid=(n_b, A_HEADS),
        in_specs=[
            pl.BlockSpec((4, HEAD_DIM), lambda b, h: (0, 0)),
            pl.BlockSpec((1, width), lambda b, h: (0, 0)),
            head_seq, head_seq, head_seq,
        ],
        out_specs=head_seq,
        out_shape=jax.ShapeDtypeStruct((t, D_MODEL), BF16),
        compiler_params=_params(("parallel", "parallel")),
        name="diff_attention",
    )(lam_vecs, subln.reshape(1, width), q, k, v)


def _window_unit(q, kc, vc, bias, lane_lo):
    zero = jnp.zeros_like(q)
    qq = jnp.concatenate([jnp.where(lane_lo, q, zero), jnp.where(lane_lo, zero, q)], axis=0)
    s = lax.dot_general(qq, kc, (((1,), (1,)), ((), ())), preferred_element_type=F32) + bias
    m = jnp.max(s, axis=-1, keepdims=True)
    p = jnp.exp2(s - m).astype(BF16)
    v_aug = jnp.concatenate([vc, jnp.ones(vc.shape, BF16)], axis=1)
    o_aug = jnp.dot(p, v_aug, preferred_element_type=F32)
    n = WINDOW_STEPS
    o = jnp.where(lane_lo, o_aug[:n, :LANES], o_aug[n:, :LANES])
    l = jnp.where(lane_lo, o_aug[:n, LANES:], o_aug[n:, LANES:])
    m_sel = jnp.where(lane_lo, m[:n], m[n:])
    return o / l, m_sel + jnp.log2(l)


def _dil_attn_kernel(q0_ref, q1_ref, q2_ref, k0_ref, v0_ref, k1_ref, v1_ref, k2_ref, v2_ref,
                     o_ref, acc_ref, lse_ref, *, seq):
    n = WINDOW_STEPS
    lane_lo = lax.broadcasted_iota(jnp.int32, (n, LANES), 1) < HEAD_DIM
    qi = lax.broadcasted_iota(jnp.int32, (2 * n, 2 * n), 0) % n
    kj = lax.broadcasted_iota(jnp.int32, (2 * n, 2 * n), 1)
    band = jnp.where((kj >= qi) & (kj <= qi + n), 0.0, NEG_BIG)
    first = jnp.where(lax.broadcasted_iota(jnp.int32, (2 * n, n), 1)
                      <= lax.broadcasted_iota(jnp.int32, (2 * n, n), 0) % n, 0.0, NEG_BIG)

    def unit(g, q_rows, k_rows, v_rows, bias, dst):
        o, lse = _window_unit(q_rows, k_rows, v_rows, bias, lane_lo)
        acc_ref[g, dst, :] = o
        lse_ref[g, dst, :] = lse

    def rows(ref, r, lo, hi):
        return ref[lo:hi, :] if len(ref.shape) == 2 else ref[0, r, lo:hi, :]

    for g, (q_ref, k_ref, v_ref) in enumerate(((q0_ref, k0_ref, v0_ref), (q1_ref, k1_ref, v1_ref),
                                               (q2_ref, k2_ref, v2_ref))):
        dil = B_DILATIONS[g]
        for r in range(dil):
            for blk in range(seq // dil // n):
                dst = pl.ds(r + blk * n * dil, n, stride=dil) if dil > 1 else pl.ds(blk * n, n)
                k_lo = max(blk - 1, 0) * n
                unit(g, rows(q_ref, r, blk * n, (blk + 1) * n), rows(k_ref, r, k_lo, (blk + 1) * n),
                     rows(v_ref, r, k_lo, (blk + 1) * n), first if blk == 0 else band, dst)

    l0, l1, l2 = lse_ref[0], lse_ref[1], lse_ref[2]
    top = jnp.maximum(jnp.maximum(l0, l1), l2)
    w0, w1, w2 = jnp.exp2(l0 - top), jnp.exp2(l1 - top), jnp.exp2(l2 - top)
    o = (w0 * acc_ref[0] + w1 * acc_ref[1] + w2 * acc_ref[2]) / (w0 + w1 + w2)
    o_ref[...] = o.astype(BF16)


def _dilated_attention(q0, q1, q2, k0, v0, k1, v1, k2, v2, *, n_b, seq):
    t = q0.shape[0]
    pairs = D_MODEL // LANES
    nat = pl.BlockSpec((seq, LANES), lambda b, p: (b, p))

    def res(dil):
        return pl.BlockSpec((1, dil, seq // dil, LANES), lambda b, p: (b, 0, 0, p))

    d1, d2 = B_DILATIONS[1], B_DILATIONS[2]
    return pl.pallas_call(
        functools.partial(_dil_attn_kernel, seq=seq),
        grid=(n_b, pairs),
        in_specs=[nat, res(d1), res(d2), nat, nat, res(d1), res(d1), res(d2), res(d2)],
        out_specs=nat,
        out_shape=jax.ShapeDtypeStruct((t, D_MODEL), BF16),
        scratch_shapes=[pltpu.VMEM((3, seq, LANES), F32), pltpu.VMEM((3, seq, LANES), F32)],
        compiler_params=_params(("parallel", "parallel")),
        name="dilated_attention",
    )(q0, q1, q2, k0, v0, k1, v1, k2, v2)


def _mix_out_kernel(o_ref, wo_ref, x_ref, gate_ref, gain_ref, shift_ref, scale_ref, rw_ref, rb_ref,
                    xn_ref, h_ref, idx_ref, wgt_ref, rank_ref, cnt_ref, *, tm):
    y = jnp.dot(o_ref[...], wo_ref[...], preferred_element_type=F32)
    xn = x_ref[...] + gate_ref[0] * y
    xn_ref[...] = xn
    h = _modulated_norm(xn, gain_ref[...], shift_ref[0], scale_ref[0])

    h_hi = h.astype(BF16)
    h_ref[...] = h_hi
    h_lo = (h - h_hi.astype(F32)).astype(BF16)
    rw = rw_ref[...]
    rw_hi = rw.astype(BF16)
    rw_lo = (rw - rw_hi.astype(F32)).astype(BF16)
    nt = (((1,), (1,)), ((), ()))
    logits = (lax.dot_general(rw_hi, h_hi, nt, preferred_element_type=F32)
              + lax.dot_general(rw_lo, h_hi, nt, preferred_element_type=F32)
              + lax.dot_general(rw_hi, h_lo, nt, preferred_element_type=F32)) + rb_ref[...]

    e_iota = lax.broadcasted_iota(jnp.int32, logits.shape, 0)
    work = logits
    sels, tops, idxs = [], [], []
    for _ in range(TOP_K):
        mk = jnp.max(work, axis=0, keepdims=True)
        ik = jnp.min(jnp.where(work == mk, e_iota, N_EXPERTS), axis=0, keepdims=True)
        sel = e_iota == ik
        work = jnp.where(sel, -jnp.inf, work)
        sels.append(sel)
        tops.append(mk)
        idxs.append(ik)
    exps = [jnp.exp(m - tops[0]) for m in tops]
    denom = exps[0] + exps[1] + exps[2] + exps[3]
    for k in range(TOP_K):
        idx_ref[k:k + 1, :] = idxs[k]
        wgt_ref[k:k + 1, :] = exps[k] / denom

    chosen = jnp.zeros(logits.shape, F32)
    for sel in sels:
        chosen = chosen + jnp.where(sel, 1.0, 0.0)
    before = (lax.broadcasted_iota(jnp.int32, (tm, tm), 0)
              < lax.broadcasted_iota(jnp.int32, (tm, tm), 1))
    upper = jnp.where(before, 1.0, 0.0).astype(BF16)
    prefix = jnp.dot(chosen.astype(BF16), upper, preferred_element_type=F32)
    for k in range(TOP_K):
        rank_ref[k:k + 1, :] = jnp.sum(jnp.where(sels[k], prefix, 0.0), axis=0,
                                       keepdims=True).astype(jnp.int32)
    counts = jnp.sum(chosen, axis=1, keepdims=True).astype(jnp.int32)
    cnt_ref[0] = jnp.broadcast_to(counts, cnt_ref.shape[1:])


def _mix_out_and_route(o, wo, layer, x, gate, gain, shift, scale, router_w, router_b, *, n_b, seq):
    t, d = x.shape
    tm = ROUTE_TILE
    n_s = seq // tm
    per_batch = pl.BlockSpec((1, 1, d), lambda i: (i // n_s, 0, 0))
    row = pl.BlockSpec((tm, d), lambda i: (i, 0))
    sel = pl.BlockSpec((TOP_K, tm), lambda i: (0, i))
    return pl.pallas_call(
        functools.partial(_mix_out_kernel, tm=tm),
        grid=(t // tm,),
        in_specs=[
            row,
            pl.BlockSpec((None,) + wo.shape[1:], lambda i: (layer, 0, 0)),
            row, per_batch,
            pl.BlockSpec((1, d), lambda i: (0, 0)),
            per_batch, per_batch,
            pl.BlockSpec((N_EXPERTS, d), lambda i: (0, 0)),
            pl.BlockSpec((N_EXPERTS, 1), lambda i: (0, 0)),
        ],
        out_specs=[row, row, sel, sel, sel,
                   pl.BlockSpec((1, N_EXPERTS, LANES), lambda i: (i, 0, 0))],
        out_shape=[
            jax.ShapeDtypeStruct((t, d), F32),
            jax.ShapeDtypeStruct((t, d), BF16),
            jax.ShapeDtypeStruct((TOP_K, t), jnp.int32),
            jax.ShapeDtypeStruct((TOP_K, t), F32),
            jax.ShapeDtypeStruct((TOP_K, t), jnp.int32),
            jax.ShapeDtypeStruct((t // tm, N_EXPERTS, LANES), jnp.int32),
        ],
        compiler_params=_params(("parallel",)),
        name="mix_out_route",
    )(o, wo, x, gate, gain.reshape(1, d), shift, scale, router_w.T, router_b.reshape(N_EXPERTS, 1))


HIGH_HALF = 0xFFFF0000


def _pack_bf16_pairs(x):
    c = x.shape[1] // 2
    lo = lax.bitcast_convert_type(x[:, :c], jnp.uint32)
    hi = lax.bitcast_convert_type(x[:, c:], jnp.uint32)
    return (lo >> 16) | (hi & jnp.uint32(HIGH_HALF))


def _unpack_bf16_pairs(w):
    lo = lax.bitcast_convert_type(w << 16, F32).astype(BF16)
    hi = lax.bitcast_convert_type(w & jnp.uint32(HIGH_HALF), F32).astype(BF16)
    return lo, hi


def _copy_tables(grp, toff, tstart):
    experts = jnp.arange(N_EXPERTS, dtype=jnp.int32)

    def words(count, first_row, rows, width):
        end = jnp.cumsum(count, axis=1)
        j = jnp.arange(width, dtype=jnp.int32)
        e_of = jnp.sum((end[:, None, :] <= j[None, :, None]).astype(jnp.int32), axis=2)
        onehot = e_of[:, :, None] == experts[None, None, :]

        def pick(a):
            return jnp.sum(jnp.where(onehot, a[:, None, :], 0), axis=2)

        row = pick(first_row) + (j[None, :] - pick(end - count)) * rows
        word = ((pick(tstart) + row) << ROW_BITS) | (pick(toff) + row)
        return word.reshape(-1).astype(jnp.int32), end[:, -1]

    n_big = grp // BIG_COPY
    big, total_big = words(n_big, jnp.zeros_like(grp), BIG_COPY, MAX_BIG)
    small, total_small = words((grp - n_big * BIG_COPY) // GROUP_ALIGN, n_big * BIG_COPY,
                               GROUP_ALIGN, MAX_SMALL)
    counts = jnp.stack([total_big, total_small, jnp.sum(grp, axis=1)], axis=1)
    return counts.reshape(-1).astype(jnp.int32), big, small


def _tile_copies(tile, cnt_ref, big_ref, small_ref, make_copy, wait):
    if wait:
        total = cnt_ref[3 * tile + 2]

        def wait_rows(rows):
            def body(j, carry):
                make_copy(0, 0, rows).wait()
                return carry
            return body

        lax.fori_loop(0, total // BIG_COPY, wait_rows(BIG_COPY), 0)
        lax.fori_loop(0, (total % BIG_COPY) // GROUP_ALIGN, wait_rows(GROUP_ALIGN), 0)
        return

    def start_rows(table_ref, width, rows):
        def body(j, carry):
            word = table_ref[tile * width + j]
            make_copy(pl.multiple_of(word & ((1 << ROW_BITS) - 1), GROUP_ALIGN),
                      pl.multiple_of(word >> ROW_BITS, GROUP_ALIGN), rows).start()
            return carry
        return body

    lax.fori_loop(0, cnt_ref[3 * tile], start_rows(big_ref, MAX_BIG, BIG_COPY), 0)
    lax.fori_loop(0, cnt_ref[3 * tile + 1], start_rows(small_ref, MAX_SMALL, GROUP_ALIGN), 0)


def _dispatch_kernel(cnt_ref, big_ref, small_ref, fill_row_ref, fill_on_ref, nvalid_ref,
                     h_ref, idx_ref, rank_ref, wgt_ref, off_ref, xs_ref, loc_ref, z_ref, zero_ref,
                     sems, *, n_blocks):
    i = pl.program_id(0)
    last = pl.num_programs(0) - 1
    slot = i % 2
    tm = h_ref.shape[0]

    def copies(tile, buf, wait):
        def make_copy(local, glob, n):
            return pltpu.make_async_copy(z_ref.at[buf, pl.ds(local, n)],
                                         xs_ref.at[pl.ds(glob, n)], sems.at[buf])

        _tile_copies(tile, cnt_ref, big_ref, small_ref, make_copy, wait)

    @pl.when(i == 0)
    def _():
        zero_ref[...] = jnp.zeros_like(zero_ref)

        def fill(row):
            return pltpu.make_async_copy(
                zero_ref, xs_ref.at[pl.ds(pl.multiple_of(row, EXPERT_ROWS), EXPERT_ROWS)],
                sems.at[2])

        def fills(act):
            def last_block(e, carry):
                @pl.when(fill_on_ref[e] == 1)
                def _():
                    act(fill(fill_row_ref[e]))
                return carry

            lax.fori_loop(0, N_EXPERTS, last_block, 0)

            def tail_block(b, carry):
                act(fill(b * EXPERT_ROWS))
                return carry

            lax.fori_loop(nvalid_ref[0], n_blocks, tail_block, 0)

        fills(lambda cp: cp.start())
        fills(lambda cp: cp.wait())

    e_iota = lax.broadcasted_iota(jnp.int32, (N_EXPERTS, tm), 0)
    off = off_ref[0][:, 0:1]
    locs = []
    for k in range(TOP_K):
        sel = e_iota == idx_ref[k:k + 1, :]
        loc = jnp.sum(jnp.where(sel, off, 0), axis=0, keepdims=True) + rank_ref[k:k + 1, :]
        loc_ref[k:k + 1, :] = loc
        locs.append(loc)

    @pl.when(i >= 2)
    def _():
        copies(i - 2, slot, wait=True)

    h = h_ref[...]
    half = h.shape[1] // 2
    rows = 256
    for c in range(COMPACT_ROWS // rows):
        r_iota = lax.broadcasted_iota(jnp.int32, (rows, tm), 0) + c * rows
        gate = jnp.zeros((rows, tm), F32)
        for k, loc in enumerate(locs):
            gate = jnp.where(r_iota == loc, wgt_ref[k:k + 1, :], gate)
        hit = jnp.where(gate != 0.0, 1.0, 0.0)
        z_ref[slot, c * rows:(c + 1) * rows, :half] = _pack_bf16_pairs(
            jnp.dot(hit.astype(BF16), h, preferred_element_type=F32))
        row_gate = jnp.sum(gate, axis=1, keepdims=True)
        z_ref[slot, c * rows:(c + 1) * rows, half:] = lax.bitcast_convert_type(
            jnp.broadcast_to(row_gate, (rows, LANES)), jnp.uint32)

    copies(i, slot, wait=False)

    @pl.when(i == last)
    def _():
        @pl.when(i >= 1)
        def _():
            copies(i - 1, 1 - slot, wait=True)

        copies(i, slot, wait=True)


def _dispatch(h, idx, rank, wgt, off_b, tables, fills, n_rows):
    t, d = h.shape
    tm = ROUTE_TILE
    width = d // 2 + LANES
    sel = pl.BlockSpec((TOP_K, tm), lambda i, *_: (0, i))
    grid_spec = pltpu.PrefetchScalarGridSpec(
        num_scalar_prefetch=6,
        grid=(t // tm,),
        in_specs=[
            pl.BlockSpec((tm, d), lambda i, *_: (i, 0)),
            sel, sel, sel,
            pl.BlockSpec((1, N_EXPERTS, LANES), lambda i, *_: (i, 0, 0)),
        ],
        out_specs=[pl.BlockSpec(memory_space=pl.ANY), sel],
        scratch_shapes=[pltpu.VMEM((2, COMPACT_ROWS, width), jnp.uint32),
                        pltpu.VMEM((EXPERT_ROWS, width), jnp.uint32),
                        pltpu.SemaphoreType.DMA((3,))],
    )
    return pl.pallas_call(
        functools.partial(_dispatch_kernel, n_blocks=n_rows // EXPERT_ROWS),
        grid_spec=grid_spec,
        out_shape=[jax.ShapeDtypeStruct((n_rows, width), jnp.uint32),
                   jax.ShapeDtypeStruct((TOP_K, t), jnp.int32)],
        compiler_params=_params(("arbitrary",)),
        name="moe_dispatch",
    )(*tables, *fills, h, idx, rank, wgt, off_b)


def _expert_kernel(be_ref, first_ref, live_ref, nvalid_ref, xs_ref, wgu_ref, bgu_ref, wd_ref,
                   bd_ref, ys_ref, wgu_bf, wd_bf):
    del be_ref, nvalid_ref
    b = pl.program_id(0)
    bm = xs_ref.shape[0]

    @pl.when(first_ref[b] == 1)
    def _():
        wgu_bf[...] = wgu_ref[0].astype(BF16)
        wd_bf[...] = wd_ref[0].astype(BF16)

    def ffn(rows):
        d_ff = wd_bf.shape[0]
        half = wd_bf.shape[1] // 2
        x = jnp.concatenate(_unpack_bf16_pairs(xs_ref[:rows, :half]), axis=1)
        row_gate = lax.bitcast_convert_type(xs_ref[:rows, half:], F32)
        gu = jnp.dot(x, wgu_bf[...], preferred_element_type=F32) + bgu_ref[0]
        gate = jnp.minimum(gu[:, :d_ff], SWIGLU_LIMIT)
        up = jnp.clip(gu[:, d_ff:], -SWIGLU_LIMIT, SWIGLU_LIMIT)
        glu = gate * jax.nn.sigmoid(SWIGLU_ALPHA * gate)
        act = ((up + 1.0) * glu).astype(BF16)
        y = jnp.dot(act, wd_bf[...], preferred_element_type=F32) + bd_ref[0]
        y = y * jnp.concatenate([row_gate] * (y.shape[1] // LANES), axis=1)
        ys_ref[:rows, :] = _pack_bf16_pairs(y.astype(BF16).astype(F32))

    for parts in range(bm // EXPERT_PART + 1):
        @pl.when(live_ref[b] == parts)
        def _():
            rows = parts * EXPERT_PART
            if rows > 0:
                ffn(rows)
            if rows < bm:
                ys_ref[rows:, :] = jnp.zeros((bm - rows, ys_ref.shape[1]), ys_ref.dtype)


def _experts(xs, block_expert, block_first, block_live, n_valid, layer, w_gu, b_gu, w_d, b_d):
    n_rows, in_width = xs.shape
    bm = EXPERT_ROWS
    d, d_ff = w_d.shape[3], w_d.shape[2]
    half = d // 2
    n_l = w_gu.shape[0]
    grid_spec = pltpu.PrefetchScalarGridSpec(
        num_scalar_prefetch=4,
        grid=(n_rows // bm,),
        in_specs=[
            pl.BlockSpec((bm, in_width), lambda b, be, fi, lv, nv: (jnp.minimum(b, nv[0] - 1), 0)),
            pl.BlockSpec((None, 1, d, 2 * d_ff), lambda b, be, *_: (layer, be[b], 0, 0)),
            pl.BlockSpec((None, 1, 1, 2 * d_ff), lambda b, be, *_: (layer, be[b], 0, 0)),
            pl.BlockSpec((None, 1, d_ff, d), lambda b, be, *_: (layer, be[b], 0, 0)),
            pl.BlockSpec((None, 1, 1, d), lambda b, be, *_: (layer, be[b], 0, 0)),
        ],
        out_specs=pl.BlockSpec((bm, half), lambda b, *_: (b, 0)),
        scratch_shapes=[pltpu.VMEM((d, 2 * d_ff), BF16), pltpu.VMEM((d_ff, d), BF16)],
    )
    return pl.pallas_call(
        _expert_kernel,
        grid_spec=grid_spec,
        out_shape=jax.ShapeDtypeStruct((n_rows, half), jnp.uint32),
        compiler_params=_params(("arbitrary",)),
        name="moe_experts",
    )(block_expert, block_first, block_live, n_valid, xs, w_gu,
      b_gu.reshape(n_l, N_EXPERTS, 1, 2 * d_ff), w_d, b_d.reshape(n_l, N_EXPERTS, 1, d))


def _combine_kernel(cnt_ref, big_ref, small_ref, loc_ref, ys_ref, x_ref, gate_ref,
                    *rest, final):
    fin_refs, (o_ref, y_ref, sems) = rest[:-3], rest[-3:]
    i = pl.program_id(0)
    slot = i % 2
    tm = x_ref.shape[0]

    def copies(tile, buf, wait):
        def make_copy(local, glob, n):
            return pltpu.make_async_copy(ys_ref.at[pl.ds(glob, n)],
                                         y_ref.at[buf, pl.ds(local, n)], sems.at[buf])

        _tile_copies(tile, cnt_ref, big_ref, small_ref, make_copy, wait)

    @pl.when(i == 0)
    def _():
        y_ref[...] = jnp.zeros_like(y_ref)
        copies(0, 0, wait=False)

    copies(i, slot, wait=True)

    @pl.when(i + 1 < pl.num_programs(0))
    def _():
        copies(i + 1, 1 - slot, wait=False)

    rows = COMBINE_CHUNK
    half = y_ref.shape[2]
    f_lo = jnp.zeros((tm, half), F32)
    f_hi = jnp.zeros((tm, half), F32)
    for c in range(COMPACT_ROWS // rows):
        c_iota = lax.broadcasted_iota(jnp.int32, (tm, rows), 1) + c * rows
        q = jnp.zeros((tm, rows), F32)
        for k in range(TOP_K):
            q = jnp.where(c_iota == loc_ref[:, k:k + 1], 1.0, q)
        q = q.astype(BF16)
        y_lo, y_hi = _unpack_bf16_pairs(y_ref[slot, c * rows:(c + 1) * rows, :])
        f_lo = f_lo + jnp.dot(q, y_lo, preferred_element_type=F32)
        f_hi = f_hi + jnp.dot(q, y_hi, preferred_element_type=F32)
    out = x_ref[...] + gate_ref[0] * jnp.concatenate([f_lo, f_hi], axis=1)
    if final:
        gain_ref, shift_ref, scale_ref = fin_refs
        out = _modulated_norm(out, gain_ref[...], shift_ref[0], scale_ref[0])
    o_ref[...] = out


def _combine(loc_tk, ys, x, gate, tables, final_mod, *, seq):
    t, d = x.shape
    tm = ROUTE_TILE
    n_s = seq // tm
    sel = pl.BlockSpec((tm, TOP_K), lambda i, *_: (i, 0))
    per_batch = pl.BlockSpec((1, 1, d), lambda i, *_: (i // n_s, 0, 0))
    in_specs = [sel, pl.BlockSpec(memory_space=pl.ANY),
                pl.BlockSpec((tm, d), lambda i, *_: (i, 0)), per_batch]
    extra = ()
    if final_mod is not None:
        gain, shift, scale = final_mod
        extra = (gain.reshape(1, d), shift, scale)
        in_specs += [pl.BlockSpec((1, d), lambda i, *_: (0, 0)), per_batch, per_batch]
    grid_spec = pltpu.PrefetchScalarGridSpec(
        num_scalar_prefetch=3,
        grid=(t // tm,),
        in_specs=in_specs,
        out_specs=pl.BlockSpec((tm, d), lambda i, *_: (i, 0)),
        scratch_shapes=[pltpu.VMEM((2, COMPACT_ROWS, d // 2), jnp.uint32),
                        pltpu.SemaphoreType.DMA((2,))],
    )
    return pl.pallas_call(
        functools.partial(_combine_kernel, final=final_mod is not None),
        grid_spec=grid_spec,
        out_shape=jax.ShapeDtypeStruct((t, d), F32),
        compiler_params=_params(("arbitrary",)),
        name="moe_combine",
    )(*tables, loc_tk, ys, x, gate, *extra)


def _moe(h, idx, wgt, rank, tile_cnt, x, gate, layer, w_gu, b_gu, w_d, b_d, final_mod, *, seq):
    t = h.shape[0]
    bm = EXPERT_ROWS
    n_tiles = t // ROUTE_TILE
    n_rows = t * TOP_K + n_tiles * N_EXPERTS * GROUP_ALIGN + N_EXPERTS * bm
    n_blocks = n_rows // bm
    cnt = tile_cnt[:, :, 0]
    grp = (cnt + GROUP_ALIGN - 1) // GROUP_ALIGN * GROUP_ALIGN
    expert_rows = jnp.sum(grp, axis=0)
    padded = (expert_rows + bm - 1) // bm * bm
    pad_end = jnp.cumsum(padded)
    tstart = (pad_end - padded)[None, :] + jnp.cumsum(grp, axis=0) - grp
    toff = jnp.cumsum(grp, axis=1) - grp
    tables = _copy_tables(grp, toff, tstart)
    off_b = jnp.broadcast_to(toff[:, :, None], (n_tiles, N_EXPERTS, LANES)).astype(jnp.int32)
    block_row = jnp.arange(n_blocks, dtype=jnp.int32) * bm
    block_expert = jnp.minimum(
        jnp.sum((pad_end[None, :] <= block_row[:, None]).astype(jnp.int32), axis=1),
        N_EXPERTS - 1).astype(jnp.int32)
    block_first = jnp.concatenate(
        [jnp.ones((1,), jnp.int32), (block_expert[1:] != block_expert[:-1]).astype(jnp.int32)])
    n_valid = (pad_end[-1:] // bm).astype(jnp.int32)
    in_expert = block_expert[:, None] == jnp.arange(N_EXPERTS, dtype=jnp.int32)[None, :]
    rows_left = jnp.sum(jnp.where(in_expert, (pad_end - padded + expert_rows)[None, :], 0),
                        axis=1) - block_row
    block_live = jnp.where(block_row >= pad_end[-1], 0,
                           jnp.clip((rows_left + EXPERT_PART - 1) // EXPERT_PART, 1,
                                    bm // EXPERT_PART)).astype(jnp.int32)
    fills = (jnp.maximum(pad_end - bm, 0).astype(jnp.int32), (padded > 0).astype(jnp.int32),
             n_valid)
    xs, loc = _dispatch(h, idx, rank, wgt, off_b, tables, fills, n_rows)
    ys = _experts(xs, block_expert, block_first, block_live, n_valid, layer, w_gu, b_gu, w_d, b_d)
    return _combine(loc.T, ys, x, gate, tables, final_mod, seq=seq)


def kernel(x, c, mod_w, mod_b, mix_norm, ffn_norm, a_wqkv, a_wo, a_lambda, a_subln, kv_norm, kv_mod_w, kv_mod_b, kv_w, b_wq, b_wo, router_w, router_b, exp_w_gate_up, exp_b_gate_up, exp_w_down, exp_b_down, final_norm, final_mod_w, final_mod_b):
    n_b, seq, d = x.shape
    t = n_b * seq
    sizes = dict(n_b=n_b, seq=seq)
    rope_tabs = _rope_tables(seq)
    q_scale = HEAD_DIM ** -0.5 * math.log2(math.e)

    def per_batch(v):
        return v.reshape(n_b, 1, d)

    mod = _adaln_vectors(c, mod_w, mod_b)
    kv_mod = _adaln_vectors(c, kv_mod_w[None], kv_mod_b[None])[0]
    fin_mod = _adaln_vectors(c, final_mod_w[None], final_mod_b[None])[0]

    a_wqkv, a_wo, b_wq, b_wo = (w.astype(BF16) for w in (a_wqkv, a_wo, b_wq, b_wo))
    kv_w = kv_w[None].astype(BF16)

    xt = x.reshape(t, d)
    shared = None
    for layer in range(DEPTH):
        sh1, sc1, g1, sh2, sc2, g2 = (per_batch(mod[layer, :, j * d:(j + 1) * d]) for j in range(6))
        if layer < N_A_LAYERS:
            lambda_init = 0.8 - 0.6 * math.exp(-0.3 * layer)
            segs = ((0, True, q_scale, (1,)), (1, True, 1.0, (1,)), (2, False, 1.0, (1,)))
            q, k, v = _norm_project(xt, mix_norm[layer], sh1, sc1, a_wqkv, layer,
                                    rope_tabs, segs, name="a_qkv_proj", **sizes)
            o = _diff_attention(q, k, v, a_lambda[layer], a_subln[layer], lambda_init, **sizes)
            wo, wo_layer = a_wo, layer
        else:
            j = layer - N_A_LAYERS
            segs = tuple((g, True, q_scale, (B_DILATIONS[g],)) for g in range(3))
            q0, q1, q2 = _norm_project(xt, mix_norm[layer], sh1, sc1, b_wq, j,
                                       rope_tabs, segs, name="b_q_proj", **sizes)
            o = _dilated_attention(q0, q1, q2, *shared, **sizes)
            wo, wo_layer = b_wo, j
        xt, h, idx, wgt, rank, tile_cnt = _mix_out_and_route(
            o, wo, wo_layer, xt, g1, ffn_norm[layer], sh2, sc2, router_w[layer],
            router_b[layer], **sizes)
        final_mod = None
        if layer == DEPTH - 1:
            final_mod = (final_norm, per_batch(fin_mod[:, :d]), per_batch(fin_mod[:, d:]))
        xt = _moe(h, idx, wgt, rank, tile_cnt, xt, g2, layer, exp_w_gate_up, exp_b_gate_up,
                  exp_w_down, exp_b_down, final_mod, seq=seq)
        if layer == N_A_LAYERS - 1:
            segs = ((0, True, 1.0, B_DILATIONS), (1, False, 1.0, B_DILATIONS))
            k0, k1, k2, v0, v1, v2 = _norm_project(
                xt, kv_norm, per_batch(kv_mod[:, :d]), per_batch(kv_mod[:, d:]),
                kv_w, 0, rope_tabs, segs, name="shared_kv_proj", **sizes)
            shared = (k0, v0, k1, v1, k2, v2)
    return xt.reshape(n_b, seq, d)
```

```python
import functools
import math

import jax
import jax.numpy as jnp
from jax import lax
from jax.experimental import pallas as pl
from jax.experimental.pallas import tpu as pltpu

F32 = jnp.float32
BF16 = jnp.bfloat16

D_MODEL = 1024
HEAD_DIM = 64
A_HEADS = 8
B_KV_HEADS = 16
B_DILATIONS = (1, 4, 16)
WINDOW_STEPS = 128
ROPE_THETA = 500000.0
ROPE_DIM = HEAD_DIM // 4
N_EXPERTS = 32
TOP_K = 4
SWIGLU_LIMIT = 7.0
SWIGLU_ALPHA = 1.702
NORM_EPS = 1e-5
N_A_LAYERS = 2
DEPTH = 4

LANES = 128
BF16_SUBLANES = 16
EXPERT_ROWS = 1024
EXPERT_PART = 256
ROUTE_TILE = 512
GROUP_ALIGN = 8
COMPACT_ROWS = ROUTE_TILE * TOP_K + N_EXPERTS * GROUP_ALIGN
COMBINE_CHUNK = COMPACT_ROWS
BIG_COPY = 32
MAX_BIG = COMPACT_ROWS // BIG_COPY
MAX_SMALL = N_EXPERTS * (BIG_COPY // GROUP_ALIGN - 1)
ROW_BITS = 12
NEG_BIG = -1e30
V7X_VMEM_BYTES = 64 * 1024 * 1024
VMEM_LIMIT = V7X_VMEM_BYTES // 8 * 7


def _params(sem, vmem=VMEM_LIMIT):
    return pltpu.CompilerParams(dimension_semantics=sem, vmem_limit_bytes=vmem)


def _adaln_kernel(c_ref, w_ref, b_ref, o_ref):
    c = c_ref[...]
    c_act = (c * jax.nn.sigmoid(c)).astype(BF16)
    o_ref[0] = jnp.dot(c_act, w_ref[0].astype(BF16), preferred_element_type=F32) + b_ref[0]


def _adaln_vectors(c, w, b):
    n_l, d, n = w.shape
    n_b = c.shape[0]
    tn = 1024
    return pl.pallas_call(
        _adaln_kernel,
        grid=(n_l, n // tn),
        in_specs=[
            pl.BlockSpec((n_b, d), lambda l, j: (0, 0)),
            pl.BlockSpec((1, d, tn), lambda l, j: (l, 0, j)),
            pl.BlockSpec((1, 1, tn), lambda l, j: (l, 0, j)),
        ],
        out_specs=pl.BlockSpec((1, n_b, tn), lambda l, j: (l, 0, j)),
        out_shape=jax.ShapeDtypeStruct((n_l, n_b, n), F32),
        compiler_params=_params(("parallel", "parallel")),
        name="adaln_vectors",
    )(c, w, b.reshape(n_l, 1, n))


def _modulated_norm(x, gain, shift, scale):
    y = x * lax.rsqrt(jnp.mean(x * x, axis=-1, keepdims=True) + NORM_EPS)
    return y * gain * (1.0 + scale) + shift


def _rope_tables(seq):
    inv = ROPE_THETA ** (-jnp.arange(0, ROPE_DIM, 2, dtype=F32) / ROPE_DIM)
    ang = jnp.arange(seq, dtype=F32)[:, None] * inv[None, :]
    cos, sin = jnp.cos(ang), jnp.sin(ang)
    half = ROPE_DIM // 2
    rest = HEAD_DIM - ROPE_DIM
    zeros = jnp.zeros((seq, half), F32)
    c_tab = jnp.concatenate([cos, cos, jnp.ones((seq, rest), F32)], axis=1)
    s1_tab = jnp.concatenate([zeros, sin, jnp.zeros((seq, rest), F32)], axis=1)
    s2_tab = jnp.concatenate([-sin, zeros, jnp.zeros((seq, rest), F32)], axis=1)
    rep = LANES // HEAD_DIM
    return tuple(jnp.tile(t, (1, rep)) for t in (c_tab, s1_tab, s2_tab))


def _apply_rope(y, c_tab, s1_tab, s2_tab):
    parts = []
    for j in range(y.shape[1] // LANES):
        yj = y[:, j * LANES:(j + 1) * LANES]
        parts.append(yj * c_tab + pltpu.roll(yj, ROPE_DIM // 2, 1) * s1_tab
                     + pltpu.roll(yj, LANES - ROPE_DIM // 2, 1) * s2_tab)
    return jnp.concatenate(parts, axis=1)


def _proj_kernel(x_ref, gain_ref, shift_ref, scale_ref, w_ref, rc_ref, rs1_ref, rs2_ref, *rest,
                 segs, tm):
    n_out = sum(len(s[3]) for s in segs)
    out_refs, scr_ref = rest[:n_out], rest[n_out]
    h = _modulated_norm(x_ref[...], gain_ref[...], shift_ref[0], scale_ref[0]).astype(BF16)
    oi = 0
    for chunk, rope, mult, dils in segs:
        y = jnp.dot(h, w_ref[:, chunk * D_MODEL:(chunk + 1) * D_MODEL], preferred_element_type=F32)
        if rope:
            y = _apply_rope(y, rc_ref[...], rs1_ref[...], rs2_ref[...])
        if mult != 1.0:
            y = y * mult
        for dil in dils:
            o_ref = out_refs[oi]
            oi += 1
            if dil == 1:
                o_ref[...] = y.astype(BF16)
            else:
                for j in range(D_MODEL // LANES):
                    scr_ref[j] = y[:, j * LANES:(j + 1) * LANES]
                for r in range(dil):
                    for j in range(D_MODEL // LANES):
                        o_ref[0, r, :, j * LANES:(j + 1) * LANES] = (
                            scr_ref[j, pl.ds(r, tm // dil, stride=dil), :].astype(BF16))


def _norm_project(x, gain, shift, scale, w, layer, rope_tabs, segs, *, n_b, seq, name):
    t, d = x.shape
    tm = 512
    n_s = seq // tm
    out_shapes, out_specs = [], []
    for _, _, _, dils in segs:
        for dil in dils:
            if dil == 1:
                out_shapes.append(jax.ShapeDtypeStruct((t, D_MODEL), BF16))
                out_specs.append(pl.BlockSpec((tm, D_MODEL), lambda i: (i, 0)))
            else:
                out_shapes.append(jax.ShapeDtypeStruct((n_b, dil, seq // dil, D_MODEL), BF16))
                out_specs.append(pl.BlockSpec((1, dil, tm // dil, D_MODEL),
                                              lambda i: (i // n_s, 0, i % n_s, 0)))
    per_batch = pl.BlockSpec((1, 1, d), lambda i: (i // n_s, 0, 0))
    rope_spec = pl.BlockSpec((tm, LANES), lambda i: (i % n_s, 0))
    return pl.pallas_call(
        functools.partial(_proj_kernel, segs=segs, tm=tm),
        grid=(t // tm,),
        in_specs=[
            pl.BlockSpec((tm, d), lambda i: (i, 0)),
            pl.BlockSpec((1, d), lambda i: (0, 0)),
            per_batch, per_batch,
            pl.BlockSpec((None,) + w.shape[1:], lambda i: (layer, 0, 0)),
            rope_spec, rope_spec, rope_spec,
        ],
        out_specs=out_specs,
        out_shape=out_shapes,
        scratch_shapes=[pltpu.VMEM((D_MODEL // LANES, tm, LANES), F32)],
        compiler_params=_params(("parallel",)),
        name=name,
    )(x, gain.reshape(1, d), shift, scale, w, *rope_tabs)


def _diff_attn_kernel(lam_ref, subln_ref, q_ref, k_ref, v_ref, o_ref, *, tq, lambda_init):
    lf = lam_ref[...]
    lam = (jnp.exp(jnp.sum(lf[0:1] * lf[1:2], keepdims=True))
           - jnp.exp(jnp.sum(lf[2:3] * lf[3:4], keepdims=True)) + lambda_init)
    lane_lo = lax.broadcasted_iota(jnp.int32, (tq, 2 * HEAD_DIM), 1) < HEAD_DIM
    causal = (lax.broadcasted_iota(jnp.int32, (tq, tq), 0)
              <= lax.broadcasted_iota(jnp.int32, (tq, tq), 1))

    def scores(kb, qm):
        return lax.dot_general(kb, qm, (((1,), (1,)), ((), ())), preferred_element_type=F32)

    def online(s, m, acc, vbt):
        m_new = jnp.maximum(m, jnp.max(s, axis=0, keepdims=True))
        p = jnp.exp2(s - m_new)
        alpha = jnp.exp2(m - m_new)
        acc = alpha * acc + jnp.dot(vbt, p.astype(BF16), preferred_element_type=F32)
        return m_new, acc

    d_v = 2 * HEAD_DIM
    ones_rows = jnp.ones((BF16_SUBLANES, tq), BF16)
    for i in range(q_ref.shape[0] // tq):
        q = q_ref[i * tq:(i + 1) * tq, :]
        zero = jnp.zeros_like(q)
        q1 = jnp.where(lane_lo, q, zero)
        q2 = jnp.where(lane_lo, zero, q)
        m1 = m2 = jnp.full((1, tq), NEG_BIG, F32)
        a1 = a2 = jnp.zeros((d_v + BF16_SUBLANES, tq), F32)
        for j in range(i + 1):
            kb = k_ref[j * tq:(j + 1) * tq, :]
            vbt = jnp.concatenate([v_ref[j * tq:(j + 1) * tq, :].T, ones_rows], axis=0)
            s1 = scores(kb, q1)
            s2 = scores(kb, q2)
            if j == i:
                s1 = jnp.where(causal, s1, NEG_BIG)
                s2 = jnp.where(causal, s2, NEG_BIG)
            m1, a1 = online(s1, m1, a1, vbt)
            m2, a2 = online(s2, m2, a2, vbt)
        o = (a1[:d_v] / a1[d_v:d_v + 1] - lam * (a2[:d_v] / a2[d_v:d_v + 1])).T
        o = o * lax.rsqrt(jnp.mean(o * o, axis=-1, keepdims=True) + NORM_EPS)
        o_ref[i * tq:(i + 1) * tq, :] = (o * subln_ref[...] * (1.0 - lambda_init)).astype(BF16)


def _diff_attention(q, k, v, lam_vecs, subln, lambda_init, *, n_b, seq):
    t = q.shape[0]
    width = 2 * HEAD_DIM
    head_seq = pl.BlockSpec((seq, width), lambda b, h: (b, h))
    return pl.pallas_call(
        functools.partial(_diff_attn_kernel, tq=512, lambda_init=lambda_init),
        grid=(n_b, A_HEADS),
        in_specs=[
            pl.BlockSpec((4, HEAD_DIM), lambda b, h: (0, 0)),
            pl.BlockSpec((1, width), lambda b, h: (0, 0)),
            head_seq, head_seq, head_seq,
        ],
        out_specs=head_seq,
        out_shape=jax.ShapeDtypeStruct((t, D_MODEL), BF16),
        compiler_params=_params(("parallel", "parallel")),
        name="diff_attention",
    )(lam_vecs, subln.reshape(1, width), q, k, v)


def _window_unit(q, kc, vc, bias, lane_lo):
    zero = jnp.zeros_like(q)
    qq = jnp.concatenate([jnp.where(lane_lo, q, zero), jnp.where(lane_lo, zero, q)], axis=0)
    s = lax.dot_general(qq, kc, (((1,), (1,)), ((), ())), preferred_element_type=F32) + bias
    m = jnp.max(s, axis=-1, keepdims=True)
    p = jnp.exp2(s - m).astype(BF16)
    v_aug = jnp.concatenate([vc, jnp.ones(vc.shape, BF16)], axis=1)
    o_aug = jnp.dot(p, v_aug, preferred_element_type=F32)
    n = WINDOW_STEPS
    o = jnp.where(lane_lo, o_aug[:n, :LANES], o_aug[n:, :LANES])
    l = jnp.where(lane_lo, o_aug[:n, LANES:], o_aug[n:, LANES:])
    m_sel = jnp.where(lane_lo, m[:n], m[n:])
    return o / l, m_sel + jnp.log2(l)


def _dil_attn_kernel(q0_ref, q1_ref, q2_ref, k0_ref, v0_ref, k1_ref, v1_ref, k2_ref, v2_ref,
                     o_ref, acc_ref, lse_ref, *, seq):
    n = WINDOW_STEPS
    lane_lo = lax.broadcasted_iota(jnp.int32, (n, LANES), 1) < HEAD_DIM
    qi = lax.broadcasted_iota(jnp.int32, (2 * n, 2 * n), 0) % n
    kj = lax.broadcasted_iota(jnp.int32, (2 * n, 2 * n), 1)
    band = jnp.where((kj >= qi) & (kj <= qi + n), 0.0, NEG_BIG)
    first = jnp.where(lax.broadcasted_iota(jnp.int32, (2 * n, n), 1)
                      <= lax.broadcasted_iota(jnp.int32, (2 * n, n), 0) % n, 0.0, NEG_BIG)

    def unit(g, q_rows, k_rows, v_rows, bias, dst):
        o, lse = _window_unit(q_rows, k_rows, v_rows, bias, lane_lo)
        acc_ref[g, dst, :] = o
        lse_ref[g, dst, :] = lse

    def rows(ref, r, lo, hi):
        return ref[lo:hi, :] if len(ref.shape) == 2 else ref[0, r, lo:hi, :]

    for g, (q_ref, k_ref, v_ref) in enumerate(((q0_ref, k0_ref, v0_ref), (q1_ref, k1_ref, v1_ref),
                                               (q2_ref, k2_ref, v2_ref))):
        dil = B_DILATIONS[g]
        for r in range(dil):
            for blk in range(seq // dil // n):
                dst = pl.ds(r + blk * n * dil, n, stride=dil) if dil > 1 else pl.ds(blk * n, n)
                k_lo = max(blk - 1, 0) * n
                unit(g, rows(q_ref, r, blk * n, (blk + 1) * n), rows(k_ref, r, k_lo, (blk + 1) * n),
                     rows(v_ref, r, k_lo, (blk + 1) * n), first if blk == 0 else band, dst)

    l0, l1, l2 = lse_ref[0], lse_ref[1], lse_ref[2]
    top = jnp.maximum(jnp.maximum(l0, l1), l2)
    w0, w1, w2 = jnp.exp2(l0 - top), jnp.exp2(l1 - top), jnp.exp2(l2 - top)
    o = (w0 * acc_ref[0] + w1 * acc_ref[1] + w2 * acc_ref[2]) / (w0 + w1 + w2)
    o_ref[...] = o.astype(BF16)


def _dilated_attention(q0, q1, q2, k0, v0, k1, v1, k2, v2, *, n_b, seq):
    t = q0.shape[0]
    pairs = D_MODEL // LANES
    nat = pl.BlockSpec((seq, LANES), lambda b, p: (b, p))

    def res(dil):
        return pl.BlockSpec((1, dil, seq // dil, LANES), lambda b, p: (b, 0, 0, p))

    d1, d2 = B_DILATIONS[1], B_DILATIONS[2]
    return pl.pallas_call(
        functools.partial(_dil_attn_kernel, seq=seq),
        grid=(n_b, pairs),
        in_specs=[nat, res(d1), res(d2), nat, nat, res(d1), res(d1), res(d2), res(d2)],
        out_specs=nat,
        out_shape=jax.ShapeDtypeStruct((t, D_MODEL), BF16),
        scratch_shapes=[pltpu.VMEM((3, seq, LANES), F32), pltpu.VMEM((3, seq, LANES), F32)],
        compiler_params=_params(("parallel", "parallel")),
        name="dilated_attention",
    )(q0, q1, q2, k0, v0, k1, v1, k2, v2)


def _mix_out_kernel(o_ref, wo_ref, x_ref, gate_ref, gain_ref, shift_ref, scale_ref, rw_ref, rb_ref,
                    xn_ref, h_ref, idx_ref, wgt_ref, rank_ref, cnt_ref, *, tm):
    y = jnp.dot(o_ref[...], wo_ref[...], preferred_element_type=F32)
    xn = x_ref[...] + gate_ref[0] * y
    xn_ref[...] = xn
    h = _modulated_norm(xn, gain_ref[...], shift_ref[0], scale_ref[0])

    h_hi = h.astype(BF16)
    h_ref[...] = h_hi
    h_lo = (h - h_hi.astype(F32)).astype(BF16)
    rw = rw_ref[...]
    rw_hi = rw.astype(BF16)
    rw_lo = (rw - rw_hi.astype(F32)).astype(BF16)
    nt = (((1,), (1,)), ((), ()))
    logits = (lax.dot_general(rw_hi, h_hi, nt, preferred_element_type=F32)
              + lax.dot_general(rw_lo, h_hi, nt, preferred_element_type=F32)
              + lax.dot_general(rw_hi, h_lo, nt, preferred_element_type=F32)) + rb_ref[...]

    e_iota = lax.broadcasted_iota(jnp.int32, logits.shape, 0)
    work = logits
    sels, tops, idxs = [], [], []
    for _ in range(TOP_K):
        mk = jnp.max(work, axis=0, keepdims=True)
        ik = jnp.min(jnp.where(work == mk, e_iota, N_EXPERTS), axis=0, keepdims=True)
        sel = e_iota == ik
        work = jnp.where(sel, -jnp.inf, work)
        sels.append(sel)
        tops.append(mk)
        idxs.append(ik)
    exps = [jnp.exp(m - tops[0]) for m in tops]
    denom = exps[0] + exps[1] + exps[2] + exps[3]
    for k in range(TOP_K):
        idx_ref[k:k + 1, :] = idxs[k]
        wgt_ref[k:k + 1, :] = exps[k] / denom

    chosen = jnp.zeros(logits.shape, F32)
    for sel in sels:
        chosen = chosen + jnp.where(sel, 1.0, 0.0)
    before = (lax.broadcasted_iota(jnp.int32, (tm, tm), 0)
              < lax.broadcasted_iota(jnp.int32, (tm, tm), 1))
    upper = jnp.where(before, 1.0, 0.0).astype(BF16)
    prefix = jnp.dot(chosen.astype(BF16), upper, preferred_element_type=F32)
    for k in range(TOP_K):
        rank_ref[k:k + 1, :] = jnp.sum(jnp.where(sels[k], prefix, 0.0), axis=0,
                                       keepdims=True).astype(jnp.int32)
    counts = jnp.sum(chosen, axis=1, keepdims=True).astype(jnp.int32)
    cnt_ref[0] = jnp.broadcast_to(counts, cnt_ref.shape[1:])


def _mix_out_and_route(o, wo, layer, x, gate, gain, shift, scale, router_w, router_b, *, n_b, seq):
    t, d = x.shape
    tm = ROUTE_TILE
    n_s = seq // tm
    per_batch = pl.BlockSpec((1, 1, d), lambda i: (i // n_s, 0, 0))
    row = pl.BlockSpec((tm, d), lambda i: (i, 0))
    sel = pl.BlockSpec((TOP_K, tm), lambda i: (0, i))
    return pl.pallas_call(
        functools.partial(_mix_out_kernel, tm=tm),
        grid=(t // tm,),
        in_specs=[
            row,
            pl.BlockSpec((None,) + wo.shape[1:], lambda i: (layer, 0, 0)),
            row, per_batch,
            pl.BlockSpec((1, d), lambda i: (0, 0)),
            per_batch, per_batch,
            pl.BlockSpec((N_EXPERTS, d), lambda i: (0, 0)),
            pl.BlockSpec((N_EXPERTS, 1), lambda i: (0, 0)),
        ],
        out_specs=[row, row, sel, sel, sel,
                   pl.BlockSpec((1, N_EXPERTS, LANES), lambda i: (i, 0, 0))],
        out_shape=[
            jax.ShapeDtypeStruct((t, d), F32),
            jax.ShapeDtypeStruct((t, d), BF16),
            jax.ShapeDtypeStruct((TOP_K, t), jnp.int32),
            jax.ShapeDtypeStruct((TOP_K, t), F32),
            jax.ShapeDtypeStruct((TOP_K, t), jnp.int32),
            jax.ShapeDtypeStruct((t // tm, N_EXPERTS, LANES), jnp.int32),
        ],
        compiler_params=_params(("parallel",)),
        name="mix_out_route",
    )(o, wo, x, gate, gain.reshape(1, d), shift, scale, router_w.T, router_b.reshape(N_EXPERTS, 1))


HIGH_HALF = 0xFFFF0000


def _pack_bf16_pairs(x):
    c = x.shape[1] // 2
    lo = lax.bitcast_convert_type(x[:, :c], jnp.uint32)
    hi = lax.bitcast_convert_type(x[:, c:], jnp.uint32)
    return (lo >> 16) | (hi & jnp.uint32(HIGH_HALF))


def _unpack_bf16_pairs(w):
    lo = lax.bitcast_convert_type(w << 16, F32).astype(BF16)
    hi = lax.bitcast_convert_type(w & jnp.uint32(HIGH_HALF), F32).astype(BF16)
    return lo, hi


def _copy_tables(grp, toff, tstart):
    experts = jnp.arange(N_EXPERTS, dtype=jnp.int32)

    def words(count, first_row, rows, width):
        end = jnp.cumsum(count, axis=1)
        j = jnp.arange(width, dtype=jnp.int32)
        e_of = jnp.sum((end[:, None, :] <= j[None, :, None]).astype(jnp.int32), axis=2)
        onehot = e_of[:, :, None] == experts[None, None, :]

        def pick(a):
            return jnp.sum(jnp.where(onehot, a[:, None, :], 0), axis=2)

        row = pick(first_row) + (j[None, :] - pick(end - count)) * rows
        word = ((pick(tstart) + row) << ROW_BITS) | (pick(toff) + row)
        return word.reshape(-1).astype(jnp.int32), end[:, -1]

    n_big = grp // BIG_COPY
    big, total_big = words(n_big, jnp.zeros_like(grp), BIG_COPY, MAX_BIG)
    small, total_small = words((grp - n_big * BIG_COPY) // GROUP_ALIGN, n_big * BIG_COPY,
                               GROUP_ALIGN, MAX_SMALL)
    counts = jnp.stack([total_big, total_small, jnp.sum(grp, axis=1)], axis=1)
    return counts.reshape(-1).astype(jnp.int32), big, small


def _tile_copies(tile, cnt_ref, big_ref, small_ref, make_copy, wait):
    if wait:
        total = cnt_ref[3 * tile + 2]

        def wait_rows(rows):
            def body(j, carry):
                make_copy(0, 0, rows).wait()
                return carry
            return body

        lax.fori_loop(0, total // BIG_COPY, wait_rows(BIG_COPY), 0)
        lax.fori_loop(0, (total % BIG_COPY) // GROUP_ALIGN, wait_rows(GROUP_ALIGN), 0)
        return

    def start_rows(table_ref, width, rows):
        def body(j, carry):
            word = table_ref[tile * width + j]
            make_copy(pl.multiple_of(word & ((1 << ROW_BITS) - 1), GROUP_ALIGN),
                      pl.multiple_of(word >> ROW_BITS, GROUP_ALIGN), rows).start()
            return carry
        return body

    lax.fori_loop(0, cnt_ref[3 * tile], start_rows(big_ref, MAX_BIG, BIG_COPY), 0)
    lax.fori_loop(0, cnt_ref[3 * tile + 1], start_rows(small_ref, MAX_SMALL, GROUP_ALIGN), 0)


def _dispatch_kernel(cnt_ref, big_ref, small_ref, fill_row_ref, fill_on_ref, nvalid_ref,
                     h_ref, idx_ref, rank_ref, wgt_ref, off_ref, xs_ref, loc_ref, z_ref, zero_ref,
                     sems, *, n_blocks):
    i = pl.program_id(0)
    last = pl.num_programs(0) - 1
    slot = i % 2
    tm = h_ref.shape[0]

    def copies(tile, buf, wait):
        def make_copy(local, glob, n):
            return pltpu.make_async_copy(z_ref.at[buf, pl.ds(local, n)],
                                         xs_ref.at[pl.ds(glob, n)], sems.at[buf])

        _tile_copies(tile, cnt_ref, big_ref, small_ref, make_copy, wait)

    @pl.when(i == 0)
    def _():
        zero_ref[...] = jnp.zeros_like(zero_ref)

        def fill(row):
            return pltpu.make_async_copy(
                zero_ref, xs_ref.at[pl.ds(pl.multiple_of(row, EXPERT_ROWS), EXPERT_ROWS)],
                sems.at[2])

        def fills(act):
            def last_block(e, carry):
                @pl.when(fill_on_ref[e] == 1)
                def _():
                    act(fill(fill_row_ref[e]))
                return carry

            lax.fori_loop(0, N_EXPERTS, last_block, 0)

            def tail_block(b, carry):
                act(fill(b * EXPERT_ROWS))
                return carry

            lax.fori_loop(nvalid_ref[0], n_blocks, tail_block, 0)

        fills(lambda cp: cp.start())
        fills(lambda cp: cp.wait())

    e_iota = lax.broadcasted_iota(jnp.int32, (N_EXPERTS, tm), 0)
    off = off_ref[0][:, 0:1]
    locs = []
    for k in range(TOP_K):
        sel = e_iota == idx_ref[k:k + 1, :]
        loc = jnp.sum(jnp.where(sel, off, 0), axis=0, keepdims=True) + rank_ref[k:k + 1, :]
        loc_ref[k:k + 1, :] = loc
        locs.append(loc)

    @pl.when(i >= 2)
    def _():
        copies(i - 2, slot, wait=True)

    h = h_ref[...]
    half = h.shape[1] // 2
    rows = 256
    for c in range(COMPACT_ROWS // rows):
        r_iota = lax.broadcasted_iota(jnp.int32, (rows, tm), 0) + c * rows
        gate = jnp.zeros((rows, tm), F32)
        for k, loc in enumerate(locs):
            gate = jnp.where(r_iota == loc, wgt_ref[k:k + 1, :], gate)
        hit = jnp.where(gate != 0.0, 1.0, 0.0)
        z_ref[slot, c * rows:(c + 1) * rows, :half] = _pack_bf16_pairs(
            jnp.dot(hit.astype(BF16), h, preferred_element_type=F32))
        row_gate = jnp.sum(gate, axis=1, keepdims=True)
        z_ref[slot, c * rows:(c + 1) * rows, half:] = lax.bitcast_convert_type(
            jnp.broadcast_to(row_gate, (rows, LANES)), jnp.uint32)

    copies(i, slot, wait=False)

    @pl.when(i == last)
    def _():
        @pl.when(i >= 1)
        def _():
            copies(i - 1, 1 - slot, wait=True)

        copies(i, slot, wait=True)


def _dispatch(h, idx, rank, wgt, off_b, tables, fills, n_rows):
    t, d = h.shape
    tm = ROUTE_TILE
    width = d // 2 + LANES
    sel = pl.BlockSpec((TOP_K, tm), lambda i, *_: (0, i))
    grid_spec = pltpu.PrefetchScalarGridSpec(
        num_scalar_prefetch=6,
        grid=(t // tm,),
        in_specs=[
            pl.BlockSpec((tm, d), lambda i, *_: (i, 0)),
            sel, sel, sel,
            pl.BlockSpec((1, N_EXPERTS, LANES), lambda i, *_: (i, 0, 0)),
        ],
        out_specs=[pl.BlockSpec(memory_space=pl.ANY), sel],
        scratch_shapes=[pltpu.VMEM((2, COMPACT_ROWS, width), jnp.uint32),
                        pltpu.VMEM((EXPERT_ROWS, width), jnp.uint32),
                        pltpu.SemaphoreType.DMA((3,))],
    )
    return pl.pallas_call(
        functools.partial(_dispatch_kernel, n_blocks=n_rows // EXPERT_ROWS),
        grid_spec=grid_spec,
        out_shape=[jax.ShapeDtypeStruct((n_rows, width), jnp.uint32),
                   jax.ShapeDtypeStruct((TOP_K, t), jnp.int32)],
        compiler_params=_params(("arbitrary",)),
        name="moe_dispatch",
    )(*tables, *fills, h, idx, rank, wgt, off_b)


def _expert_kernel(be_ref, first_ref, live_ref, nvalid_ref, xs_ref, wgu_ref, bgu_ref, wd_ref,
                   bd_ref, ys_ref, wgu_bf, wd_bf):
    del be_ref, nvalid_ref
    b = pl.program_id(0)
    bm = xs_ref.shape[0]

    @pl.when(first_ref[b] == 1)
    def _():
        wgu_bf[...] = wgu_ref[0].astype(BF16)
        wd_bf[...] = wd_ref[0].astype(BF16)

    def ffn(rows):
        d_ff = wd_bf.shape[0]
        half = wd_bf.shape[1] // 2
        x = jnp.concatenate(_unpack_bf16_pairs(xs_ref[:rows, :half]), axis=1)
        row_gate = lax.bitcast_convert_type(xs_ref[:rows, half:], F32)
        gu = jnp.dot(x, wgu_bf[...], preferred_element_type=F32) + bgu_ref[0]
        gate = jnp.minimum(gu[:, :d_ff], SWIGLU_LIMIT)
        up = jnp.clip(gu[:, d_ff:], -SWIGLU_LIMIT, SWIGLU_LIMIT)
        glu = gate * jax.nn.sigmoid(SWIGLU_ALPHA * gate)
        act = ((up + 1.0) * glu).astype(BF16)
        y = jnp.dot(act, wd_bf[...], preferred_element_type=F32) + bd_ref[0]
        y = y * jnp.concatenate([row_gate] * (y.shape[1] // LANES), axis=1)
        ys_ref[:rows, :] = _pack_bf16_pairs(y.astype(BF16).astype(F32))

    for parts in range(bm // EXPERT_PART + 1):
        @pl.when(live_ref[b] == parts)
        def _():
            rows = parts * EXPERT_PART
            if rows > 0:
                ffn(rows)
            if rows < bm:
                ys_ref[rows:, :] = jnp.zeros((bm - rows, ys_ref.shape[1]), ys_ref.dtype)


def _experts(xs, block_expert, block_first, block_live, n_valid, layer, w_gu, b_gu, w_d, b_d):
    n_rows, in_width = xs.shape
    bm = EXPERT_ROWS
    d, d_ff = w_d.shape[3], w_d.shape[2]
    half = d // 2
    n_l = w_gu.shape[0]
    grid_spec = pltpu.PrefetchScalarGridSpec(
        num_scalar_prefetch=4,
        grid=(n_rows // bm,),
        in_specs=[
            pl.BlockSpec((bm, in_width), lambda b, be, fi, lv, nv: (jnp.minimum(b, nv[0] - 1), 0)),
            pl.BlockSpec((None, 1, d, 2 * d_ff), lambda b, be, *_: (layer, be[b], 0, 0)),
            pl.BlockSpec((None, 1, 1, 2 * d_ff), lambda b, be, *_: (layer, be[b], 0, 0)),
            pl.BlockSpec((None, 1, d_ff, d), lambda b, be, *_: (layer, be[b], 0, 0)),
            pl.BlockSpec((None, 1, 1, d), lambda b, be, *_: (layer, be[b], 0, 0)),
        ],
        out_specs=pl.BlockSpec((bm, half), lambda b, *_: (b, 0)),
        scratch_shapes=[pltpu.VMEM((d, 2 * d_ff), BF16), pltpu.VMEM((d_ff, d), BF16)],
    )
    return pl.pallas_call(
        _expert_kernel,
        grid_spec=grid_spec,
        out_shape=jax.ShapeDtypeStruct((n_rows, half), jnp.uint32),
        compiler_params=_params(("arbitrary",)),
        name="moe_experts",
    )(block_expert, block_first, block_live, n_valid, xs, w_gu,
      b_gu.reshape(n_l, N_EXPERTS, 1, 2 * d_ff), w_d, b_d.reshape(n_l, N_EXPERTS, 1, d))


def _combine_kernel(cnt_ref, big_ref, small_ref, loc_ref, ys_ref, x_ref, gate_ref,
                    *rest, final):
    fin_refs, (o_ref, y_ref, sems) = rest[:-3], rest[-3:]
    i = pl.program_id(0)
    slot = i % 2
    tm = x_ref.shape[0]

    def copies(tile, buf, wait):
        def make_copy(local, glob, n):
            return pltpu.make_async_copy(ys_ref.at[pl.ds(glob, n)],
                                         y_ref.at[buf, pl.ds(local, n)], sems.at[buf])

        _tile_copies(tile, cnt_ref, big_ref, small_ref, make_copy, wait)

    @pl.when(i == 0)
    def _():
        y_ref[...] = jnp.zeros_like(y_ref)
        copies(0, 0, wait=False)

    copies(i, slot, wait=True)

    @pl.when(i + 1 < pl.num_programs(0))
    def _():
        copies(i + 1, 1 - slot, wait=False)

    rows = COMBINE_CHUNK
    half = y_ref.shape[2]
    f_lo = jnp.zeros((tm, half), F32)
    f_hi = jnp.zeros((tm, half), F32)
    for c in range(COMPACT_ROWS // rows):
        c_iota = lax.broadcasted_iota(jnp.int32, (tm, rows), 1) + c * rows
        q = jnp.zeros((tm, rows), F32)
        for k in range(TOP_K):
            q = jnp.where(c_iota == loc_ref[:, k:k + 1], 1.0, q)
        q = q.astype(BF16)
        y_lo, y_hi = _unpack_bf16_pairs(y_ref[slot, c * rows:(c + 1) * rows, :])
        f_lo = f_lo + jnp.dot(q, y_lo, preferred_element_type=F32)
        f_hi = f_hi + jnp.dot(q, y_hi, preferred_element_type=F32)
    out = x_ref[...] + gate_ref[0] * jnp.concatenate([f_lo, f_hi], axis=1)
    if final:
        gain_ref, shift_ref, scale_ref = fin_refs
        out = _modulated_norm(out, gain_ref[...], shift_ref[0], scale_ref[0])
    o_ref[...] = out


def _combine(loc_tk, ys, x, gate, tables, final_mod, *, seq):
    t, d = x.shape
    tm = ROUTE_TILE
    n_s = seq // tm
    sel = pl.BlockSpec((tm, TOP_K), lambda i, *_: (i, 0))
    per_batch = pl.BlockSpec((1, 1, d), lambda i, *_: (i // n_s, 0, 0))
    in_specs = [sel, pl.BlockSpec(memory_space=pl.ANY),
                pl.BlockSpec((tm, d), lambda i, *_: (i, 0)), per_batch]
    extra = ()
    if final_mod is not None:
        gain, shift, scale = final_mod
        extra = (gain.reshape(1, d), shift, scale)
        in_specs += [pl.BlockSpec((1, d), lambda i, *_: (0, 0)), per_batch, per_batch]
    grid_spec = pltpu.PrefetchScalarGridSpec(
        num_scalar_prefetch=3,
        grid=(t // tm,),
        in_specs=in_specs,
        out_specs=pl.BlockSpec((tm, d), lambda i, *_: (i, 0)),
        scratch_shapes=[pltpu.VMEM((2, COMPACT_ROWS, d // 2), jnp.uint32),
                        pltpu.SemaphoreType.DMA((2,))],
    )
    return pl.pallas_call(
        functools.partial(_combine_kernel, final=final_mod is not None),
        grid_spec=grid_spec,
        out_shape=jax.ShapeDtypeStruct((t, d), F32),
        compiler_params=_params(("arbitrary",)),
        name="moe_combine",
    )(*tables, loc_tk, ys, x, gate, *extra)


def _moe(h, idx, wgt, rank, tile_cnt, x, gate, layer, w_gu, b_gu, w_d, b_d, final_mod, *, seq):
    t = h.shape[0]
    bm = EXPERT_ROWS
    n_tiles = t // ROUTE_TILE
    n_rows = t * TOP_K + n_tiles * N_EXPERTS * GROUP_ALIGN + N_EXPERTS * bm
    n_blocks = n_rows // bm
    cnt = tile_cnt[:, :, 0]
    grp = (cnt + GROUP_ALIGN - 1) // GROUP_ALIGN * GROUP_ALIGN
    expert_rows = jnp.sum(grp, axis=0)
    padded = (expert_rows + bm - 1) // bm * bm
    pad_end = jnp.cumsum(padded)
    tstart = (pad_end - padded)[None, :] + jnp.cumsum(grp, axis=0) - grp
    toff = jnp.cumsum(grp, axis=1) - grp
    tables = _copy_tables(grp, toff, tstart)
    off_b = jnp.broadcast_to(toff[:, :, None], (n_tiles, N_EXPERTS, LANES)).astype(jnp.int32)
    block_row = jnp.arange(n_blocks, dtype=jnp.int32) * bm
    block_expert = jnp.minimum(
        jnp.sum((pad_end[None, :] <= block_row[:, None]).astype(jnp.int32), axis=1),
        N_EXPERTS - 1).astype(jnp.int32)
    block_first = jnp.concatenate(
        [jnp.ones((1,), jnp.int32), (block_expert[1:] != block_expert[:-1]).astype(jnp.int32)])
    n_valid = (pad_end[-1:] // bm).astype(jnp.int32)
    in_expert = block_expert[:, None] == jnp.arange(N_EXPERTS, dtype=jnp.int32)[None, :]
    rows_left = jnp.sum(jnp.where(in_expert, (pad_end - padded + expert_rows)[None, :], 0),
                        axis=1) - block_row
    block_live = jnp.where(block_row >= pad_end[-1], 0,
                           jnp.clip((rows_left + EXPERT_PART - 1) // EXPERT_PART, 1,
                                    bm // EXPERT_PART)).astype(jnp.int32)
    fills = (jnp.maximum(pad_end - bm, 0).astype(jnp.int32), (padded > 0).astype(jnp.int32),
             n_valid)
    xs, loc = _dispatch(h, idx, rank, wgt, off_b, tables, fills, n_rows)
    ys = _experts(xs, block_expert, block_first, block_live, n_valid, layer, w_gu, b_gu, w_d, b_d)
    return _combine(loc.T, ys, x, gate, tables, final_mod, seq=seq)


def kernel(x, c, mod_w, mod_b, mix_norm, ffn_norm, a_wqkv, a_wo, a_lambda, a_subln, kv_norm, kv_mod_w, kv_mod_b, kv_w, b_wq, b_wo, router_w, router_b, exp_w_gate_up, exp_b_gate_up, exp_w_down, exp_b_down, final_norm, final_mod_w, final_mod_b):
    n_b, seq, d = x.shape
    t = n_b * seq
    sizes = dict(n_b=n_b, seq=seq)
    rope_tabs = _rope_tables(seq)
    q_scale = HEAD_DIM ** -0.5 * math.log2(math.e)

    def per_batch(v):
        return v.reshape(n_b, 1, d)

    mod = _adaln_vectors(c, mod_w, mod_b)
    kv_mod = _adaln_vectors(c, kv_mod_w[None], kv_mod_b[None])[0]
    fin_mod = _adaln_vectors(c, final_mod_w[None], final_mod_b[None])[0]

    a_wqkv, a_wo, b_wq, b_wo = (w.astype(BF16) for w in (a_wqkv, a_wo, b_wq, b_wo))
    kv_w = kv_w[None].astype(BF16)

    xt = x.reshape(t, d)
    shared = None
    for layer in range(DEPTH):
        sh1, sc1, g1, sh2, sc2, g2 = (per_batch(mod[layer, :, j * d:(j + 1) * d]) for j in range(6))
        if layer < N_A_LAYERS:
            lambda_init = 0.8 - 0.6 * math.exp(-0.3 * layer)
            segs = ((0, True, q_scale, (1,)), (1, True, 1.0, (1,)), (2, False, 1.0, (1,)))
            q, k, v = _norm_project(xt, mix_norm[layer], sh1, sc1, a_wqkv, layer,
                                    rope_tabs, segs, name="a_qkv_proj", **sizes)
            o = _diff_attention(q, k, v, a_lambda[layer], a_subln[layer], lambda_init, **sizes)
            wo, wo_layer = a_wo, layer
        else:
            j = layer - N_A_LAYERS
            segs = tuple((g, True, q_scale, (B_DILATIONS[g],)) for g in range(3))
            q0, q1, q2 = _norm_project(xt, mix_norm[layer], sh1, sc1, b_wq, j,
                                       rope_tabs, segs, name="b_q_proj", **sizes)
            o = _dilated_attention(q0, q1, q2, *shared, **sizes)
            wo, wo_layer = b_wo, j
        xt, h, idx, wgt, rank, tile_cnt = _mix_out_and_route(
            o, wo, wo_layer, xt, g1, ffn_norm[layer], sh2, sc2, router_w[layer],
            router_b[layer], **sizes)
        final_mod = None
        if layer == DEPTH - 1:
            final_mod = (final_norm, per_batch(fin_mod[:, :d]), per_batch(fin_mod[:, d:]))
        xt = _moe(h, idx, wgt, rank, tile_cnt, xt, g2, layer, exp_w_gate_up, exp_b_gate_up,
                  exp_w_down, exp_b_down, final_mod, seq=seq)
        if layer == N_A_LAYERS - 1:
            segs = ((0, True, 1.0, B_DILATIONS), (1, False, 1.0, B_DILATIONS))
            k0, k1, k2, v0, v1, v2 = _norm_project(
                xt, kv_norm, per_batch(kv_mod[:, :d]), per_batch(kv_mod[:, d:]),
                kv_w, 0, rope_tabs, segs, name="shared_kv_proj", **sizes)
            shared = (k0, v0, k1, v1, k2, v2)
    return xt.reshape(n_b, seq, d)
```

```python
import functools
import math

import jax
import jax.numpy as jnp
from jax import lax
from jax.experimental import pallas as pl
from jax.experimental.pallas import tpu as pltpu

F32 = jnp.float32
BF16 = jnp.bfloat16

D_MODEL = 1024
HEAD_DIM = 64
A_HEADS = 8
B_KV_HEADS = 16
B_DILATIONS = (1, 4, 16)
WINDOW_STEPS = 128
ROPE_THETA = 500000.0
ROPE_DIM = HEAD_DIM // 4
N_EXPERTS = 32
TOP_K = 4
SWIGLU_LIMIT = 7.0
SWIGLU_ALPHA = 1.702
NORM_EPS = 1e-5
N_A_LAYERS = 2
DEPTH = 4

LANES = 128
BF16_SUBLANES = 16
EXPERT_ROWS = 1024
EXPERT_PART = 256
ROUTE_TILE = 512
GROUP_ALIGN = 8
COMPACT_ROWS = ROUTE_TILE * TOP_K + N_EXPERTS * GROUP_ALIGN
COMBINE_CHUNK = COMPACT_ROWS
BIG_COPY = 32
MAX_BIG = COMPACT_ROWS // BIG_COPY
MAX_SMALL = N_EXPERTS * (BIG_COPY // GROUP_ALIGN - 1)
ROW_BITS = 12
NEG_BIG = -1e30
V7X_VMEM_BYTES = 64 * 1024 * 1024
VMEM_LIMIT = V7X_VMEM_BYTES // 8 * 7


def _params(sem, vmem=VMEM_LIMIT):
    return pltpu.CompilerParams(dimension_semantics=sem, vmem_limit_bytes=vmem)


def _adaln_kernel(c_ref, w_ref, b_ref, o_ref):
    c = c_ref[...]
    c_act = (c * jax.nn.sigmoid(c)).astype(BF16)
    o_ref[0] = jnp.dot(c_act, w_ref[0].astype(BF16), preferred_element_type=F32) + b_ref[0]


def _adaln_vectors(c, w, b):
    n_l, d, n = w.shape
    n_b = c.shape[0]
    tn = 1024
    return pl.pallas_call(
        _adaln_kernel,
        grid=(n_l, n // tn),
        in_specs=[
            pl.BlockSpec((n_b, d), lambda l, j: (0, 0)),
            pl.BlockSpec((1, d, tn), lambda l, j: (l, 0, j)),
            pl.BlockSpec((1, 1, tn), lambda l, j: (l, 0, j)),
        ],
        out_specs=pl.BlockSpec((1, n_b, tn), lambda l, j: (l, 0, j)),
        out_shape=jax.ShapeDtypeStruct((n_l, n_b, n), F32),
        compiler_params=_params(("parallel", "parallel")),
        name="adaln_vectors",
    )(c, w, b.reshape(n_l, 1, n))


def _modulated_norm(x, gain, shift, scale):
    y = x * lax.rsqrt(jnp.mean(x * x, axis=-1, keepdims=True) + NORM_EPS)
    return y * gain * (1.0 + scale) + shift


def _rope_tables(seq):
    inv = ROPE_THETA ** (-jnp.arange(0, ROPE_DIM, 2, dtype=F32) / ROPE_DIM)
    ang = jnp.arange(seq, dtype=F32)[:, None] * inv[None, :]
    cos, sin = jnp.cos(ang), jnp.sin(ang)
    half = ROPE_DIM // 2
    rest = HEAD_DIM - ROPE_DIM
    zeros = jnp.zeros((seq, half), F32)
    c_tab = jnp.concatenate([cos, cos, jnp.ones((seq, rest), F32)], axis=1)
    s1_tab = jnp.concatenate([zeros, sin, jnp.zeros((seq, rest), F32)], axis=1)
    s2_tab = jnp.concatenate([-sin, zeros, jnp.zeros((seq, rest), F32)], axis=1)
    rep = LANES // HEAD_DIM
    return tuple(jnp.tile(t, (1, rep)) for t in (c_tab, s1_tab, s2_tab))


def _apply_rope(y, c_tab, s1_tab, s2_tab):
    parts = []
    for j in range(y.shape[1] // LANES):
        yj = y[:, j * LANES:(j + 1) * LANES]
        parts.append(yj * c_tab + pltpu.roll(yj, ROPE_DIM // 2, 1) * s1_tab
                     + pltpu.roll(yj, LANES - ROPE_DIM // 2, 1) * s2_tab)
    return jnp.concatenate(parts, axis=1)


def _proj_kernel(x_ref, gain_ref, shift_ref, scale_ref, w_ref, rc_ref, rs1_ref, rs2_ref, *rest,
                 segs, tm):
    n_out = sum(len(s[3]) for s in segs)
    out_refs, scr_ref = rest[:n_out], rest[n_out]
    h = _modulated_norm(x_ref[...], gain_ref[...], shift_ref[0], scale_ref[0]).astype(BF16)
    oi = 0
    for chunk, rope, mult, dils in segs:
        y = jnp.dot(h, w_ref[:, chunk * D_MODEL:(chunk + 1) * D_MODEL], preferred_element_type=F32)
        if rope:
            y = _apply_rope(y, rc_ref[...], rs1_ref[...], rs2_ref[...])
        if mult != 1.0:
            y = y * mult
        for dil in dils:
            o_ref = out_refs[oi]
            oi += 1
            if dil == 1:
                o_ref[...] = y.astype(BF16)
            else:
                for j in range(D_MODEL // LANES):
                    scr_ref[j] = y[:, j * LANES:(j + 1) * LANES]
                for r in range(dil):
                    for j in range(D_MODEL // LANES):
                        o_ref[0, r, :, j * LANES:(j + 1) * LANES] = (
                            scr_ref[j, pl.ds(r, tm // dil, stride=dil), :].astype(BF16))


def _norm_project(x, gain, shift, scale, w, layer, rope_tabs, segs, *, n_b, seq, name):
    t, d = x.shape
    tm = 512
    n_s = seq // tm
    out_shapes, out_specs = [], []
    for _, _, _, dils in segs:
        for dil in dils:
            if dil == 1:
                out_shapes.append(jax.ShapeDtypeStruct((t, D_MODEL), BF16))
                out_specs.append(pl.BlockSpec((tm, D_MODEL), lambda i: (i, 0)))
            else:
                out_shapes.append(jax.ShapeDtypeStruct((n_b, dil, seq // dil, D_MODEL), BF16))
                out_specs.append(pl.BlockSpec((1, dil, tm // dil, D_MODEL),
                                              lambda i: (i // n_s, 0, i % n_s, 0)))
    per_batch = pl.BlockSpec((1, 1, d), lambda i: (i // n_s, 0, 0))
    rope_spec = pl.BlockSpec((tm, LANES), lambda i: (i % n_s, 0))
    return pl.pallas_call(
        functools.partial(_proj_kernel, segs=segs, tm=tm),
        grid=(t // tm,),
        in_specs=[
            pl.BlockSpec((tm, d), lambda i: (i, 0)),
            pl.BlockSpec((1, d), lambda i: (0, 0)),
            per_batch, per_batch,
            pl.BlockSpec((None,) + w.shape[1:], lambda i: (layer, 0, 0)),
            rope_spec, rope_spec, rope_spec,
        ],
        out_specs=out_specs,
        out_shape=out_shapes,
        scratch_shapes=[pltpu.VMEM((D_MODEL // LANES, tm, LANES), F32)],
        compiler_params=_params(("parallel",)),
        name=name,
    )(x, gain.reshape(1, d), shift, scale, w, *rope_tabs)


def _diff_attn_kernel(lam_ref, subln_ref, q_ref, k_ref, v_ref, o_ref, *, tq, lambda_init):
    lf = lam_ref[...]
    lam = (jnp.exp(jnp.sum(lf[0:1] * lf[1:2], keepdims=True))
           - jnp.exp(jnp.sum(lf[2:3] * lf[3:4], keepdims=True)) + lambda_init)
    lane_lo = lax.broadcasted_iota(jnp.int32, (tq, 2 * HEAD_DIM), 1) < HEAD_DIM
    causal = (lax.broadcasted_iota(jnp.int32, (tq, tq), 0)
              <= lax.broadcasted_iota(jnp.int32, (tq, tq), 1))

    def scores(kb, qm):
        return lax.dot_general(kb, qm, (((1,), (1,)), ((), ())), preferred_element_type=F32)

    def online(s, m, acc, vbt):
        m_new = jnp.maximum(m, jnp.max(s, axis=0, keepdims=True))
        p = jnp.exp2(s - m_new)
        alpha = jnp.exp2(m - m_new)
        acc = alpha * acc + jnp.dot(vbt, p.astype(BF16), preferred_element_type=F32)
        return m_new, acc

    d_v = 2 * HEAD_DIM
    ones_rows = jnp.ones((BF16_SUBLANES, tq), BF16)
    for i in range(q_ref.shape[0] // tq):
        q = q_ref[i * tq:(i + 1) * tq, :]
        zero = jnp.zeros_like(q)
        q1 = jnp.where(lane_lo, q, zero)
        q2 = jnp.where(lane_lo, zero, q)
        m1 = m2 = jnp.full((1, tq), NEG_BIG, F32)
        a1 = a2 = jnp.zeros((d_v + BF16_SUBLANES, tq), F32)
        for j in range(i + 1):
            kb = k_ref[j * tq:(j + 1) * tq, :]
            vbt = jnp.concatenate([v_ref[j * tq:(j + 1) * tq, :].T, ones_rows], axis=0)
            s1 = scores(kb, q1)
            s2 = scores(kb, q2)
            if j == i:
                s1 = jnp.where(causal, s1, NEG_BIG)
                s2 = jnp.where(causal, s2, NEG_BIG)
            m1, a1 = online(s1, m1, a1, vbt)
            m2, a2 = online(s2, m2, a2, vbt)
        o = (a1[:d_v] / a1[d_v:d_v + 1] - lam * (a2[:d_v] / a2[d_v:d_v + 1])).T
        o = o * lax.rsqrt(jnp.mean(o * o, axis=-1, keepdims=True) + NORM_EPS)
        o_ref[i * tq:(i + 1) * tq, :] = (o * subln_ref[...] * (1.0 - lambda_init)).astype(BF16)


def _diff_attention(q, k, v, lam_vecs, subln, lambda_init, *, n_b, seq):
    t = q.shape[0]
    width = 2 * HEAD_DIM
    head_seq = pl.BlockSpec((seq, width), lambda b, h: (b, h))
    return pl.pallas_call(
        functools.partial(_diff_attn_kernel, tq=512, lambda_init=lambda_init),
        grid=(n_b, A_HEADS),
        in_specs=[
            pl.BlockSpec((4, HEAD_DIM), lambda b, h: (0, 0)),
            pl.BlockSpec((1, width), lambda b, h: (0, 0)),
            head_seq, head_seq, head_seq,
        ],
        out_specs=head_seq,
        out_shape=jax.ShapeDtypeStruct((t, D_MODEL), BF16),
        compiler_params=_params(("parallel", "parallel")),
        name="diff_attention",
    )(lam_vecs, subln.reshape(1, width), q, k, v)


def _window_unit(q, kc, vc, bias, lane_lo):
    zero = jnp.zeros_like(q)
    qq = jnp.concatenate([jnp.where(lane_lo, q, zero), jnp.where(lane_lo, zero, q)], axis=0)
    s = lax.dot_general(qq, kc, (((1,), (1,)), ((), ())), preferred_element_type=F32) + bias
    m = jnp.max(s, axis=-1, keepdims=True)
    p = jnp.exp2(s - m).astype(BF16)
    v_aug = jnp.concatenate([vc, jnp.ones(vc.shape, BF16)], axis=1)
    o_aug = jnp.dot(p, v_aug, preferred_element_type=F32)
    n = WINDOW_STEPS
    o = jnp.where(lane_lo, o_aug[:n, :LANES], o_aug[n:, :LANES])
    l = jnp.where(lane_lo, o_aug[:n, LANES:], o_aug[n:, LANES:])
    m_sel = jnp.where(lane_lo, m[:n], m[n:])
    return o / l, m_sel + jnp.log2(l)


def _dil_attn_kernel(q0_ref, q1_ref, q2_ref, k0_ref, v0_ref, k1_ref, v1_ref, k2_ref, v2_ref,
                     o_ref, acc_ref, lse_ref, *, seq):
    n = WINDOW_STEPS
    lane_lo = lax.broadcasted_iota(jnp.int32, (n, LANES), 1) < HEAD_DIM
    qi = lax.broadcasted_iota(jnp.int32, (2 * n, 2 * n), 0) % n
    kj = lax.broadcasted_iota(jnp.int32, (2 * n, 2 * n), 1)
    band = jnp.where((kj >= qi) & (kj <= qi + n), 0.0, NEG_BIG)
    first = jnp.where(lax.broadcasted_iota(jnp.int32, (2 * n, n), 1)
                      <= lax.broadcasted_iota(jnp.int32, (2 * n, n), 0) % n, 0.0, NEG_BIG)

    def unit(g, q_rows, k_rows, v_rows, bias, dst):
        o, lse = _window_unit(q_rows, k_rows, v_rows, bias, lane_lo)
        acc_ref[g, dst, :] = o
        lse_ref[g, dst, :] = lse

    def rows(ref, r, lo, hi):
        return ref[lo:hi, :] if len(ref.shape) == 2 else ref[0, r, lo:hi, :]

    for g, (q_ref, k_ref, v_ref) in enumerate(((q0_ref, k0_ref, v0_ref), (q1_ref, k1_ref, v1_ref),
                                               (q2_ref, k2_ref, v2_ref))):
        dil = B_DILATIONS[g]
        for r in range(dil):
            for blk in range(seq // dil // n):
                dst = pl.ds(r + blk * n * dil, n, stride=dil) if dil > 1 else pl.ds(blk * n, n)
                k_lo = max(blk - 1, 0) * n
                unit(g, rows(q_ref, r, blk * n, (blk + 1) * n), rows(k_ref, r, k_lo, (blk + 1) * n),
                     rows(v_ref, r, k_lo, (blk + 1) * n), first if blk == 0 else band, dst)

    l0, l1, l2 = lse_ref[0], lse_ref[1], lse_ref[2]
    top = jnp.maximum(jnp.maximum(l0, l1), l2)
    w0, w1, w2 = jnp.exp2(l0 - top), jnp.exp2(l1 - top), jnp.exp2(l2 - top)
    o = (w0 * acc_ref[0] + w1 * acc_ref[1] + w2 * acc_ref[2]) / (w0 + w1 + w2)
    o_ref[...] = o.astype(BF16)


def _dilated_attention(q0, q1, q2, k0, v0, k1, v1, k2, v2, *, n_b, seq):
    t = q0.shape[0]
    pairs = D_MODEL // LANES
    nat = pl.BlockSpec((seq, LANES), lambda b, p: (b, p))

    def res(dil):
        return pl.BlockSpec((1, dil, seq // dil, LANES), lambda b, p: (b, 0, 0, p))

    d1, d2 = B_DILATIONS[1], B_DILATIONS[2]
    return pl.pallas_call(
        functools.partial(_dil_attn_kernel, seq=seq),
        grid=(n_b, pairs),
        in_specs=[nat, res(d1), res(d2), nat, nat, res(d1), res(d1), res(d2), res(d2)],
        out_specs=nat,
        out_shape=jax.ShapeDtypeStruct((t, D_MODEL), BF16),
        scratch_shapes=[pltpu.VMEM((3, seq, LANES), F32), pltpu.VMEM((3, seq, LANES), F32)],
        compiler_params=_params(("parallel", "parallel")),
        name="dilated_attention",
    )(q0, q1, q2, k0, v0, k1, v1, k2, v2)


def _mix_out_kernel(o_ref, wo_ref, x_ref, gate_ref, gain_ref, shift_ref, scale_ref, rw_ref, rb_ref,
                    xn_ref, h_ref, idx_ref, wgt_ref, rank_ref, cnt_ref, *, tm):
    y = jnp.dot(o_ref[...], wo_ref[...], preferred_element_type=F32)
    xn = x_ref[...] + gate_ref[0] * y
    xn_ref[...] = xn
    h = _modulated_norm(xn, gain_ref[...], shift_ref[0], scale_ref[0])

    h_hi = h.astype(BF16)
    h_ref[...] = h_hi
    h_lo = (h - h_hi.astype(F32)).astype(BF16)
    rw = rw_ref[...]
    rw_hi = rw.astype(BF16)
    rw_lo = (rw - rw_hi.astype(F32)).astype(BF16)
    nt = (((1,), (1,)), ((), ()))
    logits = (lax.dot_general(rw_hi, h_hi, nt, preferred_element_type=F32)
              + lax.dot_general(rw_lo, h_hi, nt, preferred_element_type=F32)
              + lax.dot_general(rw_hi, h_lo, nt, preferred_element_type=F32)) + rb_ref[...]

    e_iota = lax.broadcasted_iota(jnp.int32, logits.shape, 0)
    work = logits
    sels, tops, idxs = [], [], []
    for _ in range(TOP_K):
        mk = jnp.max(work, axis=0, keepdims=True)
        ik = jnp.min(jnp.where(work == mk, e_iota, N_EXPERTS), axis=0, keepdims=True)
        sel = e_iota == ik
        work = jnp.where(sel, -jnp.inf, work)
        sels.append(sel)
        tops.append(mk)
        idxs.append(ik)
    exps = [jnp.exp(m - tops[0]) for m in tops]
    denom = exps[0] + exps[1] + exps[2] + exps[3]
    for k in range(TOP_K):
        idx_ref[k:k + 1, :] = idxs[k]
        wgt_ref[k:k + 1, :] = exps[k] / denom

    chosen = jnp.zeros(logits.shape, F32)
    for sel in sels:
        chosen = chosen + jnp.where(sel, 1.0, 0.0)
    before = (lax.broadcasted_iota(jnp.int32, (tm, tm), 0)
              < lax.broadcasted_iota(jnp.int32, (tm, tm), 1))
    upper = jnp.where(before, 1.0, 0.0).astype(BF16)
    prefix = jnp.dot(chosen.astype(BF16), upper, preferred_element_type=F32)
    for k in range(TOP_K):
        rank_ref[k:k + 1, :] = jnp.sum(jnp.where(sels[k], prefix, 0.0), axis=0,
                                       keepdims=True).astype(jnp.int32)
    counts = jnp.sum(chosen, axis=1, keepdims=True).astype(jnp.int32)
    cnt_ref[0] = jnp.broadcast_to(counts, cnt_ref.shape[1:])


def _mix_out_and_route(o, wo, layer, x, gate, gain, shift, scale, router_w, router_b, *, n_b, seq):
    t, d = x.shape
    tm = ROUTE_TILE
    n_s = seq // tm
    per_batch = pl.BlockSpec((1, 1, d), lambda i: (i // n_s, 0, 0))
    row = pl.BlockSpec((tm, d), lambda i: (i, 0))
    sel = pl.BlockSpec((TOP_K, tm), lambda i: (0, i))
    return pl.pallas_call(
        functools.partial(_mix_out_kernel, tm=tm),
        grid=(t // tm,),
        in_specs=[
            row,
            pl.BlockSpec((None,) + wo.shape[1:], lambda i: (layer, 0, 0)),
            row, per_batch,
            pl.BlockSpec((1, d), lambda i: (0, 0)),
            per_batch, per_batch,
            pl.BlockSpec((N_EXPERTS, d), lambda i: (0, 0)),
            pl.BlockSpec((N_EXPERTS, 1), lambda i: (0, 0)),
        ],
        out_specs=[row, row, sel, sel, sel,
                   pl.BlockSpec((1, N_EXPERTS, LANES), lambda i: (i, 0, 0))],
        out_shape=[
            jax.ShapeDtypeStruct((t, d), F32),
            jax.ShapeDtypeStruct((t, d), BF16),
            jax.ShapeDtypeStruct((TOP_K, t), jnp.int32),
            jax.ShapeDtypeStruct((TOP_K, t), F32),
            jax.ShapeDtypeStruct((TOP_K, t), jnp.int32),
            jax.ShapeDtypeStruct((t // tm, N_EXPERTS, LANES), jnp.int32),
        ],
        compiler_params=_params(("parallel",)),
        name="mix_out_route",
    )(o, wo, x, gate, gain.reshape(1, d), shift, scale, router_w.T, router_b.reshape(N_EXPERTS, 1))


HIGH_HALF = 0xFFFF0000


def _pack_bf16_pairs(x):
    c = x.shape[1] // 2
    lo = lax.bitcast_convert_type(x[:, :c], jnp.uint32)
    hi = lax.bitcast_convert_type(x[:, c:], jnp.uint32)
    return (lo >> 16) | (hi & jnp.uint32(HIGH_HALF))


def _unpack_bf16_pairs(w):
    lo = lax.bitcast_convert_type(w << 16, F32).astype(BF16)
    hi = lax.bitcast_convert_type(w & jnp.uint32(HIGH_HALF), F32).astype(BF16)
    return lo, hi


def _copy_tables(grp, toff, tstart):
    experts = jnp.arange(N_EXPERTS, dtype=jnp.int32)

    def words(count, first_row, rows, width):
        end = jnp.cumsum(count, axis=1)
        j = jnp.arange(width, dtype=jnp.int32)
        e_of = jnp.sum((end[:, None, :] <= j[None, :, None]).astype(jnp.int32), axis=2)
        onehot = e_of[:, :, None] == experts[None, None, :]

        def pick(a):
            return jnp.sum(jnp.where(onehot, a[:, None, :], 0), axis=2)

        row = pick(first_row) + (j[None, :] - pick(end - count)) * rows
        word = ((pick(tstart) + row) << ROW_BITS) | (pick(toff) + row)
        return word.reshape(-1).astype(jnp.int32), end[:, -1]

    n_big = grp // BIG_COPY
    big, total_big = words(n_big, jnp.zeros_like(grp), BIG_COPY, MAX_BIG)
    small, total_small = words((grp - n_big * BIG_COPY) // GROUP_ALIGN, n_big * BIG_COPY,
                               GROUP_ALIGN, MAX_SMALL)
    counts = jnp.stack([total_big, total_small, jnp.sum(grp, axis=1)], axis=1)
    return counts.reshape(-1).astype(jnp.int32), big, small


def _tile_copies(tile, cnt_ref, big_ref, small_ref, make_copy, wait):
    if wait:
        total = cnt_ref[3 * tile + 2]

        def wait_rows(rows):
            def body(j, carry):
                make_copy(0, 0, rows).wait()
                return carry
            return body

        lax.fori_loop(0, total // BIG_COPY, wait_rows(BIG_COPY), 0)
        lax.fori_loop(0, (total % BIG_COPY) // GROUP_ALIGN, wait_rows(GROUP_ALIGN), 0)
        return

    def start_rows(table_ref, width, rows):
        def body(j, carry):
            word = table_ref[tile * width + j]
            make_copy(pl.multiple_of(word & ((1 << ROW_BITS) - 1), GROUP_ALIGN),
                      pl.multiple_of(word >> ROW_BITS, GROUP_ALIGN), rows).start()
            return carry
        return body

    lax.fori_loop(0, cnt_ref[3 * tile], start_rows(big_ref, MAX_BIG, BIG_COPY), 0)
    lax.fori_loop(0, cnt_ref[3 * tile + 1], start_rows(small_ref, MAX_SMALL, GROUP_ALIGN), 0)


def _dispatch_kernel(cnt_ref, big_ref, small_ref, fill_row_ref, fill_on_ref, nvalid_ref,
                     h_ref, idx_ref, rank_ref, wgt_ref, off_ref, xs_ref, loc_ref, z_ref, zero_ref,
                     sems, *, n_blocks):
    i = pl.program_id(0)
    last = pl.num_programs(0) - 1
    slot = i % 2
    tm = h_ref.shape[0]

    def copies(tile, buf, wait):
        def make_copy(local, glob, n):
            return pltpu.make_async_copy(z_ref.at[buf, pl.ds(local, n)],
                                         xs_ref.at[pl.ds(glob, n)], sems.at[buf])

        _tile_copies(tile, cnt_ref, big_ref, small_ref, make_copy, wait)

    @pl.when(i == 0)
    def _():
        zero_ref[...] = jnp.zeros_like(zero_ref)

        def fill(row):
            return pltpu.make_async_copy(
                zero_ref, xs_ref.at[pl.ds(pl.multiple_of(row, EXPERT_ROWS), EXPERT_ROWS)],
                sems.at[2])

        def fills(act):
            def last_block(e, carry):
                @pl.when(fill_on_ref[e] == 1)
                def _():
                    act(fill(fill_row_ref[e]))
                return carry

            lax.fori_loop(0, N_EXPERTS, last_block, 0)

            def tail_block(b, carry):
                act(fill(b * EXPERT_ROWS))
                return carry

            lax.fori_loop(nvalid_ref[0], n_blocks, tail_block, 0)

        fills(lambda cp: cp.start())
        fills(lambda cp: cp.wait())

    e_iota = lax.broadcasted_iota(jnp.int32, (N_EXPERTS, tm), 0)
    off = off_ref[0][:, 0:1]
    locs = []
    for k in range(TOP_K):
        sel = e_iota == idx_ref[k:k + 1, :]
        loc = jnp.sum(jnp.where(sel, off, 0), axis=0, keepdims=True) + rank_ref[k:k + 1, :]
        loc_ref[k:k + 1, :] = loc
        locs.append(loc)

    @pl.when(i >= 2)
    def _():
        copies(i - 2, slot, wait=True)

    h = h_ref[...]
    half = h.shape[1] // 2
    rows = 256
    for c in range(COMPACT_ROWS // rows):
        r_iota = lax.broadcasted_iota(jnp.int32, (rows, tm), 0) + c * rows
        gate = jnp.zeros((rows, tm), F32)
        for k, loc in enumerate(locs):
            gate = jnp.where(r_iota == loc, wgt_ref[k:k + 1, :], gate)
        hit = jnp.where(gate != 0.0, 1.0, 0.0)
        z_ref[slot, c * rows:(c + 1) * rows, :half] = _pack_bf16_pairs(
            jnp.dot(hit.astype(BF16), h, preferred_element_type=F32))
        row_gate = jnp.sum(gate, axis=1, keepdims=True)
        z_ref[slot, c * rows:(c + 1) * rows, half:] = lax.bitcast_convert_type(
            jnp.broadcast_to(row_gate, (rows, LANES)), jnp.uint32)

    copies(i, slot, wait=False)

    @pl.when(i == last)
    def _():
        @pl.when(i >= 1)
        def _():
            copies(i - 1, 1 - slot, wait=True)

        copies(i, slot, wait=True)


def _dispatch(h, idx, rank, wgt, off_b, tables, fills, n_rows):
    t, d = h.shape
    tm = ROUTE_TILE
    width = d // 2 + LANES
    sel = pl.BlockSpec((TOP_K, tm), lambda i, *_: (0, i))
    grid_spec = pltpu.PrefetchScalarGridSpec(
        num_scalar_prefetch=6,
        grid=(t // tm,),
        in_specs=[
            pl.BlockSpec((tm, d), lambda i, *_: (i, 0)),
            sel, sel, sel,
            pl.BlockSpec((1, N_EXPERTS, LANES), lambda i, *_: (i, 0, 0)),
        ],
        out_specs=[pl.BlockSpec(memory_space=pl.ANY), sel],
        scratch_shapes=[pltpu.VMEM((2, COMPACT_ROWS, width), jnp.uint32),
                        pltpu.VMEM((EXPERT_ROWS, width), jnp.uint32),
                        pltpu.SemaphoreType.DMA((3,))],
    )
    return pl.pallas_call(
        functools.partial(_dispatch_kernel, n_blocks=n_rows // EXPERT_ROWS),
        grid_spec=grid_spec,
        out_shape=[jax.ShapeDtypeStruct((n_rows, width), jnp.uint32),
                   jax.ShapeDtypeStruct((TOP_K, t), jnp.int32)],
        compiler_params=_params(("arbitrary",)),
        name="moe_dispatch",
    )(*tables, *fills, h, idx, rank, wgt, off_b)


def _expert_kernel(be_ref, first_ref, live_ref, src_ref, nvalid_ref, xs_ref, wgu_ref, bgu_ref,
                   wd_ref, bd_ref, ys_ref, wgu_bf, wd_bf):
    del be_ref, src_ref, nvalid_ref
    b = pl.program_id(0)
    bm = xs_ref.shape[0]

    @pl.when(first_ref[b] == 1)
    def _():
        wgu_bf[...] = wgu_ref[0].astype(BF16)
        wd_bf[...] = wd_ref[0].astype(BF16)

    def ffn(rows):
        d_ff = wd_bf.shape[0]
        half = wd_bf.shape[1] // 2
        x = jnp.concatenate(_unpack_bf16_pairs(xs_ref[:rows, :half]), axis=1)
        row_gate = lax.bitcast_convert_type(xs_ref[:rows, half:], F32)
        gu = jnp.dot(x, wgu_bf[...], preferred_element_type=F32) + bgu_ref[0]
        gate = jnp.minimum(gu[:, :d_ff], SWIGLU_LIMIT)
        up = jnp.clip(gu[:, d_ff:], -SWIGLU_LIMIT, SWIGLU_LIMIT)
        glu = gate * jax.nn.sigmoid(SWIGLU_ALPHA * gate)
        act = ((up + 1.0) * glu).astype(BF16)
        y = jnp.dot(act, wd_bf[...], preferred_element_type=F32) + bd_ref[0]
        y = y * jnp.concatenate([row_gate] * (y.shape[1] // LANES), axis=1)
        ys_ref[:rows, :] = _pack_bf16_pairs(y.astype(BF16).astype(F32))

    for parts in range(bm // EXPERT_PART + 1):
        @pl.when(live_ref[b] == parts)
        def _():
            rows = parts * EXPERT_PART
            if rows > 0:
                ffn(rows)
            if rows < bm:
                ys_ref[rows:, :] = jnp.zeros((bm - rows, ys_ref.shape[1]), ys_ref.dtype)


def _experts(xs, block_expert, block_first, block_live, block_src, n_valid, layer,
             w_gu, b_gu, w_d, b_d):
    n_rows, in_width = xs.shape
    bm = EXPERT_ROWS
    d, d_ff = w_d.shape[3], w_d.shape[2]
    half = d // 2
    n_l = w_gu.shape[0]
    grid_spec = pltpu.PrefetchScalarGridSpec(
        num_scalar_prefetch=5,
        grid=(n_rows // bm,),
        in_specs=[
            pl.BlockSpec((bm, in_width),
                         lambda b, be, fi, lv, src, nv: (src[jnp.minimum(b, nv[0] - 1)], 0)),
            pl.BlockSpec((None, 1, d, 2 * d_ff), lambda b, be, *_: (layer, be[b], 0, 0)),
            pl.BlockSpec((None, 1, 1, 2 * d_ff), lambda b, be, *_: (layer, be[b], 0, 0)),
            pl.BlockSpec((None, 1, d_ff, d), lambda b, be, *_: (layer, be[b], 0, 0)),
            pl.BlockSpec((None, 1, 1, d), lambda b, be, *_: (layer, be[b], 0, 0)),
        ],
        out_specs=pl.BlockSpec((bm, half), lambda b, be, fi, lv, src, nv: (src[b], 0)),
        scratch_shapes=[pltpu.VMEM((d, 2 * d_ff), BF16), pltpu.VMEM((d_ff, d), BF16)],
    )
    return pl.pallas_call(
        _expert_kernel,
        grid_spec=grid_spec,
        out_shape=jax.ShapeDtypeStruct((n_rows, half), jnp.uint32),
        compiler_params=_params(("arbitrary",)),
        name="moe_experts",
    )(block_expert, block_first, block_live, block_src, n_valid, xs, w_gu,
      b_gu.reshape(n_l, N_EXPERTS, 1, 2 * d_ff), w_d, b_d.reshape(n_l, N_EXPERTS, 1, d))


def _combine_kernel(cnt_ref, big_ref, small_ref, loc_ref, ys_ref, x_ref, gate_ref,
                    *rest, final):
    fin_refs, (o_ref, y_ref, sems) = rest[:-3], rest[-3:]
    i = pl.program_id(0)
    slot = i % 2
    tm = x_ref.shape[0]

    def copies(tile, buf, wait):
        def make_copy(local, glob, n):
            return pltpu.make_async_copy(ys_ref.at[pl.ds(glob, n)],
                                         y_ref.at[buf, pl.ds(local, n)], sems.at[buf])

        _tile_copies(tile, cnt_ref, big_ref, small_ref, make_copy, wait)

    @pl.when(i == 0)
    def _():
        y_ref[...] = jnp.zeros_like(y_ref)
        copies(0, 0, wait=False)

    copies(i, slot, wait=True)

    @pl.when(i + 1 < pl.num_programs(0))
    def _():
        copies(i + 1, 1 - slot, wait=False)

    rows = COMBINE_CHUNK
    half = y_ref.shape[2]
    f_lo = jnp.zeros((tm, half), F32)
    f_hi = jnp.zeros((tm, half), F32)
    for c in range(COMPACT_ROWS // rows):
        c_iota = lax.broadcasted_iota(jnp.int32, (tm, rows), 1) + c * rows
        q = jnp.zeros((tm, rows), F32)
        for k in range(TOP_K):
            q = jnp.where(c_iota == loc_ref[:, k:k + 1], 1.0, q)
        q = q.astype(BF16)
        y_lo, y_hi = _unpack_bf16_pairs(y_ref[slot, c * rows:(c + 1) * rows, :])
        f_lo = f_lo + jnp.dot(q, y_lo, preferred_element_type=F32)
        f_hi = f_hi + jnp.dot(q, y_hi, preferred_element_type=F32)
    out = x_ref[...] + gate_ref[0] * jnp.concatenate([f_lo, f_hi], axis=1)
    if final:
        gain_ref, shift_ref, scale_ref = fin_refs
        out = _modulated_norm(out, gain_ref[...], shift_ref[0], scale_ref[0])
    o_ref[...] = out


def _combine(loc_tk, ys, x, gate, tables, final_mod, *, seq):
    t, d = x.shape
    tm = ROUTE_TILE
    n_s = seq // tm
    sel = pl.BlockSpec((tm, TOP_K), lambda i, *_: (i, 0))
    per_batch = pl.BlockSpec((1, 1, d), lambda i, *_: (i // n_s, 0, 0))
    in_specs = [sel, pl.BlockSpec(memory_space=pl.ANY),
                pl.BlockSpec((tm, d), lambda i, *_: (i, 0)), per_batch]
    extra = ()
    if final_mod is not None:
        gain, shift, scale = final_mod
        extra = (gain.reshape(1, d), shift, scale)
        in_specs += [pl.BlockSpec((1, d), lambda i, *_: (0, 0)), per_batch, per_batch]
    grid_spec = pltpu.PrefetchScalarGridSpec(
        num_scalar_prefetch=3,
        grid=(t // tm,),
        in_specs=in_specs,
        out_specs=pl.BlockSpec((tm, d), lambda i, *_: (i, 0)),
        scratch_shapes=[pltpu.VMEM((2, COMPACT_ROWS, d // 2), jnp.uint32),
                        pltpu.SemaphoreType.DMA((2,))],
    )
    return pl.pallas_call(
        functools.partial(_combine_kernel, final=final_mod is not None),
        grid_spec=grid_spec,
        out_shape=jax.ShapeDtypeStruct((t, d), F32),
        compiler_params=_params(("arbitrary",)),
        name="moe_combine",
    )(*tables, loc_tk, ys, x, gate, *extra)


def _moe(h, idx, wgt, rank, tile_cnt, x, gate, layer, w_gu, b_gu, w_d, b_d, final_mod, *, seq):
    t = h.shape[0]
    bm = EXPERT_ROWS
    n_tiles = t // ROUTE_TILE
    n_rows = t * TOP_K + n_tiles * N_EXPERTS * GROUP_ALIGN + N_EXPERTS * bm
    n_blocks = n_rows // bm
    cnt = tile_cnt[:, :, 0]
    grp = (cnt + GROUP_ALIGN - 1) // GROUP_ALIGN * GROUP_ALIGN
    expert_rows = jnp.sum(grp, axis=0)
    padded = (expert_rows + bm - 1) // bm * bm
    pad_end = jnp.cumsum(padded)
    tstart = (pad_end - padded)[None, :] + jnp.cumsum(grp, axis=0) - grp
    toff = jnp.cumsum(grp, axis=1) - grp
    tables = _copy_tables(grp, toff, tstart)
    off_b = jnp.broadcast_to(toff[:, :, None], (n_tiles, N_EXPERTS, LANES)).astype(jnp.int32)
    block_row = jnp.arange(n_blocks, dtype=jnp.int32) * bm
    block_expert = jnp.minimum(
        jnp.sum((pad_end[None, :] <= block_row[:, None]).astype(jnp.int32), axis=1),
        N_EXPERTS - 1).astype(jnp.int32)
    block_first = jnp.concatenate(
        [jnp.ones((1,), jnp.int32), (block_expert[1:] != block_expert[:-1]).astype(jnp.int32)])
    n_valid = (pad_end[-1:] // bm).astype(jnp.int32)
    in_expert = block_expert[:, None] == jnp.arange(N_EXPERTS, dtype=jnp.int32)[None, :]

    def of_expert(per_expert):
        return jnp.sum(jnp.where(in_expert, per_expert[None, :], 0), axis=1)

    first_block = of_expert((pad_end - padded) // bm)
    n_own = jnp.maximum(of_expert(padded // bm), 1)
    past_end = block_row >= pad_end[-1]
    block_id = jnp.arange(n_blocks, dtype=jnp.int32)
    block_src = jnp.where(past_end, block_id,
                          first_block + (block_id - first_block - 1) % n_own).astype(jnp.int32)
    rows_left = of_expert(pad_end - padded + expert_rows) - block_src * bm
    block_live = jnp.where(past_end, 0,
                           jnp.clip((rows_left + EXPERT_PART - 1) // EXPERT_PART, 1,
                                    bm // EXPERT_PART)).astype(jnp.int32)
    fills = (jnp.maximum(pad_end - bm, 0).astype(jnp.int32), (padded > 0).astype(jnp.int32),
             n_valid)
    xs, loc = _dispatch(h, idx, rank, wgt, off_b, tables, fills, n_rows)
    ys = _experts(xs, block_expert, block_first, block_live, block_src, n_valid, layer,
                  w_gu, b_gu, w_d, b_d)
    return _combine(loc.T, ys, x, gate, tables, final_mod, seq=seq)


def kernel(x, c, mod_w, mod_b, mix_norm, ffn_norm, a_wqkv, a_wo, a_lambda, a_subln, kv_norm, kv_mod_w, kv_mod_b, kv_w, b_wq, b_wo, router_w, router_b, exp_w_gate_up, exp_b_gate_up, exp_w_down, exp_b_down, final_norm, final_mod_w, final_mod_b):
    n_b, seq, d = x.shape
    t = n_b * seq
    sizes = dict(n_b=n_b, seq=seq)
    rope_tabs = _rope_tables(seq)
    q_scale = HEAD_DIM ** -0.5 * math.log2(math.e)

    def per_batch(v):
        return v.reshape(n_b, 1, d)

    mod = _adaln_vectors(c, mod_w, mod_b)
    kv_mod = _adaln_vectors(c, kv_mod_w[None], kv_mod_b[None])[0]
    fin_mod = _adaln_vectors(c, final_mod_w[None], final_mod_b[None])[0]

    a_wqkv, a_wo, b_wq, b_wo = (w.astype(BF16) for w in (a_wqkv, a_wo, b_wq, b_wo))
    kv_w = kv_w[None].astype(BF16)

    xt = x.reshape(t, d)
    shared = None
    for layer in range(DEPTH):
        sh1, sc1, g1, sh2, sc2, g2 = (per_batch(mod[layer, :, j * d:(j + 1) * d]) for j in range(6))
        if layer < N_A_LAYERS:
            lambda_init = 0.8 - 0.6 * math.exp(-0.3 * layer)
            segs = ((0, True, q_scale, (1,)), (1, True, 1.0, (1,)), (2, False, 1.0, (1,)))
            q, k, v = _norm_project(xt, mix_norm[layer], sh1, sc1, a_wqkv, layer,
                                    rope_tabs, segs, name="a_qkv_proj", **sizes)
            o = _diff_attention(q, k, v, a_lambda[layer], a_subln[layer], lambda_init, **sizes)
            wo, wo_layer = a_wo, layer
        else:
            j = layer - N_A_LAYERS
            segs = tuple((g, True, q_scale, (B_DILATIONS[g],)) for g in range(3))
            q0, q1, q2 = _norm_project(xt, mix_norm[layer], sh1, sc1, b_wq, j,
                                       rope_tabs, segs, name="b_q_proj", **sizes)
            o = _dilated_attention(q0, q1, q2, *shared, **sizes)
            wo, wo_layer = b_wo, j
        xt, h, idx, wgt, rank, tile_cnt = _mix_out_and_route(
            o, wo, wo_layer, xt, g1, ffn_norm[layer], sh2, sc2, router_w[layer],
            router_b[layer], **sizes)
        final_mod = None
        if layer == DEPTH - 1:
            final_mod = (final_norm, per_batch(fin_mod[:, :d]), per_batch(fin_mod[:, d:]))
        xt = _moe(h, idx, wgt, rank, tile_cnt, xt, g2, layer, exp_w_gate_up, exp_b_gate_up,
                  exp_w_down, exp_b_down, final_mod, seq=seq)
        if layer == N_A_LAYERS - 1:
            segs = ((0, True, 1.0, B_DILATIONS), (1, False, 1.0, B_DILATIONS))
            k0, k1, k2, v0, v1, v2 = _norm_project(
                xt, kv_norm, per_batch(kv_mod[:, :d]), per_batch(kv_mod[:, d:]),
                kv_w, 0, rope_tabs, segs, name="shared_kv_proj", **sizes)
            shared = (k0, v0, k1, v1, k2, v2)
    return xt.reshape(n_b, seq, d)
```

```python
import functools
import math

import jax
import jax.numpy as jnp
from jax import lax
from jax.experimental import pallas as pl
from jax.experimental.pallas import tpu as pltpu

F32 = jnp.float32
BF16 = jnp.bfloat16

D_MODEL = 1024
HEAD_DIM = 64
A_HEADS = 8
B_KV_HEADS = 16
B_DILATIONS = (1, 4, 16)
WINDOW_STEPS = 128
ROPE_THETA = 500000.0
ROPE_DIM = HEAD_DIM // 4
N_EXPERTS = 32
TOP_K = 4
SWIGLU_LIMIT = 7.0
SWIGLU_ALPHA = 1.702
NORM_EPS = 1e-5
N_A_LAYERS = 2
DEPTH = 4

LANES = 128
BF16_SUBLANES = 16
EXPERT_ROWS = 1024
EXPERT_PART = 256
ROUTE_TILE = 512
GROUP_ALIGN = 8
COMPACT_ROWS = ROUTE_TILE * TOP_K + N_EXPERTS * GROUP_ALIGN
COMBINE_CHUNK = COMPACT_ROWS
BIG_COPY = 32
MAX_BIG = COMPACT_ROWS // BIG_COPY
MAX_SMALL = N_EXPERTS * (BIG_COPY // GROUP_ALIGN - 1)
ROW_BITS = 12
FILL_SIZES = (256, 32, GROUP_ALIGN)
NEG_BIG = -1e30
V7X_VMEM_BYTES = 64 * 1024 * 1024
VMEM_LIMIT = V7X_VMEM_BYTES // 8 * 7


def _params(sem, vmem=VMEM_LIMIT):
    return pltpu.CompilerParams(dimension_semantics=sem, vmem_limit_bytes=vmem)


def _adaln_kernel(c_ref, w_ref, b_ref, o_ref):
    c = c_ref[...]
    c_act = (c * jax.nn.sigmoid(c)).astype(BF16)
    o_ref[0] = jnp.dot(c_act, w_ref[0].astype(BF16), preferred_element_type=F32) + b_ref[0]


def _adaln_vectors(c, w, b):
    n_l, d, n = w.shape
    n_b = c.shape[0]
    tn = 1024
    return pl.pallas_call(
        _adaln_kernel,
        grid=(n_l, n // tn),
        in_specs=[
            pl.BlockSpec((n_b, d), lambda l, j: (0, 0)),
            pl.BlockSpec((1, d, tn), lambda l, j: (l, 0, j)),
            pl.BlockSpec((1, 1, tn), lambda l, j: (l, 0, j)),
        ],
        out_specs=pl.BlockSpec((1, n_b, tn), lambda l, j: (l, 0, j)),
        out_shape=jax.ShapeDtypeStruct((n_l, n_b, n), F32),
        compiler_params=_params(("parallel", "parallel")),
        name="adaln_vectors",
    )(c, w, b.reshape(n_l, 1, n))


def _modulated_norm(x, gain, shift, scale):
    y = x * lax.rsqrt(jnp.mean(x * x, axis=-1, keepdims=True) + NORM_EPS)
    return y * gain * (1.0 + scale) + shift


def _rope_tables(seq):
    inv = ROPE_THETA ** (-jnp.arange(0, ROPE_DIM, 2, dtype=F32) / ROPE_DIM)
    ang = jnp.arange(seq, dtype=F32)[:, None] * inv[None, :]
    cos, sin = jnp.cos(ang), jnp.sin(ang)
    half = ROPE_DIM // 2
    rest = HEAD_DIM - ROPE_DIM
    zeros = jnp.zeros((seq, half), F32)
    c_tab = jnp.concatenate([cos, cos, jnp.ones((seq, rest), F32)], axis=1)
    s1_tab = jnp.concatenate([zeros, sin, jnp.zeros((seq, rest), F32)], axis=1)
    s2_tab = jnp.concatenate([-sin, zeros, jnp.zeros((seq, rest), F32)], axis=1)
    rep = LANES // HEAD_DIM
    return tuple(jnp.tile(t, (1, rep)) for t in (c_tab, s1_tab, s2_tab))


def _apply_rope(y, c_tab, s1_tab, s2_tab):
    parts = []
    for j in range(y.shape[1] // LANES):
        yj = y[:, j * LANES:(j + 1) * LANES]
        parts.append(yj * c_tab + pltpu.roll(yj, ROPE_DIM // 2, 1) * s1_tab
                     + pltpu.roll(yj, LANES - ROPE_DIM // 2, 1) * s2_tab)
    return jnp.concatenate(parts, axis=1)


def _proj_kernel(x_ref, gain_ref, shift_ref, scale_ref, w_ref, rc_ref, rs1_ref, rs2_ref, *rest,
                 segs, tm):
    n_out = sum(len(s[3]) for s in segs)
    out_refs, scr_ref = rest[:n_out], rest[n_out]
    h = _modulated_norm(x_ref[...], gain_ref[...], shift_ref[0], scale_ref[0]).astype(BF16)
    oi = 0
    for chunk, rope, mult, dils in segs:
        y = jnp.dot(h, w_ref[:, chunk * D_MODEL:(chunk + 1) * D_MODEL], preferred_element_type=F32)
        if rope:
            y = _apply_rope(y, rc_ref[...], rs1_ref[...], rs2_ref[...])
        if mult != 1.0:
            y = y * mult
        for dil in dils:
            o_ref = out_refs[oi]
            oi += 1
            if dil == 1:
                o_ref[...] = y.astype(BF16)
            else:
                for j in range(D_MODEL // LANES):
                    scr_ref[j] = y[:, j * LANES:(j + 1) * LANES]
                for r in range(dil):
                    for j in range(D_MODEL // LANES):
                        o_ref[0, r, :, j * LANES:(j + 1) * LANES] = (
                            scr_ref[j, pl.ds(r, tm // dil, stride=dil), :].astype(BF16))


def _norm_project(x, gain, shift, scale, w, layer, rope_tabs, segs, *, n_b, seq, name):
    t, d = x.shape
    tm = 512
    n_s = seq // tm
    out_shapes, out_specs = [], []
    for _, _, _, dils in segs:
        for dil in dils:
            if dil == 1:
                out_shapes.append(jax.ShapeDtypeStruct((t, D_MODEL), BF16))
                out_specs.append(pl.BlockSpec((tm, D_MODEL), lambda i: (i, 0)))
            else:
                out_shapes.append(jax.ShapeDtypeStruct((n_b, dil, seq // dil, D_MODEL), BF16))
                out_specs.append(pl.BlockSpec((1, dil, tm // dil, D_MODEL),
                                              lambda i: (i // n_s, 0, i % n_s, 0)))
    per_batch = pl.BlockSpec((1, 1, d), lambda i: (i // n_s, 0, 0))
    rope_spec = pl.BlockSpec((tm, LANES), lambda i: (i % n_s, 0))
    return pl.pallas_call(
        functools.partial(_proj_kernel, segs=segs, tm=tm),
        grid=(t // tm,),
        in_specs=[
            pl.BlockSpec((tm, d), lambda i: (i, 0)),
            pl.BlockSpec((1, d), lambda i: (0, 0)),
            per_batch, per_batch,
            pl.BlockSpec((None,) + w.shape[1:], lambda i: (layer, 0, 0)),
            rope_spec, rope_spec, rope_spec,
        ],
        out_specs=out_specs,
        out_shape=out_shapes,
        scratch_shapes=[pltpu.VMEM((D_MODEL // LANES, tm, LANES), F32)],
        compiler_params=_params(("parallel",)),
        name=name,
    )(x, gain.reshape(1, d), shift, scale, w, *rope_tabs)


def _diff_attn_kernel(lam_ref, subln_ref, q_ref, k_ref, v_ref, o_ref, *, tq, lambda_init):
    lf = lam_ref[...]
    lam = (jnp.exp(jnp.sum(lf[0:1] * lf[1:2], keepdims=True))
           - jnp.exp(jnp.sum(lf[2:3] * lf[3:4], keepdims=True)) + lambda_init)
    lane_lo = lax.broadcasted_iota(jnp.int32, (tq, 2 * HEAD_DIM), 1) < HEAD_DIM
    causal = (lax.broadcasted_iota(jnp.int32, (tq, tq), 0)
              <= lax.broadcasted_iota(jnp.int32, (tq, tq), 1))

    def scores(kb, qm):
        return lax.dot_general(kb, qm, (((1,), (1,)), ((), ())), preferred_element_type=F32)

    def online(s, m, acc, vbt):
        m_new = jnp.maximum(m, jnp.max(s, axis=0, keepdims=True))
        p = jnp.exp2(s - m_new)
        alpha = jnp.exp2(m - m_new)
        acc = alpha * acc + jnp.dot(vbt, p.astype(BF16), preferred_element_type=F32)
        return m_new, acc

    d_v = 2 * HEAD_DIM
    ones_rows = jnp.ones((BF16_SUBLANES, tq), BF16)
    for i in range(q_ref.shape[0] // tq):
        q = q_ref[i * tq:(i + 1) * tq, :]
        zero = jnp.zeros_like(q)
        q1 = jnp.where(lane_lo, q, zero)
        q2 = jnp.where(lane_lo, zero, q)
        m1 = m2 = jnp.full((1, tq), NEG_BIG, F32)
        a1 = a2 = jnp.zeros((d_v + BF16_SUBLANES, tq), F32)
        for j in range(i + 1):
            kb = k_ref[j * tq:(j + 1) * tq, :]
            vbt = jnp.concatenate([v_ref[j * tq:(j + 1) * tq, :].T, ones_rows], axis=0)
            s1 = scores(kb, q1)
            s2 = scores(kb, q2)
            if j == i:
                s1 = jnp.where(causal, s1, NEG_BIG)
                s2 = jnp.where(causal, s2, NEG_BIG)
            m1, a1 = online(s1, m1, a1, vbt)
            m2, a2 = online(s2, m2, a2, vbt)
        o = (a1[:d_v] / a1[d_v:d_v + 1] - lam * (a2[:d_v] / a2[d_v:d_v + 1])).T
        o = o * lax.rsqrt(jnp.mean(o * o, axis=-1, keepdims=True) + NORM_EPS)
        o_ref[i * tq:(i + 1) * tq, :] = (o * subln_ref[...] * (1.0 - lambda_init)).astype(BF16)


def _diff_attention(q, k, v, lam_vecs, subln, lambda_init, *, n_b, seq):
    t = q.shape[0]
    width = 2 * HEAD_DIM
    head_seq = pl.BlockSpec((seq, width), lambda b, h: (b, h))
    return pl.pallas_call(
        functools.partial(_diff_attn_kernel, tq=512, lambda_init=lambda_init),
        grid=(n_b, A_HEADS),
        in_specs=[
            pl.BlockSpec((4, HEAD_DIM), lambda b, h: (0, 0)),
            pl.BlockSpec((1, width), lambda b, h: (0, 0)),
            head_seq, head_seq, head_seq,
        ],
        out_specs=head_seq,
        out_shape=jax.ShapeDtypeStruct((t, D_MODEL), BF16),
        compiler_params=_params(("parallel", "parallel")),
        name="diff_attention",
    )(lam_vecs, subln.reshape(1, width), q, k, v)


def _window_unit(q, kc, vc, bias, lane_lo):
    zero = jnp.zeros_like(q)
    qq = jnp.concatenate([jnp.where(lane_lo, q, zero), jnp.where(lane_lo, zero, q)], axis=0)
    s = lax.dot_general(qq, kc, (((1,), (1,)), ((), ())), preferred_element_type=F32) + bias
    m = jnp.max(s, axis=-1, keepdims=True)
    p = jnp.exp2(s - m).astype(BF16)
    v_aug = jnp.concatenate([vc, jnp.ones(vc.shape, BF16)], axis=1)
    o_aug = jnp.dot(p, v_aug, preferred_element_type=F32)
    n = WINDOW_STEPS
    o = jnp.where(lane_lo, o_aug[:n, :LANES], o_aug[n:, :LANES])
    l = jnp.where(lane_lo, o_aug[:n, LANES:], o_aug[n:, LANES:])
    m_sel = jnp.where(lane_lo, m[:n], m[n:])
    return o / l, m_sel + jnp.log2(l)


def _dil_attn_kernel(q0_ref, q1_ref, q2_ref, k0_ref, v0_ref, k1_ref, v1_ref, k2_ref, v2_ref,
                     o_ref, acc_ref, lse_ref, *, seq):
    n = WINDOW_STEPS
    lane_lo = lax.broadcasted_iota(jnp.int32, (n, LANES), 1) < HEAD_DIM
    qi = lax.broadcasted_iota(jnp.int32, (2 * n, 2 * n), 0) % n
    kj = lax.broadcasted_iota(jnp.int32, (2 * n, 2 * n), 1)
    band = jnp.where((kj >= qi) & (kj <= qi + n), 0.0, NEG_BIG)
    first = jnp.where(lax.broadcasted_iota(jnp.int32, (2 * n, n), 1)
                      <= lax.broadcasted_iota(jnp.int32, (2 * n, n), 0) % n, 0.0, NEG_BIG)

    def unit(g, q_rows, k_rows, v_rows, bias, dst):
        o, lse = _window_unit(q_rows, k_rows, v_rows, bias, lane_lo)
        acc_ref[g, dst, :] = o
        lse_ref[g, dst, :] = lse

    def rows(ref, r, lo, hi):
        return ref[lo:hi, :] if len(ref.shape) == 2 else ref[0, r, lo:hi, :]

    for g, (q_ref, k_ref, v_ref) in enumerate(((q0_ref, k0_ref, v0_ref), (q1_ref, k1_ref, v1_ref),
                                               (q2_ref, k2_ref, v2_ref))):
        dil = B_DILATIONS[g]
        for r in range(dil):
            for blk in range(seq // dil // n):
                dst = pl.ds(r + blk * n * dil, n, stride=dil) if dil > 1 else pl.ds(blk * n, n)
                k_lo = max(blk - 1, 0) * n
                unit(g, rows(q_ref, r, blk * n, (blk + 1) * n), rows(k_ref, r, k_lo, (blk + 1) * n),
                     rows(v_ref, r, k_lo, (blk + 1) * n), first if blk == 0 else band, dst)

    l0, l1, l2 = lse_ref[0], lse_ref[1], lse_ref[2]
    top = jnp.maximum(jnp.maximum(l0, l1), l2)
    w0, w1, w2 = jnp.exp2(l0 - top), jnp.exp2(l1 - top), jnp.exp2(l2 - top)
    o = (w0 * acc_ref[0] + w1 * acc_ref[1] + w2 * acc_ref[2]) / (w0 + w1 + w2)
    o_ref[...] = o.astype(BF16)


def _dilated_attention(q0, q1, q2, k0, v0, k1, v1, k2, v2, *, n_b, seq):
    t = q0.shape[0]
    pairs = D_MODEL // LANES
    nat = pl.BlockSpec((seq, LANES), lambda b, p: (b, p))

    def res(dil):
        return pl.BlockSpec((1, dil, seq // dil, LANES), lambda b, p: (b, 0, 0, p))

    d1, d2 = B_DILATIONS[1], B_DILATIONS[2]
    return pl.pallas_call(
        functools.partial(_dil_attn_kernel, seq=seq),
        grid=(n_b, pairs),
        in_specs=[nat, res(d1), res(d2), nat, nat, res(d1), res(d1), res(d2), res(d2)],
        out_specs=nat,
        out_shape=jax.ShapeDtypeStruct((t, D_MODEL), BF16),
        scratch_shapes=[pltpu.VMEM((3, seq, LANES), F32), pltpu.VMEM((3, seq, LANES), F32)],
        compiler_params=_params(("parallel", "parallel")),
        name="dilated_attention",
    )(q0, q1, q2, k0, v0, k1, v1, k2, v2)


def _mix_out_kernel(o_ref, wo_ref, x_ref, gate_ref, gain_ref, shift_ref, scale_ref, rw_ref, rb_ref,
                    xn_ref, h_ref, idx_ref, wgt_ref, rank_ref, cnt_ref, *, tm):
    y = jnp.dot(o_ref[...], wo_ref[...], preferred_element_type=F32)
    xn = x_ref[...] + gate_ref[0] * y
    xn_ref[...] = xn
    h = _modulated_norm(xn, gain_ref[...], shift_ref[0], scale_ref[0])

    h_hi = h.astype(BF16)
    h_ref[...] = h_hi
    h_lo = (h - h_hi.astype(F32)).astype(BF16)
    rw = rw_ref[...]
    rw_hi = rw.astype(BF16)
    rw_lo = (rw - rw_hi.astype(F32)).astype(BF16)
    nt = (((1,), (1,)), ((), ()))
    logits = (lax.dot_general(rw_hi, h_hi, nt, preferred_element_type=F32)
              + lax.dot_general(rw_lo, h_hi, nt, preferred_element_type=F32)
              + lax.dot_general(rw_hi, h_lo, nt, preferred_element_type=F32)) + rb_ref[...]

    e_iota = lax.broadcasted_iota(jnp.int32, logits.shape, 0)
    work = logits
    sels, tops, idxs = [], [], []
    for _ in range(TOP_K):
        mk = jnp.max(work, axis=0, keepdims=True)
        ik = jnp.min(jnp.where(work == mk, e_iota, N_EXPERTS), axis=0, keepdims=True)
        sel = e_iota == ik
        work = jnp.where(sel, -jnp.inf, work)
        sels.append(sel)
        tops.append(mk)
        idxs.append(ik)
    exps = [jnp.exp(m - tops[0]) for m in tops]
    denom = exps[0] + exps[1] + exps[2] + exps[3]
    for k in range(TOP_K):
        idx_ref[k:k + 1, :] = idxs[k]
        wgt_ref[k:k + 1, :] = exps[k] / denom

    chosen = jnp.zeros(logits.shape, F32)
    for sel in sels:
        chosen = chosen + jnp.where(sel, 1.0, 0.0)
    before = (lax.broadcasted_iota(jnp.int32, (tm, tm), 0)
              < lax.broadcasted_iota(jnp.int32, (tm, tm), 1))
    upper = jnp.where(before, 1.0, 0.0).astype(BF16)
    prefix = jnp.dot(chosen.astype(BF16), upper, preferred_element_type=F32)
    for k in range(TOP_K):
        rank_ref[k:k + 1, :] = jnp.sum(jnp.where(sels[k], prefix, 0.0), axis=0,
                                       keepdims=True).astype(jnp.int32)
    counts = jnp.sum(chosen, axis=1, keepdims=True).astype(jnp.int32)
    cnt_ref[0] = jnp.broadcast_to(counts, cnt_ref.shape[1:])


def _mix_out_and_route(o, wo, layer, x, gate, gain, shift, scale, router_w, router_b, *, n_b, seq):
    t, d = x.shape
    tm = ROUTE_TILE
    n_s = seq // tm
    per_batch = pl.BlockSpec((1, 1, d), lambda i: (i // n_s, 0, 0))
    row = pl.BlockSpec((tm, d), lambda i: (i, 0))
    sel = pl.BlockSpec((TOP_K, tm), lambda i: (0, i))
    return pl.pallas_call(
        functools.partial(_mix_out_kernel, tm=tm),
        grid=(t // tm,),
        in_specs=[
            row,
            pl.BlockSpec((None,) + wo.shape[1:], lambda i: (layer, 0, 0)),
            row, per_batch,
            pl.BlockSpec((1, d), lambda i: (0, 0)),
            per_batch, per_batch,
            pl.BlockSpec((N_EXPERTS, d), lambda i: (0, 0)),
            pl.BlockSpec((N_EXPERTS, 1), lambda i: (0, 0)),
        ],
        out_specs=[row, row, sel, sel, sel,
                   pl.BlockSpec((1, N_EXPERTS, LANES), lambda i: (i, 0, 0))],
        out_shape=[
            jax.ShapeDtypeStruct((t, d), F32),
            jax.ShapeDtypeStruct((t, d), BF16),
            jax.ShapeDtypeStruct((TOP_K, t), jnp.int32),
            jax.ShapeDtypeStruct((TOP_K, t), F32),
            jax.ShapeDtypeStruct((TOP_K, t), jnp.int32),
            jax.ShapeDtypeStruct((t // tm, N_EXPERTS, LANES), jnp.int32),
        ],
        compiler_params=_params(("parallel",)),
        name="mix_out_route",
    )(o, wo, x, gate, gain.reshape(1, d), shift, scale, router_w.T, router_b.reshape(N_EXPERTS, 1))


HIGH_HALF = 0xFFFF0000


def _pack_bf16_pairs(x):
    c = x.shape[1] // 2
    lo = lax.bitcast_convert_type(x[:, :c], jnp.uint32)
    hi = lax.bitcast_convert_type(x[:, c:], jnp.uint32)
    return (lo >> 16) | (hi & jnp.uint32(HIGH_HALF))


def _unpack_bf16_pairs(w):
    lo = lax.bitcast_convert_type(w << 16, F32).astype(BF16)
    hi = lax.bitcast_convert_type(w & jnp.uint32(HIGH_HALF), F32).astype(BF16)
    return lo, hi


def _copy_tables(grp, toff, tstart):
    experts = jnp.arange(N_EXPERTS, dtype=jnp.int32)

    def words(count, first_row, rows, width):
        end = jnp.cumsum(count, axis=1)
        j = jnp.arange(width, dtype=jnp.int32)
        e_of = jnp.sum((end[:, None, :] <= j[None, :, None]).astype(jnp.int32), axis=2)
        onehot = e_of[:, :, None] == experts[None, None, :]

        def pick(a):
            return jnp.sum(jnp.where(onehot, a[:, None, :], 0), axis=2)

        row = pick(first_row) + (j[None, :] - pick(end - count)) * rows
        word = ((pick(tstart) + row) << ROW_BITS) | (pick(toff) + row)
        return word.reshape(-1).astype(jnp.int32), end[:, -1]

    n_big = grp // BIG_COPY
    big, total_big = words(n_big, jnp.zeros_like(grp), BIG_COPY, MAX_BIG)
    small, total_small = words((grp - n_big * BIG_COPY) // GROUP_ALIGN, n_big * BIG_COPY,
                               GROUP_ALIGN, MAX_SMALL)
    counts = jnp.stack([total_big, total_small, jnp.sum(grp, axis=1)], axis=1)
    return counts.reshape(-1).astype(jnp.int32), big, small


def _tile_copies(tile, cnt_ref, big_ref, small_ref, make_copy, wait):
    if wait:
        total = cnt_ref[3 * tile + 2]

        def wait_rows(rows):
            def body(j, carry):
                make_copy(0, 0, rows).wait()
                return carry
            return body

        lax.fori_loop(0, total // BIG_COPY, wait_rows(BIG_COPY), 0)
        lax.fori_loop(0, (total % BIG_COPY) // GROUP_ALIGN, wait_rows(GROUP_ALIGN), 0)
        return

    def start_rows(table_ref, width, rows):
        def body(j, carry):
            word = table_ref[tile * width + j]
            make_copy(pl.multiple_of(word & ((1 << ROW_BITS) - 1), GROUP_ALIGN),
                      pl.multiple_of(word >> ROW_BITS, GROUP_ALIGN), rows).start()
            return carry
        return body

    lax.fori_loop(0, cnt_ref[3 * tile], start_rows(big_ref, MAX_BIG, BIG_COPY), 0)
    lax.fori_loop(0, cnt_ref[3 * tile + 1], start_rows(small_ref, MAX_SMALL, GROUP_ALIGN), 0)


def _dispatch_kernel(cnt_ref, big_ref, small_ref, fill_row_ref, fill_len_ref, nvalid_ref,
                     h_ref, idx_ref, rank_ref, wgt_ref, off_ref, xs_ref, loc_ref, z_ref, zero_ref,
                     sems, *, n_blocks):
    i = pl.program_id(0)
    last = pl.num_programs(0) - 1
    slot = i % 2
    tm = h_ref.shape[0]

    def copies(tile, buf, wait):
        def make_copy(local, glob, n):
            return pltpu.make_async_copy(z_ref.at[buf, pl.ds(local, n)],
                                         xs_ref.at[pl.ds(glob, n)], sems.at[buf])

        _tile_copies(tile, cnt_ref, big_ref, small_ref, make_copy, wait)

    def fills(wait):
        def piece(row, n):
            cp = pltpu.make_async_copy(
                zero_ref.at[pl.ds(0, n)],
                xs_ref.at[pl.ds(pl.multiple_of(row, GROUP_ALIGN), n)], sems.at[2])
            if wait:
                cp.wait()
            else:
                cp.start()

        def expert_pad(e, carry):
            row, n = fill_row_ref[e], fill_len_ref[e]
            for size in FILL_SIZES:
                count = n // size

                def body(j, c, row=row, size=size):
                    piece(row + j * size, size)
                    return c

                lax.fori_loop(0, count, body, 0)
                row, n = row + count * size, n - count * size
            return carry

        lax.fori_loop(0, N_EXPERTS, expert_pad, 0)

        def tail_piece(j, carry):
            piece(nvalid_ref[0] * EXPERT_ROWS + j * FILL_SIZES[0], FILL_SIZES[0])
            return carry

        lax.fori_loop(0, (n_blocks - nvalid_ref[0]) * (EXPERT_ROWS // FILL_SIZES[0]),
                      tail_piece, 0)

    @pl.when(i == 0)
    def _():
        zero_ref[...] = jnp.zeros_like(zero_ref)
        fills(wait=False)

    e_iota = lax.broadcasted_iota(jnp.int32, (N_EXPERTS, tm), 0)
    off = off_ref[0][:, 0:1]
    locs = []
    for k in range(TOP_K):
        sel = e_iota == idx_ref[k:k + 1, :]
        loc = jnp.sum(jnp.where(sel, off, 0), axis=0, keepdims=True) + rank_ref[k:k + 1, :]
        loc_ref[k:k + 1, :] = loc
        locs.append(loc)

    @pl.when(i >= 2)
    def _():
        copies(i - 2, slot, wait=True)

    h = h_ref[...]
    half = h.shape[1] // 2
    rows = 256
    for c in range(COMPACT_ROWS // rows):
        r_iota = lax.broadcasted_iota(jnp.int32, (rows, tm), 0) + c * rows
        gate = jnp.zeros((rows, tm), F32)
        for k, loc in enumerate(locs):
            gate = jnp.where(r_iota == loc, wgt_ref[k:k + 1, :], gate)
        hit = jnp.where(gate != 0.0, 1.0, 0.0)
        z_ref[slot, c * rows:(c + 1) * rows, :half] = _pack_bf16_pairs(
            jnp.dot(hit.astype(BF16), h, preferred_element_type=F32))
        row_gate = jnp.sum(gate, axis=1, keepdims=True)
        z_ref[slot, c * rows:(c + 1) * rows, half:] = lax.bitcast_convert_type(
            jnp.broadcast_to(row_gate, (rows, LANES)), jnp.uint32)

    copies(i, slot, wait=False)

    @pl.when(i == last)
    def _():
        @pl.when(i >= 1)
        def _():
            copies(i - 1, 1 - slot, wait=True)

        copies(i, slot, wait=True)
        fills(wait=True)


def _dispatch(h, idx, rank, wgt, off_b, tables, fills, n_rows):
    t, d = h.shape
    tm = ROUTE_TILE
    width = d // 2 + LANES
    sel = pl.BlockSpec((TOP_K, tm), lambda i, *_: (0, i))
    grid_spec = pltpu.PrefetchScalarGridSpec(
        num_scalar_prefetch=6,
        grid=(t // tm,),
        in_specs=[
            pl.BlockSpec((tm, d), lambda i, *_: (i, 0)),
            sel, sel, sel,
            pl.BlockSpec((1, N_EXPERTS, LANES), lambda i, *_: (i, 0, 0)),
        ],
        out_specs=[pl.BlockSpec(memory_space=pl.ANY), sel],
        scratch_shapes=[pltpu.VMEM((2, COMPACT_ROWS, width), jnp.uint32),
                        pltpu.VMEM((FILL_SIZES[0], width), jnp.uint32),
                        pltpu.SemaphoreType.DMA((3,))],
    )
    return pl.pallas_call(
        functools.partial(_dispatch_kernel, n_blocks=n_rows // EXPERT_ROWS),
        grid_spec=grid_spec,
        out_shape=[jax.ShapeDtypeStruct((n_rows, width), jnp.uint32),
                   jax.ShapeDtypeStruct((TOP_K, t), jnp.int32)],
        compiler_params=_params(("arbitrary",)),
        name="moe_dispatch",
    )(*tables, *fills, h, idx, rank, wgt, off_b)


def _expert_kernel(be_ref, first_ref, live_ref, src_ref, nvalid_ref, xs_ref, wgu_ref, bgu_ref,
                   wd_ref, bd_ref, ys_ref, wgu_bf, wd_bf):
    del be_ref, src_ref, nvalid_ref
    b = pl.program_id(0)
    bm = xs_ref.shape[0]

    @pl.when(first_ref[b] == 1)
    def _():
        wgu_bf[...] = wgu_ref[0].astype(BF16)
        wd_bf[...] = wd_ref[0].astype(BF16)

    def ffn(rows):
        d_ff = wd_bf.shape[0]
        half = wd_bf.shape[1] // 2
        x = jnp.concatenate(_unpack_bf16_pairs(xs_ref[:rows, :half]), axis=1)
        row_gate = lax.bitcast_convert_type(xs_ref[:rows, half:], F32)
        gu = jnp.dot(x, wgu_bf[...], preferred_element_type=F32) + bgu_ref[0]
        gate = jnp.minimum(gu[:, :d_ff], SWIGLU_LIMIT)
        up = jnp.clip(gu[:, d_ff:], -SWIGLU_LIMIT, SWIGLU_LIMIT)
        glu = gate * jax.nn.sigmoid(SWIGLU_ALPHA * gate)
        act = ((up + 1.0) * glu).astype(BF16)
        y = jnp.dot(act, wd_bf[...], preferred_element_type=F32) + bd_ref[0]
        y = y * jnp.concatenate([row_gate] * (y.shape[1] // LANES), axis=1)
        ys_ref[:rows, :] = _pack_bf16_pairs(y.astype(BF16).astype(F32))

    for parts in range(bm // EXPERT_PART + 1):
        @pl.when(live_ref[b] == parts)
        def _():
            rows = parts * EXPERT_PART
            if rows > 0:
                ffn(rows)
            if rows < bm:
                ys_ref[rows:, :] = jnp.zeros((bm - rows, ys_ref.shape[1]), ys_ref.dtype)


def _experts(xs, block_expert, block_first, block_live, block_src, n_valid, layer,
             w_gu, b_gu, w_d, b_d):
    n_rows, in_width = xs.shape
    bm = EXPERT_ROWS
    d, d_ff = w_d.shape[3], w_d.shape[2]
    half = d // 2
    n_l = w_gu.shape[0]
    grid_spec = pltpu.PrefetchScalarGridSpec(
        num_scalar_prefetch=5,
        grid=(n_rows // bm,),
        in_specs=[
            pl.BlockSpec((bm, in_width),
                         lambda b, be, fi, lv, src, nv: (src[jnp.minimum(b, nv[0] - 1)], 0)),
            pl.BlockSpec((None, 1, d, 2 * d_ff), lambda b, be, *_: (layer, be[b], 0, 0)),
            pl.BlockSpec((None, 1, 1, 2 * d_ff), lambda b, be, *_: (layer, be[b], 0, 0)),
            pl.BlockSpec((None, 1, d_ff, d), lambda b, be, *_: (layer, be[b], 0, 0)),
            pl.BlockSpec((None, 1, 1, d), lambda b, be, *_: (layer, be[b], 0, 0)),
        ],
        out_specs=pl.BlockSpec((bm, half), lambda b, be, fi, lv, src, nv: (src[b], 0)),
        scratch_shapes=[pltpu.VMEM((d, 2 * d_ff), BF16), pltpu.VMEM((d_ff, d), BF16)],
    )
    return pl.pallas_call(
        _expert_kernel,
        grid_spec=grid_spec,
        out_shape=jax.ShapeDtypeStruct((n_rows, half), jnp.uint32),
        compiler_params=_params(("arbitrary",)),
        name="moe_experts",
    )(block_expert, block_first, block_live, block_src, n_valid, xs, w_gu,
      b_gu.reshape(n_l, N_EXPERTS, 1, 2 * d_ff), w_d, b_d.reshape(n_l, N_EXPERTS, 1, d))


def _combine_kernel(cnt_ref, big_ref, small_ref, loc_ref, ys_ref, x_ref, gate_ref,
                    *rest, final):
    fin_refs, (o_ref, y_ref, sems) = rest[:-3], rest[-3:]
    i = pl.program_id(0)
    slot = i % 2
    tm = x_ref.shape[0]

    def copies(tile, buf, wait):
        def make_copy(local, glob, n):
            return pltpu.make_async_copy(ys_ref.at[pl.ds(glob, n)],
                                         y_ref.at[buf, pl.ds(local, n)], sems.at[buf])

        _tile_copies(tile, cnt_ref, big_ref, small_ref, make_copy, wait)

    @pl.when(i == 0)
    def _():
        y_ref[...] = jnp.zeros_like(y_ref)
        copies(0, 0, wait=False)

    copies(i, slot, wait=True)

    @pl.when(i + 1 < pl.num_programs(0))
    def _():
        copies(i + 1, 1 - slot, wait=False)

    rows = COMBINE_CHUNK
    half = y_ref.shape[2]
    f_lo = jnp.zeros((tm, half), F32)
    f_hi = jnp.zeros((tm, half), F32)
    for c in range(COMPACT_ROWS // rows):
        c_iota = lax.broadcasted_iota(jnp.int32, (tm, rows), 1) + c * rows
        q = jnp.zeros((tm, rows), F32)
        for k in range(TOP_K):
            q = jnp.where(c_iota == loc_ref[:, k:k + 1], 1.0, q)
        q = q.astype(BF16)
        y_lo, y_hi = _unpack_bf16_pairs(y_ref[slot, c * rows:(c + 1) * rows, :])
        f_lo = f_lo + jnp.dot(q, y_lo, preferred_element_type=F32)
        f_hi = f_hi + jnp.dot(q, y_hi, preferred_element_type=F32)
    out = x_ref[...] + gate_ref[0] * jnp.concatenate([f_lo, f_hi], axis=1)
    if final:
        gain_ref, shift_ref, scale_ref = fin_refs
        out = _modulated_norm(out, gain_ref[...], shift_ref[0], scale_ref[0])
    o_ref[...] = out


def _combine(loc_tk, ys, x, gate, tables, final_mod, *, seq):
    t, d = x.shape
    tm = ROUTE_TILE
    n_s = seq // tm
    sel = pl.BlockSpec((tm, TOP_K), lambda i, *_: (i, 0))
    per_batch = pl.BlockSpec((1, 1, d), lambda i, *_: (i // n_s, 0, 0))
    in_specs = [sel, pl.BlockSpec(memory_space=pl.ANY),
                pl.BlockSpec((tm, d), lambda i, *_: (i, 0)), per_batch]
    extra = ()
    if final_mod is not None:
        gain, shift, scale = final_mod
        extra = (gain.reshape(1, d), shift, scale)
        in_specs += [pl.BlockSpec((1, d), lambda i, *_: (0, 0)), per_batch, per_batch]
    grid_spec = pltpu.PrefetchScalarGridSpec(
        num_scalar_prefetch=3,
        grid=(t // tm,),
        in_specs=in_specs,
        out_specs=pl.BlockSpec((tm, d), lambda i, *_: (i, 0)),
        scratch_shapes=[pltpu.VMEM((2, COMPACT_ROWS, d // 2), jnp.uint32),
                        pltpu.SemaphoreType.DMA((2,))],
    )
    return pl.pallas_call(
        functools.partial(_combine_kernel, final=final_mod is not None),
        grid_spec=grid_spec,
        out_shape=jax.ShapeDtypeStruct((t, d), F32),
        compiler_params=_params(("arbitrary",)),
        name="moe_combine",
    )(*tables, loc_tk, ys, x, gate, *extra)


def _moe(h, idx, wgt, rank, tile_cnt, x, gate, layer, w_gu, b_gu, w_d, b_d, final_mod, *, seq):
    t = h.shape[0]
    bm = EXPERT_ROWS
    n_tiles = t // ROUTE_TILE
    n_rows = t * TOP_K + n_tiles * N_EXPERTS * GROUP_ALIGN + N_EXPERTS * bm
    n_blocks = n_rows // bm
    cnt = tile_cnt[:, :, 0]
    grp = (cnt + GROUP_ALIGN - 1) // GROUP_ALIGN * GROUP_ALIGN
    expert_rows = jnp.sum(grp, axis=0)
    padded = (expert_rows + bm - 1) // bm * bm
    pad_end = jnp.cumsum(padded)
    tstart = (pad_end - padded)[None, :] + jnp.cumsum(grp, axis=0) - grp
    toff = jnp.cumsum(grp, axis=1) - grp
    tables = _copy_tables(grp, toff, tstart)
    off_b = jnp.broadcast_to(toff[:, :, None], (n_tiles, N_EXPERTS, LANES)).astype(jnp.int32)
    block_row = jnp.arange(n_blocks, dtype=jnp.int32) * bm
    block_expert = jnp.minimum(
        jnp.sum((pad_end[None, :] <= block_row[:, None]).astype(jnp.int32), axis=1),
        N_EXPERTS - 1).astype(jnp.int32)
    block_first = jnp.concatenate(
        [jnp.ones((1,), jnp.int32), (block_expert[1:] != block_expert[:-1]).astype(jnp.int32)])
    n_valid = (pad_end[-1:] // bm).astype(jnp.int32)
    in_expert = block_expert[:, None] == jnp.arange(N_EXPERTS, dtype=jnp.int32)[None, :]

    def of_expert(per_expert):
        return jnp.sum(jnp.where(in_expert, per_expert[None, :], 0), axis=1)

    first_block = of_expert((pad_end - padded) // bm)
    n_own = jnp.maximum(of_expert(padded // bm), 1)
    past_end = block_row >= pad_end[-1]
    block_id = jnp.arange(n_blocks, dtype=jnp.int32)
    block_src = jnp.where(past_end, block_id,
                          first_block + (block_id - first_block - 1) % n_own).astype(jnp.int32)
    rows_left = of_expert(pad_end - padded + expert_rows) - block_src * bm
    block_live = jnp.where(past_end, 0,
                           jnp.clip((rows_left + EXPERT_PART - 1) // EXPERT_PART, 1,
                                    bm // EXPERT_PART)).astype(jnp.int32)
    fills = ((pad_end - padded + expert_rows).astype(jnp.int32),
             (padded - expert_rows).astype(jnp.int32), n_valid)
    xs, loc = _dispatch(h, idx, rank, wgt, off_b, tables, fills, n_rows)
    ys = _experts(xs, block_expert, block_first, block_live, block_src, n_valid, layer,
                  w_gu, b_gu, w_d, b_d)
    return _combine(loc.T, ys, x, gate, tables, final_mod, seq=seq)


def kernel(x, c, mod_w, mod_b, mix_norm, ffn_norm, a_wqkv, a_wo, a_lambda, a_subln, kv_norm, kv_mod_w, kv_mod_b, kv_w, b_wq, b_wo, router_w, router_b, exp_w_gate_up, exp_b_gate_up, exp_w_down, exp_b_down, final_norm, final_mod_w, final_mod_b):
    n_b, seq, d = x.shape
    t = n_b * seq
    sizes = dict(n_b=n_b, seq=seq)
    rope_tabs = _rope_tables(seq)
    q_scale = HEAD_DIM ** -0.5 * math.log2(math.e)

    def per_batch(v):
        return v.reshape(n_b, 1, d)

    mod = _adaln_vectors(c, mod_w, mod_b)
    kv_mod = _adaln_vectors(c, kv_mod_w[None], kv_mod_b[None])[0]
    fin_mod = _adaln_vectors(c, final_mod_w[None], final_mod_b[None])[0]

    a_wqkv, a_wo, b_wq, b_wo = (w.astype(BF16) for w in (a_wqkv, a_wo, b_wq, b_wo))
    kv_w = kv_w[None].astype(BF16)

    xt = x.reshape(t, d)
    shared = None
    for layer in range(DEPTH):
        sh1, sc1, g1, sh2, sc2, g2 = (per_batch(mod[layer, :, j * d:(j + 1) * d]) for j in range(6))
        if layer < N_A_LAYERS:
            lambda_init = 0.8 - 0.6 * math.exp(-0.3 * layer)
            segs = ((0, True, q_scale, (1,)), (1, True, 1.0, (1,)), (2, False, 1.0, (1,)))
            q, k, v = _norm_project(xt, mix_norm[layer], sh1, sc1, a_wqkv, layer,
                                    rope_tabs, segs, name="a_qkv_proj", **sizes)
            o = _diff_attention(q, k, v, a_lambda[layer], a_subln[layer], lambda_init, **sizes)
            wo, wo_layer = a_wo, layer
        else:
            j = layer - N_A_LAYERS
            segs = tuple((g, True, q_scale, (B_DILATIONS[g],)) for g in range(3))
            q0, q1, q2 = _norm_project(xt, mix_norm[layer], sh1, sc1, b_wq, j,
                                       rope_tabs, segs, name="b_q_proj", **sizes)
            o = _dilated_attention(q0, q1, q2, *shared, **sizes)
            wo, wo_layer = b_wo, j
        xt, h, idx, wgt, rank, tile_cnt = _mix_out_and_route(
            o, wo, wo_layer, xt, g1, ffn_norm[layer], sh2, sc2, router_w[layer],
            router_b[layer], **sizes)
        final_mod = None
        if layer == DEPTH - 1:
            final_mod = (final_norm, per_batch(fin_mod[:, :d]), per_batch(fin_mod[:, d:]))
        xt = _moe(h, idx, wgt, rank, tile_cnt, xt, g2, layer, exp_w_gate_up, exp_b_gate_up,
                  exp_w_down, exp_b_down, final_mod, seq=seq)
        if layer == N_A_LAYERS - 1:
            segs = ((0, True, 1.0, B_DILATIONS), (1, False, 1.0, B_DILATIONS))
            k0, k1, k2, v0, v1, v2 = _norm_project(
                xt, kv_norm, per_batch(kv_mod[:, :d]), per_batch(kv_mod[:, d:]),
                kv_w, 0, rope_tabs, segs, name="shared_kv_proj", **sizes)
            shared = (k0, v0, k1, v1, k2, v2)
    return xt.reshape(n_b, seq, d)
```

```python
import functools
import math

import jax
import jax.numpy as jnp
from jax import lax
from jax.experimental import pallas as pl
from jax.experimental.pallas import tpu as pltpu

F32 = jnp.float32
BF16 = jnp.bfloat16

D_MODEL = 1024
HEAD_DIM = 64
A_HEADS = 8
B_KV_HEADS = 16
B_DILATIONS = (1, 4, 16)
WINDOW_STEPS = 128
ROPE_THETA = 500000.0
ROPE_DIM = HEAD_DIM // 4
N_EXPERTS = 32
TOP_K = 4
SWIGLU_LIMIT = 7.0
SWIGLU_ALPHA = 1.702
NORM_EPS = 1e-5
N_A_LAYERS = 2
DEPTH = 4

LANES = 128
BF16_SUBLANES = 16
EXPERT_ROWS = 1024
EXPERT_PART = 256
ROUTE_TILE = 512
GROUP_ALIGN = 8
COMPACT_ROWS = ROUTE_TILE * TOP_K + N_EXPERTS * GROUP_ALIGN
COMBINE_CHUNK = COMPACT_ROWS
BIG_COPY = 32
MAX_BIG = COMPACT_ROWS // BIG_COPY
MAX_SMALL = N_EXPERTS * (BIG_COPY // GROUP_ALIGN - 1)
ROW_BITS = 12
FILL_SIZES = (256, 32, GROUP_ALIGN)
DISPATCH_BUFFERS = 3
NEG_BIG = -1e30
V7X_VMEM_BYTES = 64 * 1024 * 1024
VMEM_LIMIT = V7X_VMEM_BYTES // 8 * 7


def _params(sem, vmem=VMEM_LIMIT):
    return pltpu.CompilerParams(dimension_semantics=sem, vmem_limit_bytes=vmem)


def _adaln_kernel(c_ref, w_ref, b_ref, o_ref):
    c = c_ref[...]
    c_act = (c * jax.nn.sigmoid(c)).astype(BF16)
    o_ref[0] = jnp.dot(c_act, w_ref[0].astype(BF16), preferred_element_type=F32) + b_ref[0]


def _adaln_vectors(c, w, b):
    n_l, d, n = w.shape
    n_b = c.shape[0]
    tn = 1024
    return pl.pallas_call(
        _adaln_kernel,
        grid=(n_l, n // tn),
        in_specs=[
            pl.BlockSpec((n_b, d), lambda l, j: (0, 0)),
            pl.BlockSpec((1, d, tn), lambda l, j: (l, 0, j)),
            pl.BlockSpec((1, 1, tn), lambda l, j: (l, 0, j)),
        ],
        out_specs=pl.BlockSpec((1, n_b, tn), lambda l, j: (l, 0, j)),
        out_shape=jax.ShapeDtypeStruct((n_l, n_b, n), F32),
        compiler_params=_params(("parallel", "parallel")),
        name="adaln_vectors",
    )(c, w, b.reshape(n_l, 1, n))


def _modulated_norm(x, gain, shift, scale):
    y = x * lax.rsqrt(jnp.mean(x * x, axis=-1, keepdims=True) + NORM_EPS)
    return y * gain * (1.0 + scale) + shift


def _rope_tables(seq):
    inv = ROPE_THETA ** (-jnp.arange(0, ROPE_DIM, 2, dtype=F32) / ROPE_DIM)
    ang = jnp.arange(seq, dtype=F32)[:, None] * inv[None, :]
    cos, sin = jnp.cos(ang), jnp.sin(ang)
    half = ROPE_DIM // 2
    rest = HEAD_DIM - ROPE_DIM
    zeros = jnp.zeros((seq, half), F32)
    c_tab = jnp.concatenate([cos, cos, jnp.ones((seq, rest), F32)], axis=1)
    s1_tab = jnp.concatenate([zeros, sin, jnp.zeros((seq, rest), F32)], axis=1)
    s2_tab = jnp.concatenate([-sin, zeros, jnp.zeros((seq, rest), F32)], axis=1)
    rep = LANES // HEAD_DIM
    return tuple(jnp.tile(t, (1, rep)) for t in (c_tab, s1_tab, s2_tab))


def _apply_rope(y, c_tab, s1_tab, s2_tab):
    parts = []
    for j in range(y.shape[1] // LANES):
        yj = y[:, j * LANES:(j + 1) * LANES]
        parts.append(yj * c_tab + pltpu.roll(yj, ROPE_DIM // 2, 1) * s1_tab
                     + pltpu.roll(yj, LANES - ROPE_DIM // 2, 1) * s2_tab)
    return jnp.concatenate(parts, axis=1)


def _proj_kernel(x_ref, gain_ref, shift_ref, scale_ref, w_ref, rc_ref, rs1_ref, rs2_ref, *rest,
                 segs, tm):
    n_out = sum(len(s[3]) for s in segs)
    out_refs, scr_ref = rest[:n_out], rest[n_out]
    h = _modulated_norm(x_ref[...], gain_ref[...], shift_ref[0], scale_ref[0]).astype(BF16)
    oi = 0
    for chunk, rope, mult, dils in segs:
        y = jnp.dot(h, w_ref[:, chunk * D_MODEL:(chunk + 1) * D_MODEL], preferred_element_type=F32)
        if rope:
            y = _apply_rope(y, rc_ref[...], rs1_ref[...], rs2_ref[...])
        if mult != 1.0:
            y = y * mult
        for dil in dils:
            o_ref = out_refs[oi]
            oi += 1
            if dil == 1:
                o_ref[...] = y.astype(BF16)
            else:
                for j in range(D_MODEL // LANES):
                    scr_ref[j] = y[:, j * LANES:(j + 1) * LANES]
                for r in range(dil):
                    for j in range(D_MODEL // LANES):
                        o_ref[0, r, :, j * LANES:(j + 1) * LANES] = (
                            scr_ref[j, pl.ds(r, tm // dil, stride=dil), :].astype(BF16))


def _norm_project(x, gain, shift, scale, w, layer, rope_tabs, segs, *, n_b, seq, name):
    t, d = x.shape
    tm = 512
    n_s = seq // tm
    out_shapes, out_specs = [], []
    for _, _, _, dils in segs:
        for dil in dils:
            if dil == 1:
                out_shapes.append(jax.ShapeDtypeStruct((t, D_MODEL), BF16))
                out_specs.append(pl.BlockSpec((tm, D_MODEL), lambda i: (i, 0)))
            else:
                out_shapes.append(jax.ShapeDtypeStruct((n_b, dil, seq // dil, D_MODEL), BF16))
                out_specs.append(pl.BlockSpec((1, dil, tm // dil, D_MODEL),
                                              lambda i: (i // n_s, 0, i % n_s, 0)))
    per_batch = pl.BlockSpec((1, 1, d), lambda i: (i // n_s, 0, 0))
    rope_spec = pl.BlockSpec((tm, LANES), lambda i: (i % n_s, 0))
    return pl.pallas_call(
        functools.partial(_proj_kernel, segs=segs, tm=tm),
        grid=(t // tm,),
        in_specs=[
            pl.BlockSpec((tm, d), lambda i: (i, 0)),
            pl.BlockSpec((1, d), lambda i: (0, 0)),
            per_batch, per_batch,
            pl.BlockSpec((None,) + w.shape[1:], lambda i: (layer, 0, 0)),
            rope_spec, rope_spec, rope_spec,
        ],
        out_specs=out_specs,
        out_shape=out_shapes,
        scratch_shapes=[pltpu.VMEM((D_MODEL // LANES, tm, LANES), F32)],
        compiler_params=_params(("parallel",)),
        name=name,
    )(x, gain.reshape(1, d), shift, scale, w, *rope_tabs)


def _diff_attn_kernel(lam_ref, subln_ref, q_ref, k_ref, v_ref, o_ref, *, tq, lambda_init):
    lf = lam_ref[...]
    lam = (jnp.exp(jnp.sum(lf[0:1] * lf[1:2], keepdims=True))
           - jnp.exp(jnp.sum(lf[2:3] * lf[3:4], keepdims=True)) + lambda_init)
    lane_lo = lax.broadcasted_iota(jnp.int32, (tq, 2 * HEAD_DIM), 1) < HEAD_DIM
    causal = (lax.broadcasted_iota(jnp.int32, (tq, tq), 0)
              <= lax.broadcasted_iota(jnp.int32, (tq, tq), 1))

    def scores(kb, qm):
        return lax.dot_general(kb, qm, (((1,), (1,)), ((), ())), preferred_element_type=F32)

    def online(s, m, acc, vbt):
        m_new = jnp.maximum(m, jnp.max(s, axis=0, keepdims=True))
        p = jnp.exp2(s - m_new)
        alpha = jnp.exp2(m - m_new)
        acc = alpha * acc + jnp.dot(vbt, p.astype(BF16), preferred_element_type=F32)
        return m_new, acc

    d_v = 2 * HEAD_DIM
    ones_rows = jnp.ones((BF16_SUBLANES, tq), BF16)
    for i in range(q_ref.shape[0] // tq):
        q = q_ref[i * tq:(i + 1) * tq, :]
        zero = jnp.zeros_like(q)
        q1 = jnp.where(lane_lo, q, zero)
        q2 = jnp.where(lane_lo, zero, q)
        m1 = m2 = jnp.full((1, tq), NEG_BIG, F32)
        a1 = a2 = jnp.zeros((d_v + BF16_SUBLANES, tq), F32)
        for j in range(i + 1):
            kb = k_ref[j * tq:(j + 1) * tq, :]
            vbt = jnp.concatenate([v_ref[j * tq:(j + 1) * tq, :].T, ones_rows], axis=0)
            s1 = scores(kb, q1)
            s2 = scores(kb, q2)
            if j == i:
                s1 = jnp.where(causal, s1, NEG_BIG)
                s2 = jnp.where(causal, s2, NEG_BIG)
            m1, a1 = online(s1, m1, a1, vbt)
            m2, a2 = online(s2, m2, a2, vbt)
        o = (a1[:d_v] / a1[d_v:d_v + 1] - lam * (a2[:d_v] / a2[d_v:d_v + 1])).T
        o = o * lax.rsqrt(jnp.mean(o * o, axis=-1, keepdims=True) + NORM_EPS)
        o_ref[i * tq:(i + 1) * tq, :] = (o * subln_ref[...] * (1.0 - lambda_init)).astype(BF16)


def _diff_attention(q, k, v, lam_vecs, subln, lambda_init, *, n_b, seq):
    t = q.shape[0]
    width = 2 * HEAD_DIM
    head_seq = pl.BlockSpec((seq, width), lambda b, h: (b, h))
    return pl.pallas_call(
        functools.partial(_diff_attn_kernel, tq=512, lambda_init=lambda_init),
        grid=(n_b, A_HEADS),
        in_specs=[
            pl.BlockSpec((4, HEAD_DIM), lambda b, h: (0, 0)),
            pl.BlockSpec((1, width), lambda b, h: (0, 0)),
            head_seq, head_seq, head_seq,
        ],
        out_specs=head_seq,
        out_shape=jax.ShapeDtypeStruct((t, D_MODEL), BF16),
        compiler_params=_params(("parallel", "parallel")),
        name="diff_attention",
    )(lam_vecs, subln.reshape(1, width), q, k, v)


def _window_unit(q, kc, vc, bias, lane_lo):
    zero = jnp.zeros_like(q)
    qq = jnp.concatenate([jnp.where(lane_lo, q, zero), jnp.where(lane_lo, zero, q)], axis=0)
    s = lax.dot_general(qq, kc, (((1,), (1,)), ((), ())), preferred_element_type=F32) + bias
    m = jnp.max(s, axis=-1, keepdims=True)
    p = jnp.exp2(s - m).astype(BF16)
    v_aug = jnp.concatenate([vc, jnp.ones(vc.shape, BF16)], axis=1)
    o_aug = jnp.dot(p, v_aug, preferred_element_type=F32)
    n = WINDOW_STEPS
    o = jnp.where(lane_lo, o_aug[:n, :LANES], o_aug[n:, :LANES])
    l = jnp.where(lane_lo, o_aug[:n, LANES:], o_aug[n:, LANES:])
    m_sel = jnp.where(lane_lo, m[:n], m[n:])
    return o / l, m_sel + jnp.log2(l)


def _dil_attn_kernel(q0_ref, q1_ref, q2_ref, k0_ref, v0_ref, k1_ref, v1_ref, k2_ref, v2_ref,
                     o_ref, acc_ref, lse_ref, *, seq):
    n = WINDOW_STEPS
    lane_lo = lax.broadcasted_iota(jnp.int32, (n, LANES), 1) < HEAD_DIM
    qi = lax.broadcasted_iota(jnp.int32, (2 * n, 2 * n), 0) % n
    kj = lax.broadcasted_iota(jnp.int32, (2 * n, 2 * n), 1)
    band = jnp.where((kj >= qi) & (kj <= qi + n), 0.0, NEG_BIG)
    first = jnp.where(lax.broadcasted_iota(jnp.int32, (2 * n, n), 1)
                      <= lax.broadcasted_iota(jnp.int32, (2 * n, n), 0) % n, 0.0, NEG_BIG)

    def unit(g, q_rows, k_rows, v_rows, bias, dst):
        o, lse = _window_unit(q_rows, k_rows, v_rows, bias, lane_lo)
        acc_ref[g, dst, :] = o
        lse_ref[g, dst, :] = lse

    def rows(ref, r, lo, hi):
        return ref[lo:hi, :] if len(ref.shape) == 2 else ref[0, r, lo:hi, :]

    for g, (q_ref, k_ref, v_ref) in enumerate(((q0_ref, k0_ref, v0_ref), (q1_ref, k1_ref, v1_ref),
                                               (q2_ref, k2_ref, v2_ref))):
        dil = B_DILATIONS[g]
        for r in range(dil):
            for blk in range(seq // dil // n):
                dst = pl.ds(r + blk * n * dil, n, stride=dil) if dil > 1 else pl.ds(blk * n, n)
                k_lo = max(blk - 1, 0) * n
                unit(g, rows(q_ref, r, blk * n, (blk + 1) * n), rows(k_ref, r, k_lo, (blk + 1) * n),
                     rows(v_ref, r, k_lo, (blk + 1) * n), first if blk == 0 else band, dst)

    l0, l1, l2 = lse_ref[0], lse_ref[1], lse_ref[2]
    top = jnp.maximum(jnp.maximum(l0, l1), l2)
    w0, w1, w2 = jnp.exp2(l0 - top), jnp.exp2(l1 - top), jnp.exp2(l2 - top)
    o = (w0 * acc_ref[0] + w1 * acc_ref[1] + w2 * acc_ref[2]) / (w0 + w1 + w2)
    o_ref[...] = o.astype(BF16)


def _dilated_attention(q0, q1, q2, k0, v0, k1, v1, k2, v2, *, n_b, seq):
    t = q0.shape[0]
    pairs = D_MODEL // LANES
    nat = pl.BlockSpec((seq, LANES), lambda b, p: (b, p))

    def res(dil):
        return pl.BlockSpec((1, dil, seq // dil, LANES), lambda b, p: (b, 0, 0, p))

    d1, d2 = B_DILATIONS[1], B_DILATIONS[2]
    return pl.pallas_call(
        functools.partial(_dil_attn_kernel, seq=seq),
        grid=(n_b, pairs),
        in_specs=[nat, res(d1), res(d2), nat, nat, res(d1), res(d1), res(d2), res(d2)],
        out_specs=nat,
        out_shape=jax.ShapeDtypeStruct((t, D_MODEL), BF16),
        scratch_shapes=[pltpu.VMEM((3, seq, LANES), F32), pltpu.VMEM((3, seq, LANES), F32)],
        compiler_params=_params(("parallel", "parallel")),
        name="dilated_attention",
    )(q0, q1, q2, k0, v0, k1, v1, k2, v2)


def _mix_out_kernel(o_ref, wo_ref, x_ref, gate_ref, gain_ref, shift_ref, scale_ref, rw_ref, rb_ref,
                    xn_ref, h_ref, idx_ref, wgt_ref, rank_ref, cnt_ref, *, tm):
    y = jnp.dot(o_ref[...], wo_ref[...], preferred_element_type=F32)
    xn = x_ref[...] + gate_ref[0] * y
    xn_ref[...] = xn
    h = _modulated_norm(xn, gain_ref[...], shift_ref[0], scale_ref[0])

    h_hi = h.astype(BF16)
    h_ref[...] = h_hi
    h_lo = (h - h_hi.astype(F32)).astype(BF16)
    rw = rw_ref[...]
    rw_hi = rw.astype(BF16)
    rw_lo = (rw - rw_hi.astype(F32)).astype(BF16)
    nt = (((1,), (1,)), ((), ()))
    logits = (lax.dot_general(rw_hi, h_hi, nt, preferred_element_type=F32)
              + lax.dot_general(rw_lo, h_hi, nt, preferred_element_type=F32)
              + lax.dot_general(rw_hi, h_lo, nt, preferred_element_type=F32)) + rb_ref[...]

    e_iota = lax.broadcasted_iota(jnp.int32, logits.shape, 0)
    work = logits
    sels, tops, idxs = [], [], []
    for _ in range(TOP_K):
        mk = jnp.max(work, axis=0, keepdims=True)
        ik = jnp.min(jnp.where(work == mk, e_iota, N_EXPERTS), axis=0, keepdims=True)
        sel = e_iota == ik
        work = jnp.where(sel, -jnp.inf, work)
        sels.append(sel)
        tops.append(mk)
        idxs.append(ik)
    exps = [jnp.exp(m - tops[0]) for m in tops]
    denom = exps[0] + exps[1] + exps[2] + exps[3]
    for k in range(TOP_K):
        idx_ref[k:k + 1, :] = idxs[k]
        wgt_ref[k:k + 1, :] = exps[k] / denom

    chosen = jnp.zeros(logits.shape, F32)
    for sel in sels:
        chosen = chosen + jnp.where(sel, 1.0, 0.0)
    before = (lax.broadcasted_iota(jnp.int32, (tm, tm), 0)
              < lax.broadcasted_iota(jnp.int32, (tm, tm), 1))
    upper = jnp.where(before, 1.0, 0.0).astype(BF16)
    prefix = jnp.dot(chosen.astype(BF16), upper, preferred_element_type=F32)
    for k in range(TOP_K):
        rank_ref[k:k + 1, :] = jnp.sum(jnp.where(sels[k], prefix, 0.0), axis=0,
                                       keepdims=True).astype(jnp.int32)
    counts = jnp.sum(chosen, axis=1, keepdims=True).astype(jnp.int32)
    cnt_ref[0] = jnp.broadcast_to(counts, cnt_ref.shape[1:])


def _mix_out_and_route(o, wo, layer, x, gate, gain, shift, scale, router_w, router_b, *, n_b, seq):
    t, d = x.shape
    tm = ROUTE_TILE
    n_s = seq // tm
    per_batch = pl.BlockSpec((1, 1, d), lambda i: (i // n_s, 0, 0))
    row = pl.BlockSpec((tm, d), lambda i: (i, 0))
    sel = pl.BlockSpec((TOP_K, tm), lambda i: (0, i))
    return pl.pallas_call(
        functools.partial(_mix_out_kernel, tm=tm),
        grid=(t // tm,),
        in_specs=[
            row,
            pl.BlockSpec((None,) + wo.shape[1:], lambda i: (layer, 0, 0)),
            row, per_batch,
            pl.BlockSpec((1, d), lambda i: (0, 0)),
            per_batch, per_batch,
            pl.BlockSpec((N_EXPERTS, d), lambda i: (0, 0)),
            pl.BlockSpec((N_EXPERTS, 1), lambda i: (0, 0)),
        ],
        out_specs=[row, row, sel, sel, sel,
                   pl.BlockSpec((1, N_EXPERTS, LANES), lambda i: (i, 0, 0))],
        out_shape=[
            jax.ShapeDtypeStruct((t, d), F32),
            jax.ShapeDtypeStruct((t, d), BF16),
            jax.ShapeDtypeStruct((TOP_K, t), jnp.int32),
            jax.ShapeDtypeStruct((TOP_K, t), F32),
            jax.ShapeDtypeStruct((TOP_K, t), jnp.int32),
            jax.ShapeDtypeStruct((t // tm, N_EXPERTS, LANES), jnp.int32),
        ],
        compiler_params=_params(("parallel",)),
        name="mix_out_route",
    )(o, wo, x, gate, gain.reshape(1, d), shift, scale, router_w.T, router_b.reshape(N_EXPERTS, 1))


HIGH_HALF = 0xFFFF0000


def _pack_bf16_pairs(x):
    c = x.shape[1] // 2
    lo = lax.bitcast_convert_type(x[:, :c], jnp.uint32)
    hi = lax.bitcast_convert_type(x[:, c:], jnp.uint32)
    return (lo >> 16) | (hi & jnp.uint32(HIGH_HALF))


def _unpack_bf16_pairs(w):
    lo = lax.bitcast_convert_type(w << 16, F32).astype(BF16)
    hi = lax.bitcast_convert_type(w & jnp.uint32(HIGH_HALF), F32).astype(BF16)
    return lo, hi


def _copy_tables(grp, toff, tstart):
    experts = jnp.arange(N_EXPERTS, dtype=jnp.int32)

    def words(count, first_row, rows, width):
        end = jnp.cumsum(count, axis=1)
        j = jnp.arange(width, dtype=jnp.int32)
        e_of = jnp.sum((end[:, None, :] <= j[None, :, None]).astype(jnp.int32), axis=2)
        onehot = e_of[:, :, None] == experts[None, None, :]

        def pick(a):
            return jnp.sum(jnp.where(onehot, a[:, None, :], 0), axis=2)

        row = pick(first_row) + (j[None, :] - pick(end - count)) * rows
        word = ((pick(tstart) + row) << ROW_BITS) | (pick(toff) + row)
        return word.reshape(-1).astype(jnp.int32), end[:, -1]

    n_big = grp // BIG_COPY
    big, total_big = words(n_big, jnp.zeros_like(grp), BIG_COPY, MAX_BIG)
    small, total_small = words((grp - n_big * BIG_COPY) // GROUP_ALIGN, n_big * BIG_COPY,
                               GROUP_ALIGN, MAX_SMALL)
    counts = jnp.stack([total_big, total_small, jnp.sum(grp, axis=1)], axis=1)
    return counts.reshape(-1).astype(jnp.int32), big, small


def _tile_copies(tile, cnt_ref, big_ref, small_ref, make_copy, wait):
    if wait:
        total = cnt_ref[3 * tile + 2]

        def wait_rows(rows):
            def body(j, carry):
                make_copy(0, 0, rows).wait()
                return carry
            return body

        lax.fori_loop(0, total // BIG_COPY, wait_rows(BIG_COPY), 0)
        lax.fori_loop(0, (total % BIG_COPY) // GROUP_ALIGN, wait_rows(GROUP_ALIGN), 0)
        return

    def start_rows(table_ref, width, rows):
        def body(j, carry):
            word = table_ref[tile * width + j]
            make_copy(pl.multiple_of(word & ((1 << ROW_BITS) - 1), GROUP_ALIGN),
                      pl.multiple_of(word >> ROW_BITS, GROUP_ALIGN), rows).start()
            return carry
        return body

    lax.fori_loop(0, cnt_ref[3 * tile], start_rows(big_ref, MAX_BIG, BIG_COPY), 0)
    lax.fori_loop(0, cnt_ref[3 * tile + 1], start_rows(small_ref, MAX_SMALL, GROUP_ALIGN), 0)


def _dispatch_kernel(cnt_ref, big_ref, small_ref, fill_row_ref, fill_len_ref, nvalid_ref,
                     h_ref, idx_ref, rank_ref, wgt_ref, off_ref, xs_ref, loc_ref, z_ref, zero_ref,
                     sems, *, n_blocks):
    i = pl.program_id(0)
    last = pl.num_programs(0) - 1
    n_buf = z_ref.shape[0]
    slot = i % n_buf
    tm = h_ref.shape[0]

    def copies(tile, buf, wait):
        def make_copy(local, glob, n):
            return pltpu.make_async_copy(z_ref.at[buf, pl.ds(local, n)],
                                         xs_ref.at[pl.ds(glob, n)], sems.at[buf])

        _tile_copies(tile, cnt_ref, big_ref, small_ref, make_copy, wait)

    def fills(wait):
        def piece(row, n):
            cp = pltpu.make_async_copy(
                zero_ref.at[pl.ds(0, n)],
                xs_ref.at[pl.ds(pl.multiple_of(row, GROUP_ALIGN), n)], sems.at[n_buf])
            if wait:
                cp.wait()
            else:
                cp.start()

        def expert_pad(e, carry):
            row, n = fill_row_ref[e], fill_len_ref[e]
            for size in FILL_SIZES:
                count = n // size

                def body(j, c, row=row, size=size):
                    piece(row + j * size, size)
                    return c

                lax.fori_loop(0, count, body, 0)
                row, n = row + count * size, n - count * size
            return carry

        lax.fori_loop(0, N_EXPERTS, expert_pad, 0)

        def tail_piece(j, carry):
            piece(nvalid_ref[0] * EXPERT_ROWS + j * FILL_SIZES[0], FILL_SIZES[0])
            return carry

        lax.fori_loop(0, (n_blocks - nvalid_ref[0]) * (EXPERT_ROWS // FILL_SIZES[0]),
                      tail_piece, 0)

    @pl.when(i == 0)
    def _():
        zero_ref[...] = jnp.zeros_like(zero_ref)
        fills(wait=False)

    e_iota = lax.broadcasted_iota(jnp.int32, (N_EXPERTS, tm), 0)
    off = off_ref[0][:, 0:1]
    locs = []
    for k in range(TOP_K):
        sel = e_iota == idx_ref[k:k + 1, :]
        loc = jnp.sum(jnp.where(sel, off, 0), axis=0, keepdims=True) + rank_ref[k:k + 1, :]
        loc_ref[k:k + 1, :] = loc
        locs.append(loc)

    @pl.when(i >= n_buf)
    def _():
        copies(i - n_buf, slot, wait=True)

    h = h_ref[...]
    half = h.shape[1] // 2
    rows = 256
    for c in range(COMPACT_ROWS // rows):
        r_iota = lax.broadcasted_iota(jnp.int32, (rows, tm), 0) + c * rows
        gate = jnp.zeros((rows, tm), F32)
        for k, loc in enumerate(locs):
            gate = jnp.where(r_iota == loc, wgt_ref[k:k + 1, :], gate)
        hit = jnp.where(gate != 0.0, 1.0, 0.0)
        z_ref[slot, c * rows:(c + 1) * rows, :half] = _pack_bf16_pairs(
            jnp.dot(hit.astype(BF16), h, preferred_element_type=F32))
        row_gate = jnp.sum(gate, axis=1, keepdims=True)
        z_ref[slot, c * rows:(c + 1) * rows, half:] = lax.bitcast_convert_type(
            jnp.broadcast_to(row_gate, (rows, LANES)), jnp.uint32)

    copies(i, slot, wait=False)

    @pl.when(i == last)
    def _():
        for back in range(n_buf - 1, 0, -1):
            @pl.when(i >= back)
            def _():
                copies(i - back, (i - back) % n_buf, wait=True)

        copies(i, slot, wait=True)
        fills(wait=True)


def _dispatch(h, idx, rank, wgt, off_b, tables, fills, n_rows):
    t, d = h.shape
    tm = ROUTE_TILE
    width = d // 2 + LANES
    sel = pl.BlockSpec((TOP_K, tm), lambda i, *_: (0, i))
    grid_spec = pltpu.PrefetchScalarGridSpec(
        num_scalar_prefetch=6,
        grid=(t // tm,),
        in_specs=[
            pl.BlockSpec((tm, d), lambda i, *_: (i, 0)),
            sel, sel, sel,
            pl.BlockSpec((1, N_EXPERTS, LANES), lambda i, *_: (i, 0, 0)),
        ],
        out_specs=[pl.BlockSpec(memory_space=pl.ANY), sel],
        scratch_shapes=[pltpu.VMEM((DISPATCH_BUFFERS, COMPACT_ROWS, width), jnp.uint32),
                        pltpu.VMEM((FILL_SIZES[0], width), jnp.uint32),
                        pltpu.SemaphoreType.DMA((DISPATCH_BUFFERS + 1,))],
    )
    return pl.pallas_call(
        functools.partial(_dispatch_kernel, n_blocks=n_rows // EXPERT_ROWS),
        grid_spec=grid_spec,
        out_shape=[jax.ShapeDtypeStruct((n_rows, width), jnp.uint32),
                   jax.ShapeDtypeStruct((TOP_K, t), jnp.int32)],
        compiler_params=_params(("arbitrary",)),
        name="moe_dispatch",
    )(*tables, *fills, h, idx, rank, wgt, off_b)


def _expert_kernel(be_ref, first_ref, live_ref, src_ref, nvalid_ref, xs_ref, wgu_ref, bgu_ref,
                   wd_ref, bd_ref, ys_ref, wgu_bf, wd_bf):
    del be_ref, src_ref, nvalid_ref
    b = pl.program_id(0)
    bm = xs_ref.shape[0]

    @pl.when(first_ref[b] == 1)
    def _():
        wgu_bf[...] = wgu_ref[0].astype(BF16)
        wd_bf[...] = wd_ref[0].astype(BF16)

    def ffn(rows):
        d_ff = wd_bf.shape[0]
        half = wd_bf.shape[1] // 2
        x = jnp.concatenate(_unpack_bf16_pairs(xs_ref[:rows, :half]), axis=1)
        row_gate = lax.bitcast_convert_type(xs_ref[:rows, half:], F32)
        gu = jnp.dot(x, wgu_bf[...], preferred_element_type=F32) + bgu_ref[0]
        gate = jnp.minimum(gu[:, :d_ff], SWIGLU_LIMIT)
        up = jnp.clip(gu[:, d_ff:], -SWIGLU_LIMIT, SWIGLU_LIMIT)
        glu = gate * jax.nn.sigmoid(SWIGLU_ALPHA * gate)
        act = ((up + 1.0) * glu).astype(BF16)
        y = jnp.dot(act, wd_bf[...], preferred_element_type=F32) + bd_ref[0]
        y = y * jnp.concatenate([row_gate] * (y.shape[1] // LANES), axis=1)
        ys_ref[:rows, :] = _pack_bf16_pairs(y.astype(BF16).astype(F32))

    for parts in range(bm // EXPERT_PART + 1):
        @pl.when(live_ref[b] == parts)
        def _():
            rows = parts * EXPERT_PART
            if rows > 0:
                ffn(rows)
            if rows < bm:
                ys_ref[rows:, :] = jnp.zeros((bm - rows, ys_ref.shape[1]), ys_ref.dtype)


def _experts(xs, block_expert, block_first, block_live, block_src, n_valid, layer,
             w_gu, b_gu, w_d, b_d):
    n_rows, in_width = xs.shape
    bm = EXPERT_ROWS
    d, d_ff = w_d.shape[3], w_d.shape[2]
    half = d // 2
    n_l = w_gu.shape[0]
    grid_spec = pltpu.PrefetchScalarGridSpec(
        num_scalar_prefetch=5,
        grid=(n_rows // bm,),
        in_specs=[
            pl.BlockSpec((bm, in_width),
                         lambda b, be, fi, lv, src, nv: (src[jnp.minimum(b, nv[0] - 1)], 0)),
            pl.BlockSpec((None, 1, d, 2 * d_ff), lambda b, be, *_: (layer, be[b], 0, 0)),
            pl.BlockSpec((None, 1, 1, 2 * d_ff), lambda b, be, *_: (layer, be[b], 0, 0)),
            pl.BlockSpec((None, 1, d_ff, d), lambda b, be, *_: (layer, be[b], 0, 0)),
            pl.BlockSpec((None, 1, 1, d), lambda b, be, *_: (layer, be[b], 0, 0)),
        ],
        out_specs=pl.BlockSpec((bm, half), lambda b, be, fi, lv, src, nv: (src[b], 0)),
        scratch_shapes=[pltpu.VMEM((d, 2 * d_ff), BF16), pltpu.VMEM((d_ff, d), BF16)],
    )
    return pl.pallas_call(
        _expert_kernel,
        grid_spec=grid_spec,
        out_shape=jax.ShapeDtypeStruct((n_rows, half), jnp.uint32),
        compiler_params=_params(("arbitrary",)),
        name="moe_experts",
    )(block_expert, block_first, block_live, block_src, n_valid, xs, w_gu,
      b_gu.reshape(n_l, N_EXPERTS, 1, 2 * d_ff), w_d, b_d.reshape(n_l, N_EXPERTS, 1, d))


def _combine_kernel(cnt_ref, big_ref, small_ref, loc_ref, ys_ref, x_ref, gate_ref,
                    *rest, final):
    fin_refs, (o_ref, y_ref, sems) = rest[:-3], rest[-3:]
    i = pl.program_id(0)
    slot = i % 2
    tm = x_ref.shape[0]

    def copies(tile, buf, wait):
        def make_copy(local, glob, n):
            return pltpu.make_async_copy(ys_ref.at[pl.ds(glob, n)],
                                         y_ref.at[buf, pl.ds(local, n)], sems.at[buf])

        _tile_copies(tile, cnt_ref, big_ref, small_ref, make_copy, wait)

    @pl.when(i == 0)
    def _():
        y_ref[...] = jnp.zeros_like(y_ref)
        copies(0, 0, wait=False)

    copies(i, slot, wait=True)

    @pl.when(i + 1 < pl.num_programs(0))
    def _():
        copies(i + 1, 1 - slot, wait=False)

    rows = COMBINE_CHUNK
    half = y_ref.shape[2]
    f_lo = jnp.zeros((tm, half), F32)
    f_hi = jnp.zeros((tm, half), F32)
    for c in range(COMPACT_ROWS // rows):
        c_iota = lax.broadcasted_iota(jnp.int32, (tm, rows), 1) + c * rows
        q = jnp.zeros((tm, rows), F32)
        for k in range(TOP_K):
            q = jnp.where(c_iota == loc_ref[:, k:k + 1], 1.0, q)
        q = q.astype(BF16)
        y_lo, y_hi = _unpack_bf16_pairs(y_ref[slot, c * rows:(c + 1) * rows, :])
        f_lo = f_lo + jnp.dot(q, y_lo, preferred_element_type=F32)
        f_hi = f_hi + jnp.dot(q, y_hi, preferred_element_type=F32)
    out = x_ref[...] + gate_ref[0] * jnp.concatenate([f_lo, f_hi], axis=1)
    if final:
        gain_ref, shift_ref, scale_ref = fin_refs
        out = _modulated_norm(out, gain_ref[...], shift_ref[0], scale_ref[0])
    o_ref[...] = out


def _combine(loc_tk, ys, x, gate, tables, final_mod, *, seq):
    t, d = x.shape
    tm = ROUTE_TILE
    n_s = seq // tm
    sel = pl.BlockSpec((tm, TOP_K), lambda i, *_: (i, 0))
    per_batch = pl.BlockSpec((1, 1, d), lambda i, *_: (i // n_s, 0, 0))
    in_specs = [sel, pl.BlockSpec(memory_space=pl.ANY),
                pl.BlockSpec((tm, d), lambda i, *_: (i, 0)), per_batch]
    extra = ()
    if final_mod is not None:
        gain, shift, scale = final_mod
        extra = (gain.reshape(1, d), shift, scale)
        in_specs += [pl.BlockSpec((1, d), lambda i, *_: (0, 0)), per_batch, per_batch]
    grid_spec = pltpu.PrefetchScalarGridSpec(
        num_scalar_prefetch=3,
        grid=(t // tm,),
        in_specs=in_specs,
        out_specs=pl.BlockSpec((tm, d), lambda i, *_: (i, 0)),
        scratch_shapes=[pltpu.VMEM((2, COMPACT_ROWS, d // 2), jnp.uint32),
                        pltpu.SemaphoreType.DMA((2,))],
    )
    return pl.pallas_call(
        functools.partial(_combine_kernel, final=final_mod is not None),
        grid_spec=grid_spec,
        out_shape=jax.ShapeDtypeStruct((t, d), F32),
        compiler_params=_params(("arbitrary",)),
        name="moe_combine",
    )(*tables, loc_tk, ys, x, gate, *extra)


def _moe(h, idx, wgt, rank, tile_cnt, x, gate, layer, w_gu, b_gu, w_d, b_d, final_mod, *, seq):
    t = h.shape[0]
    bm = EXPERT_ROWS
    n_tiles = t // ROUTE_TILE
    n_rows = t * TOP_K + n_tiles * N_EXPERTS * GROUP_ALIGN + N_EXPERTS * bm
    n_blocks = n_rows // bm
    cnt = tile_cnt[:, :, 0]
    grp = (cnt + GROUP_ALIGN - 1) // GROUP_ALIGN * GROUP_ALIGN
    expert_rows = jnp.sum(grp, axis=0)
    padded = (expert_rows + bm - 1) // bm * bm
    pad_end = jnp.cumsum(padded)
    tstart = (pad_end - padded)[None, :] + jnp.cumsum(grp, axis=0) - grp
    toff = jnp.cumsum(grp, axis=1) - grp
    tables = _copy_tables(grp, toff, tstart)
    off_b = jnp.broadcast_to(toff[:, :, None], (n_tiles, N_EXPERTS, LANES)).astype(jnp.int32)
    block_row = jnp.arange(n_blocks, dtype=jnp.int32) * bm
    block_expert = jnp.minimum(
        jnp.sum((pad_end[None, :] <= block_row[:, None]).astype(jnp.int32), axis=1),
        N_EXPERTS - 1).astype(jnp.int32)
    block_first = jnp.concatenate(
        [jnp.ones((1,), jnp.int32), (block_expert[1:] != block_expert[:-1]).astype(jnp.int32)])
    n_valid = (pad_end[-1:] // bm).astype(jnp.int32)
    in_expert = block_expert[:, None] == jnp.arange(N_EXPERTS, dtype=jnp.int32)[None, :]

    def of_expert(per_expert):
        return jnp.sum(jnp.where(in_expert, per_expert[None, :], 0), axis=1)

    first_block = of_expert((pad_end - padded) // bm)
    n_own = jnp.maximum(of_expert(padded // bm), 1)
    past_end = block_row >= pad_end[-1]
    block_id = jnp.arange(n_blocks, dtype=jnp.int32)
    block_src = jnp.where(past_end, block_id,
                          first_block + (block_id - first_block - 1) % n_own).astype(jnp.int32)
    rows_left = of_expert(pad_end - padded + expert_rows) - block_src * bm
    block_live = jnp.where(past_end, 0,
                           jnp.clip((rows_left + EXPERT_PART - 1) // EXPERT_PART, 1,
                                    bm // EXPERT_PART)).astype(jnp.int32)
    fills = ((pad_end - padded + expert_rows).astype(jnp.int32),
             (padded - expert_rows).astype(jnp.int32), n_valid)
    xs, loc = _dispatch(h, idx, rank, wgt, off_b, tables, fills, n_rows)
    ys = _experts(xs, block_expert, block_first, block_live, block_src, n_valid, layer,
                  w_gu, b_gu, w_d, b_d)
    return _combine(loc.T, ys, x, gate, tables, final_mod, seq=seq)


def kernel(x, c, mod_w, mod_b, mix_norm, ffn_norm, a_wqkv, a_wo, a_lambda, a_subln, kv_norm, kv_mod_w, kv_mod_b, kv_w, b_wq, b_wo, router_w, router_b, exp_w_gate_up, exp_b_gate_up, exp_w_down, exp_b_down, final_norm, final_mod_w, final_mod_b):
    n_b, seq, d = x.shape
    t = n_b * seq
    sizes = dict(n_b=n_b, seq=seq)
    rope_tabs = _rope_tables(seq)
    q_scale = HEAD_DIM ** -0.5 * math.log2(math.e)

    def per_batch(v):
        return v.reshape(n_b, 1, d)

    mod = _adaln_vectors(c, mod_w, mod_b)
    kv_mod = _adaln_vectors(c, kv_mod_w[None], kv_mod_b[None])[0]
    fin_mod = _adaln_vectors(c, final_mod_w[None], final_mod_b[None])[0]

    a_wqkv, a_wo, b_wq, b_wo = (w.astype(BF16) for w in (a_wqkv, a_wo, b_wq, b_wo))
    kv_w = kv_w[None].astype(BF16)

    xt = x.reshape(t, d)
    shared = None
    for layer in range(DEPTH):
        sh1, sc1, g1, sh2, sc2, g2 = (per_batch(mod[layer, :, j * d:(j + 1) * d]) for j in range(6))
        if layer < N_A_LAYERS:
            lambda_init = 0.8 - 0.6 * math.exp(-0.3 * layer)
            segs = ((0, True, q_scale, (1,)), (1, True, 1.0, (1,)), (2, False, 1.0, (1,)))
            q, k, v = _norm_project(xt, mix_norm[layer], sh1, sc1, a_wqkv, layer,
                                    rope_tabs, segs, name="a_qkv_proj", **sizes)
            o = _diff_attention(q, k, v, a_lambda[layer], a_subln[layer], lambda_init, **sizes)
            wo, wo_layer = a_wo, layer
        else:
            j = layer - N_A_LAYERS
            segs = tuple((g, True, q_scale, (B_DILATIONS[g],)) for g in range(3))
            q0, q1, q2 = _norm_project(xt, mix_norm[layer], sh1, sc1, b_wq, j,
                                       rope_tabs, segs, name="b_q_proj", **sizes)
            o = _dilated_attention(q0, q1, q2, *shared, **sizes)
            wo, wo_layer = b_wo, j
        xt, h, idx, wgt, rank, tile_cnt = _mix_out_and_route(
            o, wo, wo_layer, xt, g1, ffn_norm[layer], sh2, sc2, router_w[layer],
            router_b[layer], **sizes)
        final_mod = None
        if layer == DEPTH - 1:
            final_mod = (final_norm, per_batch(fin_mod[:, :d]), per_batch(fin_mod[:, d:]))
        xt = _moe(h, idx, wgt, rank, tile_cnt, xt, g2, layer, exp_w_gate_up, exp_b_gate_up,
                  exp_w_down, exp_b_down, final_mod, seq=seq)
        if layer == N_A_LAYERS - 1:
            segs = ((0, True, 1.0, B_DILATIONS), (1, False, 1.0, B_DILATIONS))
            k0, k1, k2, v0, v1, v2 = _norm_project(
                xt, kv_norm, per_batch(kv_mod[:, :d]), per_batch(kv_mod[:, d:]),
                kv_w, 0, rope_tabs, segs, name="shared_kv_proj", **sizes)
            shared = (k0, v0, k1, v1, k2, v2)
    return xt.reshape(n_b, seq, d)
```

```python
import functools
import math

import jax
import jax.numpy as jnp
from jax import lax
from jax.experimental import pallas as pl
from jax.experimental.pallas import tpu as pltpu

F32 = jnp.float32
BF16 = jnp.bfloat16

D_MODEL = 1024
HEAD_DIM = 64
A_HEADS = 8
B_KV_HEADS = 16
B_DILATIONS = (1, 4, 16)
WINDOW_STEPS = 128
ROPE_THETA = 500000.0
ROPE_DIM = HEAD_DIM // 4
N_EXPERTS = 32
TOP_K = 4
SWIGLU_LIMIT = 7.0
SWIGLU_ALPHA = 1.702
NORM_EPS = 1e-5
N_A_LAYERS = 2
DEPTH = 4

LANES = 128
BF16_SUBLANES = 16
EXPERT_ROWS = 1024
EXPERT_PART = 256
ROUTE_TILE = 512
GROUP_ALIGN = 8
COMPACT_ROWS = ROUTE_TILE * TOP_K + N_EXPERTS * GROUP_ALIGN
COMBINE_CHUNK = COMPACT_ROWS
BIG_COPY = 32
MAX_BIG = COMPACT_ROWS // BIG_COPY
MAX_SMALL = N_EXPERTS * (BIG_COPY // GROUP_ALIGN - 1)
ROW_BITS = 12
FILL_SIZES = (256, 32, GROUP_ALIGN)
NEG_BIG = -1e30
V7X_VMEM_BYTES = 64 * 1024 * 1024
VMEM_LIMIT = V7X_VMEM_BYTES // 8 * 7


def _params(sem, vmem=VMEM_LIMIT):
    return pltpu.CompilerParams(dimension_semantics=sem, vmem_limit_bytes=vmem)


def _adaln_kernel(c_ref, w_ref, b_ref, o_ref):
    c = c_ref[...]
    c_act = (c * jax.nn.sigmoid(c)).astype(BF16)
    o_ref[0] = jnp.dot(c_act, w_ref[0].astype(BF16), preferred_element_type=F32) + b_ref[0]


def _adaln_vectors(c, w, b):
    n_l, d, n = w.shape
    n_b = c.shape[0]
    tn = 1024
    return pl.pallas_call(
        _adaln_kernel,
        grid=(n_l, n // tn),
        in_specs=[
            pl.BlockSpec((n_b, d), lambda l, j: (0, 0)),
            pl.BlockSpec((1, d, tn), lambda l, j: (l, 0, j)),
            pl.BlockSpec((1, 1, tn), lambda l, j: (l, 0, j)),
        ],
        out_specs=pl.BlockSpec((1, n_b, tn), lambda l, j: (l, 0, j)),
        out_shape=jax.ShapeDtypeStruct((n_l, n_b, n), F32),
        compiler_params=_params(("parallel", "parallel")),
        name="adaln_vectors",
    )(c, w, b.reshape(n_l, 1, n))


def _modulated_norm(x, gain, shift, scale):
    y = x * lax.rsqrt(jnp.mean(x * x, axis=-1, keepdims=True) + NORM_EPS)
    return y * gain * (1.0 + scale) + shift


def _rope_tables(seq):
    inv = ROPE_THETA ** (-jnp.arange(0, ROPE_DIM, 2, dtype=F32) / ROPE_DIM)
    ang = jnp.arange(seq, dtype=F32)[:, None] * inv[None, :]
    cos, sin = jnp.cos(ang), jnp.sin(ang)
    half = ROPE_DIM // 2
    rest = HEAD_DIM - ROPE_DIM
    zeros = jnp.zeros((seq, half), F32)
    c_tab = jnp.concatenate([cos, cos, jnp.ones((seq, rest), F32)], axis=1)
    s1_tab = jnp.concatenate([zeros, sin, jnp.zeros((seq, rest), F32)], axis=1)
    s2_tab = jnp.concatenate([-sin, zeros, jnp.zeros((seq, rest), F32)], axis=1)
    rep = LANES // HEAD_DIM
    return tuple(jnp.tile(t, (1, rep)) for t in (c_tab, s1_tab, s2_tab))


def _apply_rope(y, c_tab, s1_tab, s2_tab):
    parts = []
    for j in range(y.shape[1] // LANES):
        yj = y[:, j * LANES:(j + 1) * LANES]
        parts.append(yj * c_tab + pltpu.roll(yj, ROPE_DIM // 2, 1) * s1_tab
                     + pltpu.roll(yj, LANES - ROPE_DIM // 2, 1) * s2_tab)
    return jnp.concatenate(parts, axis=1)


def _proj_kernel(x_ref, gain_ref, shift_ref, scale_ref, w_ref, rc_ref, rs1_ref, rs2_ref, *rest,
                 segs, tm):
    n_out = sum(len(s[3]) for s in segs)
    out_refs, scr_ref = rest[:n_out], rest[n_out]
    h = _modulated_norm(x_ref[...], gain_ref[...], shift_ref[0], scale_ref[0]).astype(BF16)
    oi = 0
    for chunk, rope, mult, dils in segs:
        y = jnp.dot(h, w_ref[:, chunk * D_MODEL:(chunk + 1) * D_MODEL], preferred_element_type=F32)
        if rope:
            y = _apply_rope(y, rc_ref[...], rs1_ref[...], rs2_ref[...])
        if mult != 1.0:
            y = y * mult
        for dil in dils:
            o_ref = out_refs[oi]
            oi += 1
            if dil == 1:
                o_ref[...] = y.astype(BF16)
            else:
                for j in range(D_MODEL // LANES):
                    scr_ref[j] = y[:, j * LANES:(j + 1) * LANES]
                for r in range(dil):
                    for j in range(D_MODEL // LANES):
                        o_ref[0, r, :, j * LANES:(j + 1) * LANES] = (
                            scr_ref[j, pl.ds(r, tm // dil, stride=dil), :].astype(BF16))


def _norm_project(x, gain, shift, scale, w, layer, rope_tabs, segs, *, n_b, seq, name):
    t, d = x.shape
    tm = 512
    n_s = seq // tm
    out_shapes, out_specs = [], []
    for _, _, _, dils in segs:
        for dil in dils:
            if dil == 1:
                out_shapes.append(jax.ShapeDtypeStruct((t, D_MODEL), BF16))
                out_specs.append(pl.BlockSpec((tm, D_MODEL), lambda i: (i, 0)))
            else:
                out_shapes.append(jax.ShapeDtypeStruct((n_b, dil, seq // dil, D_MODEL), BF16))
                out_specs.append(pl.BlockSpec((1, dil, tm // dil, D_MODEL),
                                              lambda i: (i // n_s, 0, i % n_s, 0)))
    per_batch = pl.BlockSpec((1, 1, d), lambda i: (i // n_s, 0, 0))
    rope_spec = pl.BlockSpec((tm, LANES), lambda i: (i % n_s, 0))
    return pl.pallas_call(
        functools.partial(_proj_kernel, segs=segs, tm=tm),
        grid=(t // tm,),
        in_specs=[
            pl.BlockSpec((tm, d), lambda i: (i, 0)),
            pl.BlockSpec((1, d), lambda i: (0, 0)),
            per_batch, per_batch,
            pl.BlockSpec((None,) + w.shape[1:], lambda i: (layer, 0, 0)),
            rope_spec, rope_spec, rope_spec,
        ],
        out_specs=out_specs,
        out_shape=out_shapes,
        scratch_shapes=[pltpu.VMEM((D_MODEL // LANES, tm, LANES), F32)],
        compiler_params=_params(("parallel",)),
        name=name,
    )(x, gain.reshape(1, d), shift, scale, w, *rope_tabs)


def _diff_attn_kernel(lam_ref, subln_ref, q_ref, k_ref, v_ref, o_ref, *, tq, lambda_init):
    lf = lam_ref[...]
    lam = (jnp.exp(jnp.sum(lf[0:1] * lf[1:2], keepdims=True))
           - jnp.exp(jnp.sum(lf[2:3] * lf[3:4], keepdims=True)) + lambda_init)
    lane_lo = lax.broadcasted_iota(jnp.int32, (tq, 2 * HEAD_DIM), 1) < HEAD_DIM
    causal = (lax.broadcasted_iota(jnp.int32, (tq, tq), 0)
              <= lax.broadcasted_iota(jnp.int32, (tq, tq), 1))

    def scores(kb, qm):
        return lax.dot_general(kb, qm, (((1,), (1,)), ((), ())), preferred_element_type=F32)

    def online(s, m, acc, vbt):
        m_new = jnp.maximum(m, jnp.max(s, axis=0, keepdims=True))
        p = jnp.exp2(s - m_new)
        alpha = jnp.exp2(m - m_new)
        acc = alpha * acc + jnp.dot(vbt, p.astype(BF16), preferred_element_type=F32)
        return m_new, acc

    d_v = 2 * HEAD_DIM
    ones_rows = jnp.ones((BF16_SUBLANES, tq), BF16)
    for i in range(q_ref.shape[0] // tq):
        q = q_ref[i * tq:(i + 1) * tq, :]
        zero = jnp.zeros_like(q)
        q1 = jnp.where(lane_lo, q, zero)
        q2 = jnp.where(lane_lo, zero, q)
        m1 = m2 = jnp.full((1, tq), NEG_BIG, F32)
        a1 = a2 = jnp.zeros((d_v + BF16_SUBLANES, tq), F32)
        for j in range(i + 1):
            kb = k_ref[j * tq:(j + 1) * tq, :]
            vbt = jnp.concatenate([v_ref[j * tq:(j + 1) * tq, :].T, ones_rows], axis=0)
            s1 = scores(kb, q1)
            s2 = scores(kb, q2)
            if j == i:
                s1 = jnp.where(causal, s1, NEG_BIG)
                s2 = jnp.where(causal, s2, NEG_BIG)
            m1, a1 = online(s1, m1, a1, vbt)
            m2, a2 = online(s2, m2, a2, vbt)
        o = (a1[:d_v] / a1[d_v:d_v + 1] - lam * (a2[:d_v] / a2[d_v:d_v + 1])).T
        o = o * lax.rsqrt(jnp.mean(o * o, axis=-1, keepdims=True) + NORM_EPS)
        o_ref[i * tq:(i + 1) * tq, :] = (o * subln_ref[...] * (1.0 - lambda_init)).astype(BF16)


def _diff_attention(q, k, v, lam_vecs, subln, lambda_init, *, n_b, seq):
    t = q.shape[0]
    width = 2 * HEAD_DIM
    head_seq = pl.BlockSpec((seq, width), lambda b, h: (b, h))
    return pl.pallas_call(
        functools.partial(_diff_attn_kernel, tq=512, lambda_init=lambda_init),
        grid=(n_b, A_HEADS),
        in_specs=[
            pl.BlockSpec((4, HEAD_DIM), lambda b, h: (0, 0)),
            pl.BlockSpec((1, width), lambda b, h: (0, 0)),
            head_seq, head_seq, head_seq,
        ],
        out_specs=head_seq,
        out_shape=jax.ShapeDtypeStruct((t, D_MODEL), BF16),
        compiler_params=_params(("parallel", "parallel")),
        name="diff_attention",
    )(lam_vecs, subln.reshape(1, width), q, k, v)


def _window_unit(q, kc, vc, bias, lane_lo):
    zero = jnp.zeros_like(q)
    qq = jnp.concatenate([jnp.where(lane_lo, q, zero), jnp.where(lane_lo, zero, q)], axis=0)
    s = lax.dot_general(qq, kc, (((1,), (1,)), ((), ())), preferred_element_type=F32) + bias
    m = jnp.max(s, axis=-1, keepdims=True)
    p = jnp.exp2(s - m).astype(BF16)
    v_aug = jnp.concatenate([vc, jnp.ones(vc.shape, BF16)], axis=1)
    o_aug = jnp.dot(p, v_aug, preferred_element_type=F32)
    n = WINDOW_STEPS
    o = jnp.where(lane_lo, o_aug[:n, :LANES], o_aug[n:, :LANES])
    l = jnp.where(lane_lo, o_aug[:n, LANES:], o_aug[n:, LANES:])
    m_sel = jnp.where(lane_lo, m[:n], m[n:])
    return o / l, m_sel + jnp.log2(l)


def _dil_attn_kernel(q0_ref, q1_ref, q2_ref, k0_ref, v0_ref, k1_ref, v1_ref, k2_ref, v2_ref,
                     o_ref, acc_ref, lse_ref, *, seq):
    n = WINDOW_STEPS
    lane_lo = lax.broadcasted_iota(jnp.int32, (n, LANES), 1) < HEAD_DIM
    qi = lax.broadcasted_iota(jnp.int32, (2 * n, 2 * n), 0) % n
    kj = lax.broadcasted_iota(jnp.int32, (2 * n, 2 * n), 1)
    band = jnp.where((kj >= qi) & (kj <= qi + n), 0.0, NEG_BIG)
    first = jnp.where(lax.broadcasted_iota(jnp.int32, (2 * n, n), 1)
                      <= lax.broadcasted_iota(jnp.int32, (2 * n, n), 0) % n, 0.0, NEG_BIG)

    def unit(g, q_rows, k_rows, v_rows, bias, dst):
        o, lse = _window_unit(q_rows, k_rows, v_rows, bias, lane_lo)
        acc_ref[g, dst, :] = o
        lse_ref[g, dst, :] = lse

    def rows(ref, r, lo, hi):
        return ref[lo:hi, :] if len(ref.shape) == 2 else ref[0, r, lo:hi, :]

    for g, (q_ref, k_ref, v_ref) in enumerate(((q0_ref, k0_ref, v0_ref), (q1_ref, k1_ref, v1_ref),
                                               (q2_ref, k2_ref, v2_ref))):
        dil = B_DILATIONS[g]
        for r in range(dil):
            for blk in range(seq // dil // n):
                dst = pl.ds(r + blk * n * dil, n, stride=dil) if dil > 1 else pl.ds(blk * n, n)
                k_lo = max(blk - 1, 0) * n
                unit(g, rows(q_ref, r, blk * n, (blk + 1) * n), rows(k_ref, r, k_lo, (blk + 1) * n),
                     rows(v_ref, r, k_lo, (blk + 1) * n), first if blk == 0 else band, dst)

    l0, l1, l2 = lse_ref[0], lse_ref[1], lse_ref[2]
    top = jnp.maximum(jnp.maximum(l0, l1), l2)
    w0, w1, w2 = jnp.exp2(l0 - top), jnp.exp2(l1 - top), jnp.exp2(l2 - top)
    o = (w0 * acc_ref[0] + w1 * acc_ref[1] + w2 * acc_ref[2]) / (w0 + w1 + w2)
    o_ref[...] = o.astype(BF16)


def _dilated_attention(q0, q1, q2, k0, v0, k1, v1, k2, v2, *, n_b, seq):
    t = q0.shape[0]
    pairs = D_MODEL // LANES
    nat = pl.BlockSpec((seq, LANES), lambda b, p: (b, p))

    def res(dil):
        return pl.BlockSpec((1, dil, seq // dil, LANES), lambda b, p: (b, 0, 0, p))

    d1, d2 = B_DILATIONS[1], B_DILATIONS[2]
    return pl.pallas_call(
        functools.partial(_dil_attn_kernel, seq=seq),
        grid=(n_b, pairs),
        in_specs=[nat, res(d1), res(d2), nat, nat, res(d1), res(d1), res(d2), res(d2)],
        out_specs=nat,
        out_shape=jax.ShapeDtypeStruct((t, D_MODEL), BF16),
        scratch_shapes=[pltpu.VMEM((3, seq, LANES), F32), pltpu.VMEM((3, seq, LANES), F32)],
        compiler_params=_params(("parallel", "parallel")),
        name="dilated_attention",
    )(q0, q1, q2, k0, v0, k1, v1, k2, v2)


def _mix_out_kernel(o_ref, wo_ref, x_ref, gate_ref, gain_ref, shift_ref, scale_ref, rw_ref, rb_ref,
                    xn_ref, h_ref, idx_ref, wgt_ref, rank_ref, cnt_ref, *, tm):
    y = jnp.dot(o_ref[...], wo_ref[...], preferred_element_type=F32)
    xn = x_ref[...] + gate_ref[0] * y
    xn_ref[...] = xn
    h = _modulated_norm(xn, gain_ref[...], shift_ref[0], scale_ref[0])

    h_hi = h.astype(BF16)
    h_ref[...] = h_hi
    h_lo = (h - h_hi.astype(F32)).astype(BF16)
    rw = rw_ref[...]
    rw_hi = rw.astype(BF16)
    rw_lo = (rw - rw_hi.astype(F32)).astype(BF16)
    nt = (((1,), (1,)), ((), ()))
    logits = (lax.dot_general(rw_hi, h_hi, nt, preferred_element_type=F32)
              + lax.dot_general(rw_lo, h_hi, nt, preferred_element_type=F32)
              + lax.dot_general(rw_hi, h_lo, nt, preferred_element_type=F32)) + rb_ref[...]

    e_iota = lax.broadcasted_iota(jnp.int32, logits.shape, 0)
    work = logits
    sels, tops, idxs = [], [], []
    for _ in range(TOP_K):
        mk = jnp.max(work, axis=0, keepdims=True)
        ik = jnp.min(jnp.where(work == mk, e_iota, N_EXPERTS), axis=0, keepdims=True)
        sel = e_iota == ik
        work = jnp.where(sel, -jnp.inf, work)
        sels.append(sel)
        tops.append(mk)
        idxs.append(ik)
    exps = [jnp.exp(m - tops[0]) for m in tops]
    denom = exps[0] + exps[1] + exps[2] + exps[3]
    for k in range(TOP_K):
        idx_ref[k:k + 1, :] = idxs[k]
        wgt_ref[k:k + 1, :] = exps[k] / denom

    chosen = jnp.zeros(logits.shape, F32)
    for sel in sels:
        chosen = chosen + jnp.where(sel, 1.0, 0.0)
    before = (lax.broadcasted_iota(jnp.int32, (tm, tm), 0)
              < lax.broadcasted_iota(jnp.int32, (tm, tm), 1))
    upper = jnp.where(before, 1.0, 0.0).astype(BF16)
    prefix = jnp.dot(chosen.astype(BF16), upper, preferred_element_type=F32)
    for k in range(TOP_K):
        rank_ref[k:k + 1, :] = jnp.sum(jnp.where(sels[k], prefix, 0.0), axis=0,
                                       keepdims=True).astype(jnp.int32)
    counts = jnp.sum(chosen, axis=1, keepdims=True).astype(jnp.int32)
    cnt_ref[0] = jnp.broadcast_to(counts, cnt_ref.shape[1:])


def _mix_out_and_route(o, wo, layer, x, gate, gain, shift, scale, router_w, router_b, *, n_b, seq):
    t, d = x.shape
    tm = ROUTE_TILE
    n_s = seq // tm
    per_batch = pl.BlockSpec((1, 1, d), lambda i: (i // n_s, 0, 0))
    row = pl.BlockSpec((tm, d), lambda i: (i, 0))
    sel = pl.BlockSpec((TOP_K, tm), lambda i: (0, i))
    return pl.pallas_call(
        functools.partial(_mix_out_kernel, tm=tm),
        grid=(t // tm,),
        in_specs=[
            row,
            pl.BlockSpec((None,) + wo.shape[1:], lambda i: (layer, 0, 0)),
            row, per_batch,
            pl.BlockSpec((1, d), lambda i: (0, 0)),
            per_batch, per_batch,
            pl.BlockSpec((N_EXPERTS, d), lambda i: (0, 0)),
            pl.BlockSpec((N_EXPERTS, 1), lambda i: (0, 0)),
        ],
        out_specs=[row, row, sel, sel, sel,
                   pl.BlockSpec((1, N_EXPERTS, LANES), lambda i: (i, 0, 0))],
        out_shape=[
            jax.ShapeDtypeStruct((t, d), F32),
            jax.ShapeDtypeStruct((t, d), BF16),
            jax.ShapeDtypeStruct((TOP_K, t), jnp.int32),
            jax.ShapeDtypeStruct((TOP_K, t), F32),
            jax.ShapeDtypeStruct((TOP_K, t), jnp.int32),
            jax.ShapeDtypeStruct((t // tm, N_EXPERTS, LANES), jnp.int32),
        ],
        compiler_params=_params(("parallel",)),
        name="mix_out_route",
    )(o, wo, x, gate, gain.reshape(1, d), shift, scale, router_w.T, router_b.reshape(N_EXPERTS, 1))


HIGH_HALF = 0xFFFF0000


def _pack_bf16_pairs(x):
    c = x.shape[1] // 2
    lo = lax.bitcast_convert_type(x[:, :c], jnp.uint32)
    hi = lax.bitcast_convert_type(x[:, c:], jnp.uint32)
    return (lo >> 16) | (hi & jnp.uint32(HIGH_HALF))


def _unpack_bf16_pairs(w):
    lo = lax.bitcast_convert_type(w << 16, F32).astype(BF16)
    hi = lax.bitcast_convert_type(w & jnp.uint32(HIGH_HALF), F32).astype(BF16)
    return lo, hi


def _copy_tables(grp, toff, tstart):
    experts = jnp.arange(N_EXPERTS, dtype=jnp.int32)

    def words(count, first_row, rows, width):
        end = jnp.cumsum(count, axis=1)
        j = jnp.arange(width, dtype=jnp.int32)
        e_of = jnp.sum((end[:, None, :] <= j[None, :, None]).astype(jnp.int32), axis=2)
        onehot = e_of[:, :, None] == experts[None, None, :]

        def pick(a):
            return jnp.sum(jnp.where(onehot, a[:, None, :], 0), axis=2)

        row = pick(first_row) + (j[None, :] - pick(end - count)) * rows
        word = ((pick(tstart) + row) << ROW_BITS) | (pick(toff) + row)
        return word.reshape(-1).astype(jnp.int32), end[:, -1]

    n_big = grp // BIG_COPY
    big, total_big = words(n_big, jnp.zeros_like(grp), BIG_COPY, MAX_BIG)
    small, total_small = words((grp - n_big * BIG_COPY) // GROUP_ALIGN, n_big * BIG_COPY,
                               GROUP_ALIGN, MAX_SMALL)
    counts = jnp.stack([total_big, total_small, jnp.sum(grp, axis=1)], axis=1)
    return counts.reshape(-1).astype(jnp.int32), big, small


def _tile_copies(tile, cnt_ref, big_ref, small_ref, make_copy, wait, small_priority=0):
    if wait:
        total = cnt_ref[3 * tile + 2]

        def wait_rows(rows):
            def body(j, carry):
                make_copy(0, 0, rows).wait()
                return carry
            return body

        lax.fori_loop(0, total // BIG_COPY, wait_rows(BIG_COPY), 0)
        lax.fori_loop(0, (total % BIG_COPY) // GROUP_ALIGN, wait_rows(GROUP_ALIGN), 0)
        return

    def start_rows(table_ref, width, rows, priority):
        def body(j, carry):
            word = table_ref[tile * width + j]
            make_copy(pl.multiple_of(word & ((1 << ROW_BITS) - 1), GROUP_ALIGN),
                      pl.multiple_of(word >> ROW_BITS, GROUP_ALIGN), rows).start(priority=priority)
            return carry
        return body

    lax.fori_loop(0, cnt_ref[3 * tile], start_rows(big_ref, MAX_BIG, BIG_COPY, 0), 0)
    lax.fori_loop(0, cnt_ref[3 * tile + 1],
                  start_rows(small_ref, MAX_SMALL, GROUP_ALIGN, small_priority), 0)


def _dispatch_kernel(cnt_ref, big_ref, small_ref, fill_row_ref, fill_len_ref, nvalid_ref,
                     h_ref, idx_ref, rank_ref, wgt_ref, off_ref, xs_ref, loc_ref, z_ref, zero_ref,
                     sems, *, n_blocks):
    i = pl.program_id(0)
    last = pl.num_programs(0) - 1
    slot = i % 2
    tm = h_ref.shape[0]

    def copies(tile, buf, wait):
        def make_copy(local, glob, n):
            return pltpu.make_async_copy(z_ref.at[buf, pl.ds(local, n)],
                                         xs_ref.at[pl.ds(glob, n)], sems.at[buf])

        _tile_copies(tile, cnt_ref, big_ref, small_ref, make_copy, wait, small_priority=1)

    def fills(wait):
        def piece(row, n):
            cp = pltpu.make_async_copy(
                zero_ref.at[pl.ds(0, n)],
                xs_ref.at[pl.ds(pl.multiple_of(row, GROUP_ALIGN), n)], sems.at[2])
            if wait:
                cp.wait()
            else:
                cp.start()

        def expert_pad(e, carry):
            row, n = fill_row_ref[e], fill_len_ref[e]
            for size in FILL_SIZES:
                count = n // size

                def body(j, c, row=row, size=size):
                    piece(row + j * size, size)
                    return c

                lax.fori_loop(0, count, body, 0)
                row, n = row + count * size, n - count * size
            return carry

        lax.fori_loop(0, N_EXPERTS, expert_pad, 0)

        def tail_piece(j, carry):
            piece(nvalid_ref[0] * EXPERT_ROWS + j * FILL_SIZES[0], FILL_SIZES[0])
            return carry

        lax.fori_loop(0, (n_blocks - nvalid_ref[0]) * (EXPERT_ROWS // FILL_SIZES[0]),
                      tail_piece, 0)

    @pl.when(i == 0)
    def _():
        zero_ref[...] = jnp.zeros_like(zero_ref)
        fills(wait=False)

    e_iota = lax.broadcasted_iota(jnp.int32, (N_EXPERTS, tm), 0)
    off = off_ref[0][:, 0:1]
    locs = []
    for k in range(TOP_K):
        sel = e_iota == idx_ref[k:k + 1, :]
        loc = jnp.sum(jnp.where(sel, off, 0), axis=0, keepdims=True) + rank_ref[k:k + 1, :]
        loc_ref[k:k + 1, :] = loc
        locs.append(loc)

    @pl.when(i >= 2)
    def _():
        copies(i - 2, slot, wait=True)

    h = h_ref[...]
    half = h.shape[1] // 2
    rows = 256
    for c in range(COMPACT_ROWS // rows):
        r_iota = lax.broadcasted_iota(jnp.int32, (rows, tm), 0) + c * rows
        gate = jnp.zeros((rows, tm), F32)
        for k, loc in enumerate(locs):
            gate = jnp.where(r_iota == loc, wgt_ref[k:k + 1, :], gate)
        hit = jnp.where(gate != 0.0, 1.0, 0.0)
        z_ref[slot, c * rows:(c + 1) * rows, :half] = _pack_bf16_pairs(
            jnp.dot(hit.astype(BF16), h, preferred_element_type=F32))
        row_gate = jnp.sum(gate, axis=1, keepdims=True)
        z_ref[slot, c * rows:(c + 1) * rows, half:] = lax.bitcast_convert_type(
            jnp.broadcast_to(row_gate, (rows, LANES)), jnp.uint32)

    copies(i, slot, wait=False)

    @pl.when(i == last)
    def _():
        @pl.when(i >= 1)
        def _():
            copies(i - 1, 1 - slot, wait=True)

        copies(i, slot, wait=True)
        fills(wait=True)


def _dispatch(h, idx, rank, wgt, off_b, tables, fills, n_rows):
    t, d = h.shape
    tm = ROUTE_TILE
    width = d // 2 + LANES
    sel = pl.BlockSpec((TOP_K, tm), lambda i, *_: (0, i))
    grid_spec = pltpu.PrefetchScalarGridSpec(
        num_scalar_prefetch=6,
        grid=(t // tm,),
        in_specs=[
            pl.BlockSpec((tm, d), lambda i, *_: (i, 0)),
            sel, sel, sel,
            pl.BlockSpec((1, N_EXPERTS, LANES), lambda i, *_: (i, 0, 0)),
        ],
        out_specs=[pl.BlockSpec(memory_space=pl.ANY), sel],
        scratch_shapes=[pltpu.VMEM((2, COMPACT_ROWS, width), jnp.uint32),
                        pltpu.VMEM((FILL_SIZES[0], width), jnp.uint32),
                        pltpu.SemaphoreType.DMA((3,))],
    )
    return pl.pallas_call(
        functools.partial(_dispatch_kernel, n_blocks=n_rows // EXPERT_ROWS),
        grid_spec=grid_spec,
        out_shape=[jax.ShapeDtypeStruct((n_rows, width), jnp.uint32),
                   jax.ShapeDtypeStruct((TOP_K, t), jnp.int32)],
        compiler_params=_params(("arbitrary",)),
        name="moe_dispatch",
    )(*tables, *fills, h, idx, rank, wgt, off_b)


def _expert_kernel(be_ref, first_ref, live_ref, src_ref, nvalid_ref, xs_ref, wgu_ref, bgu_ref,
                   wd_ref, bd_ref, ys_ref, wgu_bf, wd_bf):
    del be_ref, src_ref, nvalid_ref
    b = pl.program_id(0)
    bm = xs_ref.shape[0]

    @pl.when(first_ref[b] == 1)
    def _():
        wgu_bf[...] = wgu_ref[0].astype(BF16)
        wd_bf[...] = wd_ref[0].astype(BF16)

    def ffn(rows):
        d_ff = wd_bf.shape[0]
        half = wd_bf.shape[1] // 2
        x = jnp.concatenate(_unpack_bf16_pairs(xs_ref[:rows, :half]), axis=1)
        row_gate = lax.bitcast_convert_type(xs_ref[:rows, half:], F32)
        gu = jnp.dot(x, wgu_bf[...], preferred_element_type=F32) + bgu_ref[0]
        gate = jnp.minimum(gu[:, :d_ff], SWIGLU_LIMIT)
        up = jnp.clip(gu[:, d_ff:], -SWIGLU_LIMIT, SWIGLU_LIMIT)
        glu = gate * jax.nn.sigmoid(SWIGLU_ALPHA * gate)
        act = ((up + 1.0) * glu).astype(BF16)
        y = jnp.dot(act, wd_bf[...], preferred_element_type=F32) + bd_ref[0]
        y = y * jnp.concatenate([row_gate] * (y.shape[1] // LANES), axis=1)
        ys_ref[:rows, :] = _pack_bf16_pairs(y.astype(BF16).astype(F32))

    for parts in range(bm // EXPERT_PART + 1):
        @pl.when(live_ref[b] == parts)
        def _():
            rows = parts * EXPERT_PART
            if rows > 0:
                ffn(rows)
            if rows < bm:
                ys_ref[rows:, :] = jnp.zeros((bm - rows, ys_ref.shape[1]), ys_ref.dtype)


def _experts(xs, block_expert, block_first, block_live, block_src, n_valid, layer,
             w_gu, b_gu, w_d, b_d):
    n_rows, in_width = xs.shape
    bm = EXPERT_ROWS
    d, d_ff = w_d.shape[3], w_d.shape[2]
    half = d // 2
    n_l = w_gu.shape[0]
    grid_spec = pltpu.PrefetchScalarGridSpec(
        num_scalar_prefetch=5,
        grid=(n_rows // bm,),
        in_specs=[
            pl.BlockSpec((bm, in_width),
                         lambda b, be, fi, lv, src, nv: (src[jnp.minimum(b, nv[0] - 1)], 0)),
            pl.BlockSpec((None, 1, d, 2 * d_ff), lambda b, be, *_: (layer, be[b], 0, 0)),
            pl.BlockSpec((None, 1, 1, 2 * d_ff), lambda b, be, *_: (layer, be[b], 0, 0)),
            pl.BlockSpec((None, 1, d_ff, d), lambda b, be, *_: (layer, be[b], 0, 0)),
            pl.BlockSpec((None, 1, 1, d), lambda b, be, *_: (layer, be[b], 0, 0)),
        ],
        out_specs=pl.BlockSpec((bm, half), lambda b, be, fi, lv, src, nv: (src[b], 0)),
        scratch_shapes=[pltpu.VMEM((d, 2 * d_ff), BF16), pltpu.VMEM((d_ff, d), BF16)],
    )
    return pl.pallas_call(
        _expert_kernel,
        grid_spec=grid_spec,
        out_shape=jax.ShapeDtypeStruct((n_rows, half), jnp.uint32),
        compiler_params=_params(("arbitrary",)),
        name="moe_experts",
    )(block_expert, block_first, block_live, block_src, n_valid, xs, w_gu,
      b_gu.reshape(n_l, N_EXPERTS, 1, 2 * d_ff), w_d, b_d.reshape(n_l, N_EXPERTS, 1, d))


def _combine_kernel(cnt_ref, big_ref, small_ref, loc_ref, ys_ref, x_ref, gate_ref,
                    *rest, final):
    fin_refs, (o_ref, y_ref, sems) = rest[:-3], rest[-3:]
    i = pl.program_id(0)
    slot = i % 2
    tm = x_ref.shape[0]

    def copies(tile, buf, wait):
        def make_copy(local, glob, n):
            return pltpu.make_async_copy(ys_ref.at[pl.ds(glob, n)],
                                         y_ref.at[buf, pl.ds(local, n)], sems.at[buf])

        _tile_copies(tile, cnt_ref, big_ref, small_ref, make_copy, wait)

    @pl.when(i == 0)
    def _():
        y_ref[...] = jnp.zeros_like(y_ref)
        copies(0, 0, wait=False)

    copies(i, slot, wait=True)

    @pl.when(i + 1 < pl.num_programs(0))
    def _():
        copies(i + 1, 1 - slot, wait=False)

    rows = COMBINE_CHUNK
    half = y_ref.shape[2]
    f_lo = jnp.zeros((tm, half), F32)
    f_hi = jnp.zeros((tm, half), F32)
    for c in range(COMPACT_ROWS // rows):
        c_iota = lax.broadcasted_iota(jnp.int32, (tm, rows), 1) + c * rows
        q = jnp.zeros((tm, rows), F32)
        for k in range(TOP_K):
            q = jnp.where(c_iota == loc_ref[:, k:k + 1], 1.0, q)
        q = q.astype(BF16)
        y_lo, y_hi = _unpack_bf16_pairs(y_ref[slot, c * rows:(c + 1) * rows, :])
        f_lo = f_lo + jnp.dot(q, y_lo, preferred_element_type=F32)
        f_hi = f_hi + jnp.dot(q, y_hi, preferred_element_type=F32)
    out = x_ref[...] + gate_ref[0] * jnp.concatenate([f_lo, f_hi], axis=1)
    if final:
        gain_ref, shift_ref, scale_ref = fin_refs
        out = _modulated_norm(out, gain_ref[...], shift_ref[0], scale_ref[0])
    o_ref[...] = out


def _combine(loc_tk, ys, x, gate, tables, final_mod, *, seq):
    t, d = x.shape
    tm = ROUTE_TILE
    n_s = seq // tm
    sel = pl.BlockSpec((tm, TOP_K), lambda i, *_: (i, 0))
    per_batch = pl.BlockSpec((1, 1, d), lambda i, *_: (i // n_s, 0, 0))
    in_specs = [sel, pl.BlockSpec(memory_space=pl.ANY),
                pl.BlockSpec((tm, d), lambda i, *_: (i, 0)), per_batch]
    extra = ()
    if final_mod is not None:
        gain, shift, scale = final_mod
        extra = (gain.reshape(1, d), shift, scale)
        in_specs += [pl.BlockSpec((1, d), lambda i, *_: (0, 0)), per_batch, per_batch]
    grid_spec = pltpu.PrefetchScalarGridSpec(
        num_scalar_prefetch=3,
        grid=(t // tm,),
        in_specs=in_specs,
        out_specs=pl.BlockSpec((tm, d), lambda i, *_: (i, 0)),
        scratch_shapes=[pltpu.VMEM((2, COMPACT_ROWS, d // 2), jnp.uint32),
                        pltpu.SemaphoreType.DMA((2,))],
    )
    return pl.pallas_call(
        functools.partial(_combine_kernel, final=final_mod is not None),
        grid_spec=grid_spec,
        out_shape=jax.ShapeDtypeStruct((t, d), F32),
        compiler_params=_params(("arbitrary",)),
        name="moe_combine",
    )(*tables, loc_tk, ys, x, gate, *extra)


def _moe(h, idx, wgt, rank, tile_cnt, x, gate, layer, w_gu, b_gu, w_d, b_d, final_mod, *, seq):
    t = h.shape[0]
    bm = EXPERT_ROWS
    n_tiles = t // ROUTE_TILE
    n_rows = t * TOP_K + n_tiles * N_EXPERTS * GROUP_ALIGN + N_EXPERTS * bm
    n_blocks = n_rows // bm
    cnt = tile_cnt[:, :, 0]
    grp = (cnt + GROUP_ALIGN - 1) // GROUP_ALIGN * GROUP_ALIGN
    expert_rows = jnp.sum(grp, axis=0)
    padded = (expert_rows + bm - 1) // bm * bm
    pad_end = jnp.cumsum(padded)
    tstart = (pad_end - padded)[None, :] + jnp.cumsum(grp, axis=0) - grp
    toff = jnp.cumsum(grp, axis=1) - grp
    tables = _copy_tables(grp, toff, tstart)
    off_b = jnp.broadcast_to(toff[:, :, None], (n_tiles, N_EXPERTS, LANES)).astype(jnp.int32)
    block_row = jnp.arange(n_blocks, dtype=jnp.int32) * bm
    block_expert = jnp.minimum(
        jnp.sum((pad_end[None, :] <= block_row[:, None]).astype(jnp.int32), axis=1),
        N_EXPERTS - 1).astype(jnp.int32)
    block_first = jnp.concatenate(
        [jnp.ones((1,), jnp.int32), (block_expert[1:] != block_expert[:-1]).astype(jnp.int32)])
    n_valid = (pad_end[-1:] // bm).astype(jnp.int32)
    in_expert = block_expert[:, None] == jnp.arange(N_EXPERTS, dtype=jnp.int32)[None, :]

    def of_expert(per_expert):
        return jnp.sum(jnp.where(in_expert, per_expert[None, :], 0), axis=1)

    first_block = of_expert((pad_end - padded) // bm)
    n_own = jnp.maximum(of_expert(padded // bm), 1)
    past_end = block_row >= pad_end[-1]
    block_id = jnp.arange(n_blocks, dtype=jnp.int32)
    block_src = jnp.where(past_end, block_id,
                          first_block + (block_id - first_block - 1) % n_own).astype(jnp.int32)
    rows_left = of_expert(pad_end - padded + expert_rows) - block_src * bm
    block_live = jnp.where(past_end, 0,
                           jnp.clip((rows_left + EXPERT_PART - 1) // EXPERT_PART, 1,
                                    bm // EXPERT_PART)).astype(jnp.int32)
    fills = ((pad_end - padded + expert_rows).astype(jnp.int32),
             (padded - expert_rows).astype(jnp.int32), n_valid)
    xs, loc = _dispatch(h, idx, rank, wgt, off_b, tables, fills, n_rows)
    ys = _experts(xs, block_expert, block_first, block_live, block_src, n_valid, layer,
                  w_gu, b_gu, w_d, b_d)
    return _combine(loc.T, ys, x, gate, tables, final_mod, seq=seq)


def kernel(x, c, mod_w, mod_b, mix_norm, ffn_norm, a_wqkv, a_wo, a_lambda, a_subln, kv_norm, kv_mod_w, kv_mod_b, kv_w, b_wq, b_wo, router_w, router_b, exp_w_gate_up, exp_b_gate_up, exp_w_down, exp_b_down, final_norm, final_mod_w, final_mod_b):
    n_b, seq, d = x.shape
    t = n_b * seq
    sizes = dict(n_b=n_b, seq=seq)
    rope_tabs = _rope_tables(seq)
    q_scale = HEAD_DIM ** -0.5 * math.log2(math.e)

    def per_batch(v):
        return v.reshape(n_b, 1, d)

    mod = _adaln_vectors(c, mod_w, mod_b)
    kv_mod = _adaln_vectors(c, kv_mod_w[None], kv_mod_b[None])[0]
    fin_mod = _adaln_vectors(c, final_mod_w[None], final_mod_b[None])[0]

    a_wqkv, a_wo, b_wq, b_wo = (w.astype(BF16) for w in (a_wqkv, a_wo, b_wq, b_wo))
    kv_w = kv_w[None].astype(BF16)

    xt = x.reshape(t, d)
    shared = None
    for layer in range(DEPTH):
        sh1, sc1, g1, sh2, sc2, g2 = (per_batch(mod[layer, :, j * d:(j + 1) * d]) for j in range(6))
        if layer < N_A_LAYERS:
            lambda_init = 0.8 - 0.6 * math.exp(-0.3 * layer)
            segs = ((0, True, q_scale, (1,)), (1, True, 1.0, (1,)), (2, False, 1.0, (1,)))
            q, k, v = _norm_project(xt, mix_norm[layer], sh1, sc1, a_wqkv, layer,
                                    rope_tabs, segs, name="a_qkv_proj", **sizes)
            o = _diff_attention(q, k, v, a_lambda[layer], a_subln[layer], lambda_init, **sizes)
            wo, wo_layer = a_wo, layer
        else:
            j = layer - N_A_LAYERS
            segs = tuple((g, True, q_scale, (B_DILATIONS[g],)) for g in range(3))
            q0, q1, q2 = _norm_project(xt, mix_norm[layer], sh1, sc1, b_wq, j,
                                       rope_tabs, segs, name="b_q_proj", **sizes)
            o = _dilated_attention(q0, q1, q2, *shared, **sizes)
            wo, wo_layer = b_wo, j
        xt, h, idx, wgt, rank, tile_cnt = _mix_out_and_route(
            o, wo, wo_layer, xt, g1, ffn_norm[layer], sh2, sc2, router_w[layer],
            router_b[layer], **sizes)
        final_mod = None
        if layer == DEPTH - 1:
            final_mod = (final_norm, per_batch(fin_mod[:, :d]), per_batch(fin_mod[:, d:]))
        xt = _moe(h, idx, wgt, rank, tile_cnt, xt, g2, layer, exp_w_gate_up, exp_b_gate_up,
                  exp_w_down, exp_b_down, final_mod, seq=seq)
        if layer == N_A_LAYERS - 1:
            segs = ((0, True, 1.0, B_DILATIONS), (1, False, 1.0, B_DILATIONS))
            k0, k1, k2, v0, v1, v2 = _norm_project(
                xt, kv_norm, per_batch(kv_mod[:, :d]), per_batch(kv_mod[:, d:]),
                kv_w, 0, rope_tabs, segs, name="shared_kv_proj", **sizes)
            shared = (k0, v0, k1, v1, k2, v2)
    return xt.reshape(n_b, seq, d)
```
